```python
import functools
import jax, jax.numpy as jnp
from jax import lax
import numpy as np

D_MODEL = 2048
BATCH = 4
SEQ = 2048
DEPTH = 2
DEC_BATCH = 16
DEC_SEQ = 32
PAST_LEN = 2048

CHUNK = 64
N_HEADS = 8
HEAD_DIM = 128
ATT_W = N_HEADS * HEAD_DIM
LRU_W = D_MODEL - ATT_W
LRU_BLOCKS = 8
LRU_BW = LRU_W // LRU_BLOCKS
CONV_W = 4
LRU_C = 8.0
N_EXPERTS = 64
TOP_K = 8
D_EXPERT = 512
D_SHARED = 512
ROUTE_SCALE = 2.5
MOE_BLOCK = 128
Q_BLOCK = 128
ATT_SCALE = HEAD_DIM ** -0.5
ALPHA = (2 * DEPTH) ** 0.25
BETA = (8 * DEPTH) ** -0.25
LN_EPS = 1e-5
RMS_EPS = 1e-6
IN_COLS = 3 * ATT_W + N_HEADS + 2 * LRU_W
IN_SPLITS = (ATT_W, 2 * ATT_W, 3 * ATT_W, 3 * ATT_W + N_HEADS, 3 * ATT_W + N_HEADS + LRU_W)

kernel_name = 'hybrid_fox_rglru_moe_stream_step'


def layer_norm(x, g, b):
    xf = x.astype(jnp.float32)
    mu = jnp.mean(xf, axis=-1, keepdims=True)
    var = jnp.mean(jnp.square(xf - mu), axis=-1, keepdims=True)
    return ((xf - mu) * lax.rsqrt(var + LN_EPS) * g + b).astype(x.dtype)


def rms_norm(x, g):
    xf = x.astype(jnp.float32)
    return (xf * lax.rsqrt(jnp.mean(xf * xf, axis=-1, keepdims=True) + RMS_EPS) * g).astype(x.dtype)


def fox_prompt(q, k, v, logf):
    S = q.shape[1]
    c = jnp.cumsum(logf, axis=1).transpose(0, 2, 1)
    outs = []
    for blk in range(S // Q_BLOCK):
        q0, q1 = blk * Q_BLOCK, (blk + 1) * Q_BLOCK
        s = jnp.einsum('bqhd,bkhd->bhqk', q[:, q0:q1], k[:, :q1],
                       preferred_element_type=jnp.float32) * ATT_SCALE
        s = s + c[:, :, q0:q1, None] - c[:, :, None, :q1]
        mask = jnp.arange(q0, q1)[:, None] >= jnp.arange(q1)[None, :]
        p = jax.nn.softmax(jnp.where(mask, s, -jnp.inf), axis=-1)
        outs.append(jnp.einsum('bhqk,bkhd->bqhd', p.astype(v.dtype), v[:, :q1]))
    return jnp.concatenate(outs, axis=1)


def fox_sample(q, k, v, logf, ck, cv, clogf):
    P, T = ck.shape[1], q.shape[1]
    c = jnp.cumsum(jnp.concatenate([clogf.astype(jnp.float32), logf], axis=1), axis=1).transpose(0, 2, 1)
    k_all = jnp.concatenate([ck.astype(k.dtype), k], axis=1)
    v_all = jnp.concatenate([cv.astype(v.dtype), v], axis=1)
    s = jnp.einsum('bqhd,bkhd->bhqk', q, k_all, preferred_element_type=jnp.float32) * ATT_SCALE
    s = s + c[:, :, P:, None] - c[:, :, None, :]
    mask = (P + jnp.arange(T))[:, None] >= jnp.arange(P + T)[None, :]
    p = jax.nn.softmax(jnp.where(mask, s, -jnp.inf), axis=-1)
    return jnp.einsum('bhqk,bkhd->bqhd', p.astype(v.dtype), v_all)


def causal_conv(x, buf, w, b):
    S = x.shape[1]
    xp = jnp.concatenate([buf.astype(x.dtype), x], axis=1)
    y = b + sum(xp[:, i:i + S] * w[i] for i in range(CONV_W))
    return y, xp[:, -(CONV_W - 1):]


def block_diag(x, w, b):
    xb = x.reshape(*x.shape[:-1], LRU_BLOCKS, LRU_BW)
    return (jnp.einsum('bsnk,nkj->bsnj', xb, w) + b).reshape(x.shape)


def rg_lru(x, h0, w_r, b_r, w_i, b_i, lam):
    r = jax.nn.sigmoid(block_diag(x, w_r, b_r).astype(jnp.float32))
    i = jax.nn.sigmoid(block_diag(x, w_i, b_i).astype(jnp.float32))
    log_a = -LRU_C * r * jax.nn.softplus(-lam.astype(jnp.float32))
    a = jnp.exp(log_a)
    u = jnp.sqrt(-jnp.expm1(2.0 * log_a)) * i * x.astype(jnp.float32)

    def step(h, au):
        h = au[0] * h + au[1]
        return h, h

    h_last, hs = lax.scan(step, h0.astype(jnp.float32), (a.swapaxes(0, 1), u.swapaxes(0, 1)))
    return hs.swapaxes(0, 1).astype(x.dtype), h_last.astype(x.dtype)


def mixing_sublayer(h, attend, conv_buf, lru_h0, w_in, b_f, w_out, g_att, g_lru,
                    conv_w, conv_b, w_r, b_r, w_i, b_i, lru_lambda):
    B, S, _ = h.shape
    q, k, v, f_logit, x_r, x_g = jnp.split(h @ w_in, IN_SPLITS, axis=-1)
    q, k, v = (t.reshape(B, S, N_HEADS, HEAD_DIM) for t in (q, k, v))
    logf = jax.nn.log_sigmoid((f_logit + b_f).astype(jnp.float32))
    o_att = attend(q, k, v, logf).reshape(B, S, ATT_W)
    x_c, new_buf = causal_conv(x_r, conv_buf, conv_w, conv_b)
    h_seq, h_last = rg_lru(x_c, lru_h0, w_r, b_r, w_i, b_i, lru_lambda)
    o_lru = h_seq * jax.nn.gelu(x_g)
    o = jnp.concatenate([rms_norm(o_att, g_att), rms_norm(o_lru, g_lru)], axis=-1)
    return o @ w_out, (k, v, logf.astype(h.dtype), new_buf, h_last)


def swiglu(x, wg, wu, wd):
    return (jax.nn.silu(x @ wg) * (x @ wu)) @ wd


def moe_ffn(h, w_router, b_router, we_gate, we_up, we_down, ws_gate, ws_up, ws_down):
    shp = h.shape
    x = h.reshape(-1, D_MODEL)
    T = x.shape[0]
    scores = jax.nn.sigmoid((x @ w_router).astype(jnp.float32))
    _, idx = lax.top_k(scores + b_router.astype(jnp.float32), TOP_K)
    gate = jnp.take_along_axis(scores, idx, axis=-1)
    gate = gate / jnp.sum(gate, axis=-1, keepdims=True) * ROUTE_SCALE
    n_assign = T * TOP_K
    e_flat = idx.reshape(-1)
    order = jnp.argsort(e_flat)
    e_sorted = e_flat[order]
    tok_sorted = order // TOP_K
    counts = jnp.bincount(e_flat, length=N_EXPERTS)
    padded = (counts + MOE_BLOCK - 1) // MOE_BLOCK * MOE_BLOCK
    pad_end = jnp.cumsum(padded)
    pad_start = pad_end - padded
    start = jnp.cumsum(counts) - counts
    dest = pad_start[e_sorted] + jnp.arange(n_assign) - start[e_sorted]
    n_blocks = -(-n_assign // MOE_BLOCK) + N_EXPERTS
    n_rows = n_blocks * MOE_BLOCK
    row_tok = jnp.full((n_rows,), T, jnp.int32).at[dest].set(tok_sorted.astype(jnp.int32))
    blk_expert = jnp.minimum(
        jnp.searchsorted(pad_end, jnp.arange(n_blocks) * MOE_BLOCK, side='right'), N_EXPERTS - 1)
    x_pad = jnp.concatenate([x, jnp.zeros((1, D_MODEL), x.dtype)], axis=0)
    xs = x_pad[row_tok].reshape(n_blocks, MOE_BLOCK, D_MODEL)

    def expert_block(args):
        xb, e = args
        return swiglu(xb, we_gate[e], we_up[e], we_down[e])

    ys = lax.map(expert_block, (xs, blk_expert)).reshape(n_rows, D_MODEL)
    g_sorted = gate.reshape(-1)[order].astype(x.dtype)
    y = jnp.zeros_like(x).at[tok_sorted].add(ys[dest] * g_sorted[:, None])
    y = y + swiglu(x, ws_gate, ws_up, ws_down)
    return y.reshape(shp)


def setup_inputs(seed: int = 0) -> dict:
    key = jax.random.key(seed)
    keys = iter(jax.random.split(key, 40))

    def nrm(shape, scale=1.0):
        return scale * jax.random.normal(next(keys), shape, jnp.float32)

    a0 = jax.random.uniform(next(keys), (DEPTH, LRU_W), jnp.float32, 0.9, 0.999)
    s0 = a0 ** (1.0 / LRU_C)
    lru_lambda = jnp.log(s0) - jnp.log1p(-s0)
    return {
        'x_prompt': nrm((BATCH, SEQ, D_MODEL)),
        'x_sample': nrm((DEC_BATCH, DEC_SEQ, D_MODEL)),
        'cache_k': nrm((DEPTH, DEC_BATCH, PAST_LEN, N_HEADS, HEAD_DIM)),
        'cache_v': nrm((DEPTH, DEC_BATCH, PAST_LEN, N_HEADS, HEAD_DIM)),
        'cache_logf': jax.nn.log_sigmoid(3.0 + nrm((DEPTH, DEC_BATCH, PAST_LEN, N_HEADS))),
        'state_conv': nrm((DEPTH, DEC_BATCH, CONV_W - 1, LRU_W)),
        'state_lru': nrm((DEPTH, DEC_BATCH, LRU_W), 0.5),
        'ln_in_g': 1.0 + nrm((D_MODEL,), 0.02),
        'ln_in_b': nrm((D_MODEL,), 0.02),
        'w_in': nrm((DEPTH, D_MODEL, IN_COLS), D_MODEL ** -0.5),
        'b_f': 3.0 + nrm((DEPTH, N_HEADS), 0.1),
        'w_out': nrm((DEPTH, D_MODEL, D_MODEL), BETA * D_MODEL ** -0.5),
        'g_att': 1.0 + nrm((DEPTH, ATT_W), 0.02),
        'g_lru': 1.0 + nrm((DEPTH, LRU_W), 0.02),
        'conv_w': nrm((DEPTH, CONV_W, LRU_W), CONV_W ** -0.5),
        'conv_b': nrm((DEPTH, LRU_W), 0.02),
        'w_r': nrm((DEPTH, LRU_BLOCKS, LRU_BW, LRU_BW), LRU_BW ** -0.5),
        'b_r': nrm((DEPTH, LRU_BLOCKS, LRU_BW), 0.02),
        'w_i': nrm((DEPTH, LRU_BLOCKS, LRU_BW, LRU_BW), LRU_BW ** -0.5),
        'b_i': nrm((DEPTH, LRU_BLOCKS, LRU_BW), 0.02),
        'lru_lambda': lru_lambda,
        'ln1_g': 1.0 + nrm((DEPTH, D_MODEL), 0.02),
        'ln1_b': nrm((DEPTH, D_MODEL), 0.02),
        'w_router': nrm((DEPTH, D_MODEL, N_EXPERTS), D_MODEL ** -0.5),
        'b_router': nrm((DEPTH, N_EXPERTS), 0.01),
        'we_gate': nrm((DEPTH, N_EXPERTS, D_MODEL, D_EXPERT), D_MODEL ** -0.5),
        'we_up': nrm((DEPTH, N_EXPERTS, D_MODEL, D_EXPERT), D_MODEL ** -0.5),
        'we_down': nrm((DEPTH, N_EXPERTS, D_EXPERT, D_MODEL), BETA * D_EXPERT ** -0.5),
        'ws_gate': nrm((DEPTH, D_MODEL, D_SHARED), D_MODEL ** -0.5),
        'ws_up': nrm((DEPTH, D_MODEL, D_SHARED), D_MODEL ** -0.5),
        'ws_down': nrm((DEPTH, D_SHARED, D_MODEL), BETA * D_SHARED ** -0.5),
        'ln2_g': 1.0 + nrm((DEPTH, D_MODEL), 0.02),
        'ln2_b': nrm((DEPTH, D_MODEL), 0.02),
    }


def reference(x_prompt, x_sample, cache_k, cache_v, cache_logf, state_conv, state_lru,
              ln_in_g, ln_in_b, w_in, b_f, w_out, g_att, g_lru, conv_w, conv_b,
              w_r, b_r, w_i, b_i, lru_lambda, ln1_g, ln1_b, w_router, b_router,
              we_gate, we_up, we_down, ws_gate, ws_up, ws_down, ln2_g, ln2_b):
    def run(x, attends, conv_bufs, lru_h0s):
        x = layer_norm(x, ln_in_g, ln_in_b)
        ks, vs, lfs, bufs, hs = [], [], [], [], []
        for l in range(DEPTH):
            m, (k, v, lf, buf, h_last) = mixing_sublayer(
                x, attends[l], conv_bufs[l], lru_h0s[l], w_in[l], b_f[l], w_out[l], g_att[l], g_lru[l],
                conv_w[l], conv_b[l], w_r[l], b_r[l], w_i[l], b_i[l], lru_lambda[l])
            x = layer_norm(ALPHA * x + m, ln1_g[l], ln1_b[l])
            f = moe_ffn(x, w_router[l], b_router[l], we_gate[l], we_up[l], we_down[l],
                        ws_gate[l], ws_up[l], ws_down[l])
            x = layer_norm(ALPHA * x + f, ln2_g[l], ln2_b[l])
            ks.append(k); vs.append(v); lfs.append(lf); bufs.append(buf); hs.append(h_last)
        return x, jnp.stack(ks), jnp.stack(vs), jnp.stack(lfs), jnp.stack(bufs), jnp.stack(hs)

    B = x_prompt.shape[0]
    zero_buf = jnp.zeros((B, CONV_W - 1, LRU_W), x_prompt.dtype)
    zero_h = jnp.zeros((B, LRU_W), jnp.float32)
    y_prompt, p_k, p_v, p_logf, p_conv, p_lru = run(
        x_prompt, [fox_prompt] * DEPTH, [zero_buf] * DEPTH, [zero_h] * DEPTH)

    sample_attends = [functools.partial(fox_sample, ck=cache_k[l], cv=cache_v[l], clogf=cache_logf[l])
                      for l in range(DEPTH)]
    y_sample, s_k, s_v, s_logf, s_conv, s_lru = run(
        x_sample, sample_attends, [state_conv[l] for l in range(DEPTH)], [state_lru[l] for l in range(DEPTH)])

    return (y_prompt, y_sample, p_k, p_v, p_logf, p_conv, p_lru, s_k, s_v, s_logf, s_conv, s_lru)
```

```python
import functools

import jax
import jax.numpy as jnp
from jax import lax
from jax.experimental import pallas as pl
from jax.experimental.pallas import tpu as pltpu

F32 = jnp.float32
BF16 = jnp.bfloat16
I32 = jnp.int32

D_MODEL = 2048
N_HEADS = 8
HEAD_DIM = 128
ATT_W = N_HEADS * HEAD_DIM
LRU_W = D_MODEL - ATT_W
LRU_BLOCKS = 8
LRU_BW = LRU_W // LRU_BLOCKS
CONV_W = 4
LRU_C = 8.0
N_EXPERTS = 64
TOP_K = 8
D_EXPERT = 512
D_SHARED = 512
ROUTE_SCALE = 2.5
ATT_SCALE = HEAD_DIM ** -0.5
LN_EPS = 1e-5
RMS_EPS = 1e-6

LANES = 128
SUBLANES = 8
ROW_CHUNKS = D_MODEL // LANES
VMEM_LIMIT = 56 * 1024 * 1024

TM = 256
MOE_BLK = 256
TM_COMBINE = 128
ATT_TQ = 256
ATT_TK = 512
LRU_TS = 256
CUM_CHUNK = 128

C_Q, C_K, C_V, C_XR, C_XG, C_F, C_END = 0, 1024, 2048, 3072, 4096, 5120, 5248


def _params(n_axes=1):
    return pltpu.CompilerParams(dimension_semantics=("arbitrary",) * n_axes,
                                vmem_limit_bytes=VMEM_LIMIT)


def _resident(shape):
    nd = len(shape)
    return pl.BlockSpec(shape, lambda *_: (0,) * nd, pipeline_mode=pl.Buffered(1))


def _layer_norm(x, g, b):
    mu = jnp.mean(x, axis=-1, keepdims=True)
    xc = x - mu
    var = jnp.mean(xc * xc, axis=-1, keepdims=True)
    return xc * lax.rsqrt(var + LN_EPS) * g + b


def _rms_norm(x, g):
    return x * lax.rsqrt(jnp.mean(x * x, axis=-1, keepdims=True) + RMS_EPS) * g


def _silu(x):
    return x * jax.nn.sigmoid(x)


def _gelu_tanh(x):
    return 0.5 * x * (1.0 + jnp.tanh(0.7978845608028654 * (x + 0.044715 * (x * x * x))))


def _split3(x):
    hi = x.astype(BF16)
    r1 = x - hi.astype(F32)
    mid = r1.astype(BF16)
    lo = (r1 - mid.astype(F32)).astype(BF16)
    return hi, mid, lo


def _inproj_kernel(apply_ln, x_ref, g_ref, b_ref, w_ref, bf_ref, *outs):
    if apply_ln:
        x0_ref, outs = outs[0], outs[1:]
    q_ref, k_ref, v_ref, kb_ref, vb_ref, xr_ref, xg_ref, lf_ref = outs
    x = x_ref[...]
    if apply_ln:
        x = _layer_norm(x, g_ref[...], b_ref[...])
        x0_ref[...] = x
    h = x.astype(BF16)

    def mm(c0, c1):
        return jnp.dot(h, w_ref[:, c0:c1], preferred_element_type=F32)

    q_ref[...] = mm(C_Q, C_K).astype(BF16)
    k = mm(C_K, C_V)
    k_ref[...] = k
    kb_ref[...] = k.astype(BF16)
    v = mm(C_V, C_XR)
    v_ref[...] = v
    vb_ref[...] = v.astype(BF16)
    xr_ref[...] = mm(C_XR, C_XG)
    xg_ref[...] = mm(C_XG, C_F)
    f = mm(C_F, C_END) + bf_ref[...]
    lf_ref[...] = jnp.minimum(f, 0.0) - jnp.log1p(jnp.exp(-jnp.abs(f)))


def _inproj(x, ln_g, ln_b, w_pad, bf_pad, apply_ln):
    T = x.shape[0]
    row = lambda w: pl.BlockSpec((TM, w), lambda i: (i, 0))
    out_shape = [
        jax.ShapeDtypeStruct((T, ATT_W), BF16),
        jax.ShapeDtypeStruct((T, ATT_W), F32),
        jax.ShapeDtypeStruct((T, ATT_W), F32),
        jax.ShapeDtypeStruct((T, ATT_W), BF16),
        jax.ShapeDtypeStruct((T, ATT_W), BF16),
        jax.ShapeDtypeStruct((T, LRU_W), F32),
        jax.ShapeDtypeStruct((T, LRU_W), F32),
        jax.ShapeDtypeStruct((T, LANES), F32),
    ]
    out_specs = [row(ATT_W)] * 5 + [row(LRU_W)] * 2 + [row(LANES)]
    if apply_ln:
        out_shape = [jax.ShapeDtypeStruct((T, D_MODEL), F32)] + out_shape
        out_specs = [row(D_MODEL)] + out_specs
    return pl.pallas_call(
        functools.partial(_inproj_kernel, apply_ln),
        grid=(T // TM,),
        in_specs=[row(D_MODEL), _resident((1, D_MODEL)), _resident((1, D_MODEL)),
                  _resident((D_MODEL, C_END)), _resident((1, LANES))],
        out_specs=out_specs,
        out_shape=out_shape,
        compiler_params=_params(),
        name="inproj",
    )(x, ln_g, ln_b, w_pad, bf_pad)


def _cumsum_kernel(lf_ref, c_ref, *, S):
    carry = jnp.zeros((1, LANES), F32)
    for c0 in range(0, S, CUM_CHUNK):
        n = min(CUM_CHUNK, S - c0)
        r = lax.broadcasted_iota(I32, (n, n), 0)
        c = lax.broadcasted_iota(I32, (n, n), 1)
        tril = (r >= c).astype(BF16)
        hi, mid, lo = _split3(lf_ref[c0:c0 + n, :])
        acc = jnp.dot(tril, hi, preferred_element_type=F32)
        acc = acc + jnp.dot(tril, mid, preferred_element_type=F32)
        acc = acc + jnp.dot(tril, lo, preferred_element_type=F32)
        acc = acc + carry
        c_ref[c0:c0 + n, :] = acc
        carry = acc[n - 1:n, :]


def _cumsum(lf):
    B, S, _ = lf.shape
    spec = pl.BlockSpec((None, S, LANES), lambda b: (b, 0, 0))
    return pl.pallas_call(
        functools.partial(_cumsum_kernel, S=S),
        grid=(B,),
        in_specs=[spec],
        out_specs=spec,
        out_shape=jax.ShapeDtypeStruct((B, S, LANES), F32),
        compiler_params=_params(),
        name="cumsum_logf",
    )(lf)


def _attn_prompt_kernel(q_ref, k_ref, v_ref, c_ref, ct_ref, o_ref, *, S):
    nq = S // ATT_TQ
    for h in range(N_HEADS):
        hs = slice(h * HEAD_DIM, (h + 1) * HEAD_DIM)

        def q_body(qi, carry, h=h, hs=hs):
            q0 = pl.multiple_of(qi * ATT_TQ, ATT_TQ)
            q = q_ref[pl.ds(q0, ATT_TQ), hs]
            cq = c_ref[pl.ds(q0, ATT_TQ), h:h + 1]
            row = q0 + lax.broadcasted_iota(I32, (ATT_TQ, ATT_TK), 0)

            def kv_body(kj, st):
                m, l, acc = st
                k0 = pl.multiple_of(kj * ATT_TK, ATT_TK)
                k = k_ref[pl.ds(k0, ATT_TK), hs]
                v = v_ref[pl.ds(k0, ATT_TK), hs]
                ck = ct_ref[h, pl.ds(kj, 1), :]
                s = lax.dot_general(q, k, (((1,), (1,)), ((), ())),
                                    preferred_element_type=F32) * ATT_SCALE
                s = s + cq - ck
                col = k0 + lax.broadcasted_iota(I32, (ATT_TQ, ATT_TK), 1)
                s = jnp.where(row >= col, s, -jnp.inf)
                m_new = jnp.maximum(m, jnp.max(s, axis=1, keepdims=True))
                alpha = jnp.exp(m - m_new)
                p = jnp.exp(s - m_new)
                l = alpha * l + jnp.sum(p, axis=1, keepdims=True)
                acc = alpha * acc + jnp.dot(p.astype(BF16), v, preferred_element_type=F32)
                return m_new, l, acc

            nkv = (q0 + ATT_TQ + ATT_TK - 1) // ATT_TK
            init = (jnp.full((ATT_TQ, 1), -jnp.inf, F32), jnp.zeros((ATT_TQ, 1), F32),
                    jnp.zeros((ATT_TQ, HEAD_DIM), F32))
            m, l, acc = lax.fori_loop(0, nkv, kv_body, init)
            o_ref[pl.ds(q0, ATT_TQ), hs] = acc / l
            return carry

        lax.fori_loop(0, nq, q_body, 0)


def _attn_prompt(qb, kb, vb, c, ct, B, S):
    T = qb.shape[0]
    seq = pl.BlockSpec((S, ATT_W), lambda b: (b, 0))
    return pl.pallas_call(
        functools.partial(_attn_prompt_kernel, S=S),
        grid=(B,),
        in_specs=[seq, seq, seq,
                  pl.BlockSpec((None, S, LANES), lambda b: (b, 0, 0)),
                  pl.BlockSpec((None, N_HEADS, S // ATT_TK, ATT_TK), lambda b: (b, 0, 0, 0))],
        out_specs=seq,
        out_shape=jax.ShapeDtypeStruct((T, ATT_W), F32),
        compiler_params=_params(),
        name="fox_prompt",
    )(qb, kb, vb, c, ct)


def _attn_sample_kernel(q_ref, kn_ref, vn_ref, ck_ref, cv_ref, c_ref, ct_ref, o_in_ref, o_ref,
                        *, P, Tn):
    del o_in_ref
    r = lax.broadcasted_iota(I32, (Tn, Tn), 0)
    c = lax.broadcasted_iota(I32, (Tn, Tn), 1)
    nt = (((1,), (1,)), ((), ()))
    for h in range(N_HEADS):
        hs = slice(h * HEAD_DIM, (h + 1) * HEAD_DIM)
        q = q_ref[:, hs]
        kc = ck_ref[:, hs].astype(BF16)
        vc = cv_ref[:, hs].astype(BF16)
        cq = c_ref[P:P + Tn, h:h + 1]
        ckr = ct_ref[h:h + 1, :]
        s_c = lax.dot_general(q, kc, nt, preferred_element_type=F32) * ATT_SCALE
        s_c = s_c + cq - ckr[:, :P]
        s_n = lax.dot_general(q, kn_ref[:, hs], nt, preferred_element_type=F32) * ATT_SCALE
        s_n = jnp.where(r >= c, s_n + cq - ckr[:, P:P + Tn], -jnp.inf)
        m = jnp.maximum(jnp.max(s_c, axis=1, keepdims=True), jnp.max(s_n, axis=1, keepdims=True))
        p_c = jnp.exp(s_c - m)
        p_n = jnp.exp(s_n - m)
        l = jnp.sum(p_c, axis=1, keepdims=True) + jnp.sum(p_n, axis=1, keepdims=True)
        o = jnp.dot(p_c.astype(BF16), vc, preferred_element_type=F32)
        o = o + jnp.dot(p_n.astype(BF16), vn_ref[:, hs], preferred_element_type=F32)
        o_ref[:, hs] = o / l


def _attn_sample(qb, kb, vb, cache_k, cache_v, layer, c, ct, o_att, row0, Bd, P, Tn):
    T = qb.shape[0]
    blk0 = row0 // Tn
    new = pl.BlockSpec((Tn, ATT_W), lambda b: (blk0 + b, 0))
    cache = pl.BlockSpec((None, None, P, ATT_W), lambda b: (layer, b, 0, 0))
    return pl.pallas_call(
        functools.partial(_attn_sample_kernel, P=P, Tn=Tn),
        grid=(Bd,),
        in_specs=[new, new, new, cache, cache,
                  pl.BlockSpec((None, P + Tn, LANES), lambda b: (b, 0, 0)),
                  pl.BlockSpec((None, N_HEADS, P + Tn), lambda b: (b, 0, 0)),
                  pl.BlockSpec(memory_space=pl.ANY)],
        out_specs=new,
        out_shape=jax.ShapeDtypeStruct((T, ATT_W), F32),
        input_output_aliases={7: 0},
        compiler_params=_params(),
        name="fox_sample",
    )(qb, kb, vb, cache_k, cache_v, c, ct, o_att)


def _lru_kernel(xr_ref, xg_ref, cs_ref, h0_ref, cw_ref, cb_ref, wr_ref, br_ref, wi_ref, bi_ref,
                lam_ref, *rest, ts, aliased):
    if aliased:
        rest = rest[1:]
    o_ref, cso_ref, hl_ref, xp_ref, a_ref, u_ref, hs_ref, h_ref = rest
    s = pl.program_id(1)
    pad = SUBLANES
    nprev = CONV_W - 1

    @pl.when(s == 0)
    def _():
        xp_ref[pad - nprev:pad, :] = cs_ref[...]
        h_ref[...] = h0_ref[...]

    x = xr_ref[...]
    xp_ref[pad:pad + ts, :] = x
    xc = cb_ref[...] + cw_ref[nprev:nprev + 1, :] * x
    for i in range(nprev):
        xc = xc + cw_ref[i:i + 1, :] * xp_ref[pad - nprev + i:pad - nprev + i + ts, :]
    tail = xp_ref[pad + ts - nprev:pad + ts, :]
    xp_ref[pad - nprev:pad, :] = tail
    cso_ref[...] = tail

    xcb = xc.astype(BF16)

    def block_diag(w_ref, b_ref):
        parts = [jnp.dot(xcb[:, n * LRU_BW:(n + 1) * LRU_BW], w_ref[n], preferred_element_type=F32)
                 for n in range(LRU_BLOCKS)]
        return jnp.concatenate(parts, axis=1) + b_ref[...]

    r = jax.nn.sigmoid(block_diag(wr_ref, br_ref))
    g = jax.nn.sigmoid(block_diag(wi_ref, bi_ref))
    nl = -lam_ref[...]
    softplus = jnp.maximum(nl, 0.0) + jnp.log1p(jnp.exp(-jnp.abs(nl)))
    log_a = (-LRU_C) * r * softplus
    a = jnp.exp(log_a)
    u = jnp.sqrt(1.0 - a * a) * g * xc

    sub = lax.broadcasted_iota(I32, (ts, LRU_W), 0) % SUBLANES
    d = 1
    while d < SUBLANES:
        keep = sub >= d
        u = jnp.where(keep, u + a * pltpu.roll(u, d, axis=0), u)
        a = jnp.where(keep, a * pltpu.roll(a, d, axis=0), a)
        d *= 2
    a_ref[...] = a
    u_ref[...] = u

    def group(gi, hprev):
        r0 = pl.multiple_of(gi * SUBLANES, SUBLANES)
        hh = u_ref[pl.ds(r0, SUBLANES), :] + a_ref[pl.ds(r0, SUBLANES), :] * hprev
        hs_ref[pl.ds(r0, SUBLANES), :] = hh
        return hh[SUBLANES - 1:SUBLANES, :]

    hlast = lax.fori_loop(0, ts // SUBLANES, group, h_ref[...])
    h_ref[...] = hlast
    hl_ref[...] = hlast
    o_ref[...] = hs_ref[...] * _gelu_tanh(xg_ref[...])


def _lru(xr, xg, conv_state, h0, lw, o_prev, row0, nb, S, ts):
    T = xr.shape[0]
    ns = S // ts
    blk0 = row0 // ts
    rows = pl.BlockSpec((ts, LRU_W), lambda b, s: (blk0 + b * ns + s, 0))
    vec = lambda n: _resident((n, LRU_W))
    per_seq = lambda n: pl.BlockSpec((None, n, LRU_W), lambda b, s: (b, 0, 0))
    aliased = o_prev is not None
    in_specs = [rows, rows, per_seq(CONV_W - 1), per_seq(1), vec(CONV_W), vec(1),
                _resident((LRU_BLOCKS, LRU_BW, LRU_BW)), vec(1),
                _resident((LRU_BLOCKS, LRU_BW, LRU_BW)), vec(1), vec(1)]
    args = [xr, xg, conv_state, h0, lw["conv_w"], lw["conv_b"], lw["w_r"], lw["b_r"],
            lw["w_i"], lw["b_i"], lw["lam"]]
    kwargs = {}
    if aliased:
        in_specs.append(pl.BlockSpec(memory_space=pl.ANY))
        args.append(o_prev)
        kwargs["input_output_aliases"] = {len(args) - 1: 0}
    return pl.pallas_call(
        functools.partial(_lru_kernel, ts=ts, aliased=aliased),
        grid=(nb, ns),
        in_specs=in_specs,
        out_specs=[rows, per_seq(CONV_W - 1), per_seq(1)],
        out_shape=[jax.ShapeDtypeStruct((T, LRU_W), F32),
                   jax.ShapeDtypeStruct((nb, CONV_W - 1, LRU_W), F32),
                   jax.ShapeDtypeStruct((nb, 1, LRU_W), F32)],
        scratch_shapes=[pltpu.VMEM((SUBLANES + ts, LRU_W), F32),
                        pltpu.VMEM((ts, LRU_W), F32), pltpu.VMEM((ts, LRU_W), F32),
                        pltpu.VMEM((ts, LRU_W), F32), pltpu.VMEM((1, LRU_W), F32)],
        compiler_params=_params(2),
        name="rg_lru",
        **kwargs,
    )(*args)


def _outproj_kernel(oa_ref, ol_ref, x_ref, w_ref, ga_ref, gl_ref, lg_ref, lb_ref,
                    wrh_ref, wrl_ref, rb_ref, x1_ref, idx_ref, gate_ref, *, alpha):
    na = _rms_norm(oa_ref[...], ga_ref[...]).astype(BF16)
    nl = _rms_norm(ol_ref[...], gl_ref[...]).astype(BF16)
    m = jnp.dot(na, w_ref[0:ATT_W, :], preferred_element_type=F32)
    m = m + jnp.dot(nl, w_ref[ATT_W:D_MODEL, :], preferred_element_type=F32)
    x1 = _layer_norm(alpha * x_ref[...] + m, lg_ref[...], lb_ref[...])
    x1_ref[...] = x1

    xh = x1.astype(BF16)
    xl = (x1 - xh.astype(F32)).astype(BF16)
    logits = jnp.dot(xh, wrh_ref[...], preferred_element_type=F32)
    logits = logits + jnp.dot(xl, wrh_ref[...], preferred_element_type=F32)
    logits = logits + jnp.dot(xh, wrl_ref[...], preferred_element_type=F32)
    scores = jax.nn.sigmoid(logits)
    tm = scores.shape[0]
    lane = lax.broadcasted_iota(I32, (tm, LANES), 1)
    lane_f = lane.astype(F32)
    sel = jnp.where(lane < N_EXPERTS, scores + rb_ref[...], -jnp.inf)
    idx_out = jnp.zeros((tm, LANES), F32)
    gate_out = jnp.zeros((tm, LANES), F32)
    gsum = jnp.zeros((tm, 1), F32)
    for k in range(TOP_K):
        mx = jnp.max(sel, axis=1, keepdims=True)
        ik = jnp.min(jnp.where(sel == mx, lane_f, float(LANES)), axis=1, keepdims=True)
        hit = lane_f == ik
        gk = jnp.sum(jnp.where(hit, scores, 0.0), axis=1, keepdims=True)
        sel = jnp.where(hit, -jnp.inf, sel)
        idx_out = jnp.where(lane == k, ik, idx_out)
        gate_out = jnp.where(lane == k, gk, gate_out)
        gsum = gsum + gk
    idx_ref[...] = idx_out.astype(I32)
    gate_ref[...] = gate_out / gsum * ROUTE_SCALE


def _outproj(o_att, o_lru, x, lw, alpha):
    T = x.shape[0]
    row = lambda w: pl.BlockSpec((TM, w), lambda i: (i, 0))
    vec = lambda n: _resident((1, n))
    return pl.pallas_call(
        functools.partial(_outproj_kernel, alpha=alpha),
        grid=(T // TM,),
        in_specs=[row(ATT_W), row(LRU_W), row(D_MODEL), _resident((D_MODEL, D_MODEL)),
                  vec(ATT_W), vec(LRU_W), vec(D_MODEL), vec(D_MODEL),
                  _resident((D_MODEL, LANES)), _resident((D_MODEL, LANES)), vec(LANES)],
        out_specs=[row(D_MODEL), row(LANES), row(LANES)],
        out_shape=[jax.ShapeDtypeStruct((T, D_MODEL), F32),
                   jax.ShapeDtypeStruct((T, LANES), I32),
                   jax.ShapeDtypeStruct((T, LANES), F32)],
        compiler_params=_params(),
        name="outproj_ln_router",
    )(o_att, o_lru, x, lw["w_out"], lw["g_att"], lw["g_lru"], lw["ln1_g"], lw["ln1_b"],
      lw["wr_hi"], lw["wr_lo"], lw["b_router"])


def _rank_kernel(idx_ref, rank_ref, cnt_ref, carry_ref):
    @pl.when(pl.program_id(0) == 0)
    def _():
        carry_ref[...] = jnp.zeros_like(carry_ref)

    tm = idx_ref.shape[0]
    idx = idx_ref[...]
    lane = lax.broadcasted_iota(I32, (tm, LANES), 1)
    onehot = jnp.zeros((tm, LANES), F32)
    for k in range(TOP_K):
        onehot = onehot + (lane == idx[:, k:k + 1]).astype(F32)
    r = lax.broadcasted_iota(I32, (tm, tm), 0)
    c = lax.broadcasted_iota(I32, (tm, tm), 1)
    below = (r > c).astype(BF16)
    rank = jnp.dot(below, onehot.astype(BF16), preferred_element_type=F32) + carry_ref[...]
    out = jnp.zeros((tm, LANES), F32)
    for k in range(TOP_K):
        rk = jnp.sum(jnp.where(lane == idx[:, k:k + 1], rank, 0.0), axis=1, keepdims=True)
        out = jnp.where(lane == k, rk, out)
    rank_ref[...] = out.astype(I32)
    total = carry_ref[...] + jnp.sum(onehot, axis=0, keepdims=True)
    carry_ref[...] = total
    cnt_ref[...] = total.astype(I32)


def _rank(idx):
    T = idx.shape[0]
    row = pl.BlockSpec((TM, LANES), lambda i: (i, 0))
    return pl.pallas_call(
        _rank_kernel,
        grid=(T // TM,),
        in_specs=[row],
        out_specs=[row, pl.BlockSpec((1, LANES), lambda i: (0, 0))],
        out_shape=[jax.ShapeDtypeStruct((T, LANES), I32), jax.ShapeDtypeStruct((1, LANES), I32)],
        scratch_shapes=[pltpu.VMEM((1, LANES), F32)],
        compiler_params=_params(),
        name="route_rank",
    )(idx)


def _dispatch_kernel(zst_ref, zcnt_ref, ztot_ref, dest_hbm, x_ref, xs_hbm, dsm, zrow, sem_i, sem_o,
                     sem_z):
    i = pl.program_id(0)
    n = TM * TOP_K

    @pl.when(i == 0)
    def _():
        zrow[...] = jnp.zeros_like(zrow)

        def per_expert(e, carry):
            z0 = zst_ref[e]

            def one(j, c):
                r0 = pl.multiple_of((z0 + j) * ROW_CHUNKS, ROW_CHUNKS)
                pltpu.make_async_copy(zrow, xs_hbm.at[pl.ds(r0, ROW_CHUNKS)], sem_z).start()
                return c

            return lax.fori_loop(0, zcnt_ref[e], one, carry)

        lax.fori_loop(0, N_EXPERTS, per_expert, 0)

        def wait_one(j, c):
            pltpu.make_async_copy(zrow, xs_hbm.at[pl.ds(0, ROW_CHUNKS)], sem_z).wait()
            return c

        lax.fori_loop(0, ztot_ref[0], wait_one, 0)

    cp = pltpu.make_async_copy(dest_hbm.at[pl.ds(pl.multiple_of(i * n, n), n)], dsm, sem_i)
    cp.start()
    cp.wait()

    def per_token(t, carry):
        src = x_ref.at[pl.ds(pl.multiple_of(t * ROW_CHUNKS, ROW_CHUNKS), ROW_CHUNKS)]
        for k in range(TOP_K):
            r0 = pl.multiple_of(dsm[t * TOP_K + k] * ROW_CHUNKS, ROW_CHUNKS)
            pltpu.make_async_copy(src, xs_hbm.at[pl.ds(r0, ROW_CHUNKS)], sem_o).start()
        return carry

    lax.fori_loop(0, TM, per_token, 0)
    for _ in range(TOP_K):
        pltpu.make_async_copy(x_ref, xs_hbm.at[pl.ds(0, TM * ROW_CHUNKS)], sem_o).wait()


def _dispatch(zst, zcnt, ztot, dest_flat, x_rows, n_slots):
    T = x_rows.shape[0] // ROW_CHUNKS
    grid_spec = pltpu.PrefetchScalarGridSpec(
        num_scalar_prefetch=3,
        grid=(T // TM,),
        in_specs=[pl.BlockSpec(memory_space=pl.ANY),
                  pl.BlockSpec((TM * ROW_CHUNKS, LANES), lambda i, *_: (i, 0))],
        out_specs=pl.BlockSpec(memory_space=pl.ANY),
        scratch_shapes=[pltpu.SMEM((TM * TOP_K,), I32),
                        pltpu.VMEM((ROW_CHUNKS, LANES), F32),
                        pltpu.SemaphoreType.DMA(()), pltpu.SemaphoreType.DMA(()),
                        pltpu.SemaphoreType.DMA(())],
    )
    return pl.pallas_call(
        _dispatch_kernel,
        grid_spec=grid_spec,
        out_shape=jax.ShapeDtypeStruct((n_slots * ROW_CHUNKS, LANES), F32),
        compiler_params=pltpu.CompilerParams(dimension_semantics=("arbitrary",),
                                             vmem_limit_bytes=VMEM_LIMIT, has_side_effects=True),
        name="moe_dispatch",
    )(zst, zcnt, ztot, dest_flat, x_rows)


def _rows_to_matrix(ref, n):
    return jnp.concatenate([ref[pl.ds(j, n, stride=ROW_CHUNKS), :] for j in range(ROW_CHUNKS)],
                           axis=1)


def _matrix_to_rows(ref, y, n):
    for j in range(ROW_CHUNKS):
        ref[pl.ds(j, n, stride=ROW_CHUNKS), :] = y[:, j * LANES:(j + 1) * LANES]


def _experts_kernel(be_ref, nu_ref, xs_ref, wg_ref, wu_ref, wd_ref, ys_ref, wg_s, wu_s, wd_s):
    b = pl.program_id(0)
    e = be_ref[b]
    prev = be_ref[jnp.maximum(b - 1, 0)]

    @pl.when((b == 0) | (e != prev))
    def _():
        wg_s[...] = wg_ref[...].astype(BF16)
        wu_s[...] = wu_ref[...].astype(BF16)
        wd_s[...] = wd_ref[...].astype(BF16)

    @pl.when(b < nu_ref[0])
    def _():
        x = _rows_to_matrix(xs_ref, MOE_BLK).astype(BF16)
        g = jnp.dot(x, wg_s[...], preferred_element_type=F32)
        u = jnp.dot(x, wu_s[...], preferred_element_type=F32)
        h = (_silu(g) * u).astype(BF16)
        y = jnp.dot(h, wd_s[...], preferred_element_type=F32)
        _matrix_to_rows(ys_ref, y, MOE_BLK)

    @pl.when(b >= nu_ref[0])
    def _():
        ys_ref[...] = jnp.zeros_like(ys_ref)


def _experts(blk_expert, n_used, xs, we_gate, we_up, we_down, layer):
    n_blocks = xs.shape[0] // (MOE_BLK * ROW_CHUNKS)
    slots = pl.BlockSpec((MOE_BLK * ROW_CHUNKS, LANES),
                         lambda b, be, nu: (jnp.minimum(b, nu[0] - 1), 0))
    w_in_spec = pl.BlockSpec((None, None, D_MODEL, D_EXPERT),
                             lambda b, be, nu: (layer, be[b], 0, 0))
    grid_spec = pltpu.PrefetchScalarGridSpec(
        num_scalar_prefetch=2,
        grid=(n_blocks,),
        in_specs=[slots, w_in_spec, w_in_spec,
                  pl.BlockSpec((None, None, D_EXPERT, D_MODEL),
                               lambda b, be, nu: (layer, be[b], 0, 0))],
        out_specs=pl.BlockSpec((MOE_BLK * ROW_CHUNKS, LANES), lambda b, be, nu: (b, 0)),
        scratch_shapes=[pltpu.VMEM((D_MODEL, D_EXPERT), BF16), pltpu.VMEM((D_MODEL, D_EXPERT), BF16),
                        pltpu.VMEM((D_EXPERT, D_MODEL), BF16)],
    )
    return pl.pallas_call(
        _experts_kernel,
        grid_spec=grid_spec,
        out_shape=jax.ShapeDtypeStruct(xs.shape, F32),
        compiler_params=_params(),
        name="moe_experts",
    )(blk_expert, n_used, xs, we_gate, we_up, we_down)


def _combine_kernel(dest_hbm, gate_ref, x_ref, wsg_ref, wsu_ref, wsd_ref, lg_ref, lb_ref, ys_hbm,
                    o_ref, dsm, buf, sem_i, sem_g, *, alpha):
    i = pl.program_id(0)
    tm = TM_COMBINE
    n = tm * TOP_K
    cp = pltpu.make_async_copy(dest_hbm.at[pl.ds(pl.multiple_of(i * n, n), n)], dsm, sem_i)
    cp.start()
    cp.wait()

    def per_token(t, carry):
        d0 = pl.multiple_of(t * ROW_CHUNKS, ROW_CHUNKS)
        for k in range(TOP_K):
            r0 = pl.multiple_of(dsm[t * TOP_K + k] * ROW_CHUNKS, ROW_CHUNKS)
            pltpu.make_async_copy(ys_hbm.at[pl.ds(r0, ROW_CHUNKS)],
                                  buf.at[k, pl.ds(d0, ROW_CHUNKS)], sem_g).start()
        return carry

    lax.fori_loop(0, tm, per_token, 0)

    x1 = x_ref[...]
    xb = x1.astype(BF16)
    hs = _silu(jnp.dot(xb, wsg_ref[...], preferred_element_type=F32))
    hs = (hs * jnp.dot(xb, wsu_ref[...], preferred_element_type=F32)).astype(BF16)
    acc = jnp.dot(hs, wsd_ref[...], preferred_element_type=F32)

    for k in range(TOP_K):
        pltpu.make_async_copy(ys_hbm.at[pl.ds(0, tm * ROW_CHUNKS)], buf.at[k], sem_g).wait()
    gates = gate_ref[...]
    for k in range(TOP_K):
        acc = acc + gates[:, k:k + 1] * _rows_to_matrix(buf.at[k], tm)
    o_ref[...] = _layer_norm(alpha * x1 + acc, lg_ref[...], lb_ref[...])


def _combine(dest_flat, gates, x1, lw, ys, alpha):
    T = x1.shape[0]
    tm = TM_COMBINE
    row = lambda w: pl.BlockSpec((tm, w), lambda i: (i, 0))
    return pl.pallas_call(
        functools.partial(_combine_kernel, alpha=alpha),
        grid=(T // tm,),
        in_specs=[pl.BlockSpec(memory_space=pl.ANY), row(LANES), row(D_MODEL),
                  _resident((D_MODEL, D_SHARED)), _resident((D_MODEL, D_SHARED)),
                  _resident((D_SHARED, D_MODEL)), _resident((1, D_MODEL)), _resident((1, D_MODEL)),
                  pl.BlockSpec(memory_space=pl.ANY)],
        out_specs=row(D_MODEL),
        out_shape=jax.ShapeDtypeStruct((T, D_MODEL), F32),
        scratch_shapes=[pltpu.SMEM((tm * TOP_K,), I32),
                        pltpu.VMEM((TOP_K, tm * ROW_CHUNKS, LANES), F32),
                        pltpu.SemaphoreType.DMA(()), pltpu.SemaphoreType.DMA(())],
        compiler_params=_params(),
        name="moe_combine",
    )(dest_flat, gates, x1, lw["ws_gate"], lw["ws_up"], lw["ws_down"], lw["ln2_g"], lw["ln2_b"], ys)


def _slot_tables(counts, idx8, rank8, n_blocks):
    padded = (counts + MOE_BLK - 1) // MOE_BLK * MOE_BLK
    pad_end = jnp.cumsum(padded)
    pad_start = pad_end - padded
    dest = jnp.take(pad_start, idx8, axis=0) + rank8
    blk_expert = jnp.minimum(
        jnp.searchsorted(pad_end, jnp.arange(n_blocks, dtype=I32) * MOE_BLK, side="right"),
        N_EXPERTS - 1).astype(I32)
    n_used = (pad_end[-1:] // MOE_BLK).astype(I32)
    zst = (pad_start + counts).astype(I32)
    zcnt = (padded - counts).astype(I32)
    ztot = jnp.sum(zcnt, keepdims=True).astype(I32)
    return dest.reshape(-1).astype(I32), blk_expert, n_used, zst, zcnt, ztot


def _moe(x1, idx, gates, lw, we_gate, we_up, we_down, layer, alpha):
    T = x1.shape[0]
    n_blocks = T * TOP_K // MOE_BLK + N_EXPERTS
    rank, cnt = _rank(idx)
    dest, blk_expert, n_used, zst, zcnt, ztot = _slot_tables(
        cnt[0, :N_EXPERTS], idx[:, :TOP_K], rank[:, :TOP_K], n_blocks)
    x_rows = x1.reshape(T * ROW_CHUNKS, LANES)
    xs = _dispatch(zst, zcnt, ztot, dest, x_rows, n_blocks * MOE_BLK)
    ys = _experts(blk_expert, n_used, xs, we_gate, we_up, we_down, layer)
    return _combine(dest, gates, x1, lw, ys, alpha)


def _prep_layer(l, w_in, b_f, w_out, g_att, g_lru, conv_w, conv_b, w_r, b_r, w_i, b_i, lru_lambda,
                ln1_g, ln1_b, w_router, b_router, ws_gate, ws_up, ws_down, ln2_g, ln2_b):
    qkv_end = 3 * ATT_W
    wl = w_in[l]
    w_pad = jnp.concatenate(
        [wl[:, :qkv_end], wl[:, qkv_end + N_HEADS:],
         jnp.pad(wl[:, qkv_end:qkv_end + N_HEADS], ((0, 0), (0, LANES - N_HEADS)))],
        axis=1).astype(BF16)
    wr = jnp.pad(w_router[l], ((0, 0), (0, LANES - N_EXPERTS)))
    wr_hi = wr.astype(BF16)
    return dict(
        w_pad=w_pad,
        bf_pad=jnp.pad(b_f[l], (0, LANES - N_HEADS)).reshape(1, LANES),
        w_out=w_out[l].astype(BF16),
        g_att=g_att[l].reshape(1, ATT_W), g_lru=g_lru[l].reshape(1, LRU_W),
        conv_w=conv_w[l], conv_b=conv_b[l].reshape(1, LRU_W),
        w_r=w_r[l].astype(BF16), b_r=b_r[l].reshape(1, LRU_W),
        w_i=w_i[l].astype(BF16), b_i=b_i[l].reshape(1, LRU_W),
        lam=lru_lambda[l].reshape(1, LRU_W),
        ln1_g=ln1_g[l].reshape(1, D_MODEL), ln1_b=ln1_b[l].reshape(1, D_MODEL),
        wr_hi=wr_hi, wr_lo=(wr - wr_hi.astype(F32)).astype(BF16),
        b_router=jnp.pad(b_router[l], (0, LANES - N_EXPERTS)).reshape(1, LANES),
        ws_gate=ws_gate[l].astype(BF16), ws_up=ws_up[l].astype(BF16),
        ws_down=ws_down[l].astype(BF16),
        ln2_g=ln2_g[l].reshape(1, D_MODEL), ln2_b=ln2_b[l].reshape(1, D_MODEL),
    )


def kernel(x_prompt, x_sample, cache_k, cache_v, cache_logf, state_conv, state_lru, ln_in_g, ln_in_b, w_in, b_f, w_out, g_att, g_lru, conv_w, conv_b, w_r, b_r, w_i, b_i, lru_lambda, ln1_g, ln1_b, w_router, b_router, we_gate, we_up, we_down, ws_gate, ws_up, ws_down, ln2_g, ln2_b):
    B, S, _ = x_prompt.shape
    Bd, Tn, _ = x_sample.shape
    depth = w_in.shape[0]
    P = cache_k.shape[2]
    Tp = B * S
    Ts = Bd * Tn
    alpha = (2 * depth) ** 0.25

    x = jnp.concatenate([x_prompt.reshape(Tp, D_MODEL), x_sample.reshape(Ts, D_MODEL)], axis=0)
    ln_g = ln_in_g.reshape(1, D_MODEL)
    ln_b = ln_in_b.reshape(1, D_MODEL)
    zero_conv = jnp.zeros((B, CONV_W - 1, LRU_W), F32)
    zero_h = jnp.zeros((B, 1, LRU_W), F32)

    ks, vs, lfs, p_conv, p_lru, s_conv, s_lru = [], [], [], [], [], [], []
    for l in range(depth):
        lw = _prep_layer(l, w_in, b_f, w_out, g_att, g_lru, conv_w, conv_b, w_r, b_r, w_i, b_i,
                         lru_lambda, ln1_g, ln1_b, w_router, b_router, ws_gate, ws_up, ws_down,
                         ln2_g, ln2_b)
        outs = _inproj(x, ln_g, ln_b, lw["w_pad"], lw["bf_pad"], apply_ln=(l == 0))
        if l == 0:
            x, outs = outs[0], outs[1:]
        qb, k, v, kb, vb, xr, xg, lf = outs

        c_p = _cumsum(lf[:Tp].reshape(B, S, LANES))
        ct_p = c_p[:, :, :N_HEADS].transpose(0, 2, 1).reshape(B, N_HEADS, S // ATT_TK, ATT_TK)
        lf_cache = jnp.pad(cache_logf[l], ((0, 0), (0, 0), (0, LANES - N_HEADS)))
        c_s = _cumsum(jnp.concatenate([lf_cache, lf[Tp:].reshape(Bd, Tn, LANES)], axis=1))
        ct_s = c_s[:, :, :N_HEADS].transpose(0, 2, 1)

        o_att = _attn_prompt(qb, kb, vb, c_p, ct_p, B, S)
        o_att = _attn_sample(qb, kb, vb, cache_k.reshape(depth, Bd, P, ATT_W),
                             cache_v.reshape(depth, Bd, P, ATT_W), l, c_s, ct_s, o_att,
                             Tp, Bd, P, Tn)

        o_lru, pc, ph = _lru(xr, xg, zero_conv, zero_h, lw, None, 0, B, S, LRU_TS)
        o_lru, sc, sh = _lru(xr, xg, state_conv[l], state_lru[l].reshape(Bd, 1, LRU_W), lw,
                             o_lru, Tp, Bd, Tn, Tn)

        x1, idx, gates = _outproj(o_att, o_lru, x, lw, alpha)
        x = _moe(x1, idx, gates, lw, we_gate, we_up, we_down, l, alpha)

        ks.append(k)
        vs.append(v)
        lfs.append(lf[:, :N_HEADS])
        p_conv.append(pc)
        p_lru.append(ph.reshape(B, LRU_W))
        s_conv.append(sc)
        s_lru.append(sh.reshape(Bd, LRU_W))

    def split(ts, tail):
        p = jnp.stack([t[:Tp].reshape((B, S) + tail) for t in ts])
        s = jnp.stack([t[Tp:].reshape((Bd, Tn) + tail) for t in ts])
        return p, s

    p_k, s_k = split(ks, (N_HEADS, HEAD_DIM))
    p_v, s_v = split(vs, (N_HEADS, HEAD_DIM))
    p_logf, s_logf = split(lfs, (N_HEADS,))
    y_prompt = x[:Tp].reshape(B, S, D_MODEL)
    y_sample = x[Tp:].reshape(Bd, Tn, D_MODEL)
    return (y_prompt, y_sample, p_k, p_v, p_logf, jnp.stack(p_conv), jnp.stack(p_lru),
            s_k, s_v, s_logf, jnp.stack(s_conv), jnp.stack(s_lru))
```

```python
import functools

import jax
import jax.numpy as jnp
from jax import lax
from jax.experimental import pallas as pl
from jax.experimental.pallas import tpu as pltpu

F32 = jnp.float32
BF16 = jnp.bfloat16
I32 = jnp.int32

D_MODEL = 2048
N_HEADS = 8
HEAD_DIM = 128
ATT_W = N_HEADS * HEAD_DIM
LRU_W = D_MODEL - ATT_W
LRU_BLOCKS = 8
LRU_BW = LRU_W // LRU_BLOCKS
CONV_W = 4
LRU_C = 8.0
N_EXPERTS = 64
TOP_K = 8
TOP_K_SHIFT = TOP_K.bit_length() - 1
assert 1 << TOP_K_SHIFT == TOP_K
D_EXPERT = 512
D_SHARED = 512
ROUTE_SCALE = 2.5
ATT_SCALE = HEAD_DIM ** -0.5
LN_EPS = 1e-5
RMS_EPS = 1e-6

LANES = 128
SUBLANES = 8
ROW_CHUNKS = D_MODEL // LANES
VMEM_LIMIT = 56 * 1024 * 1024

TM = 256
MOE_BLK = 256
TM_COMBINE = 128
ROUTE_CHUNK = 2048
TAB_CHUNK = 1024
TAB_RING = TAB_CHUNK // MOE_BLK
ATT_TQ = 256
ATT_TK = 512
LRU_TS = 256
CUM_CHUNK = 128

C_Q, C_K, C_V, C_XR, C_XG, C_F, C_END = 0, 1024, 2048, 3072, 4096, 5120, 5248


def _params(n_axes=1):
    return pltpu.CompilerParams(dimension_semantics=("arbitrary",) * n_axes,
                                vmem_limit_bytes=VMEM_LIMIT)


def _resident(shape):
    nd = len(shape)
    return pl.BlockSpec(shape, lambda *_: (0,) * nd, pipeline_mode=pl.Buffered(1))


def _layer_norm(x, g, b):
    mu = jnp.mean(x, axis=-1, keepdims=True)
    xc = x - mu
    var = jnp.mean(xc * xc, axis=-1, keepdims=True)
    return xc * lax.rsqrt(var + LN_EPS) * g + b


def _rms_norm(x, g):
    return x * lax.rsqrt(jnp.mean(x * x, axis=-1, keepdims=True) + RMS_EPS) * g


def _silu(x):
    return x * jax.nn.sigmoid(x)


def _gelu_tanh(x):
    return 0.5 * x * (1.0 + jnp.tanh(0.7978845608028654 * (x + 0.044715 * (x * x * x))))


def _split3(x):
    hi = x.astype(BF16)
    r1 = x - hi.astype(F32)
    mid = r1.astype(BF16)
    lo = (r1 - mid.astype(F32)).astype(BF16)
    return hi, mid, lo


def _inproj_kernel(apply_ln, x_ref, g_ref, b_ref, w_ref, bf_ref, *outs):
    if apply_ln:
        x0_ref, outs = outs[0], outs[1:]
    q_ref, k_ref, v_ref, kb_ref, vb_ref, xr_ref, xg_ref, lf_ref = outs
    x = x_ref[...]
    if apply_ln:
        x = _layer_norm(x, g_ref[...], b_ref[...])
        x0_ref[...] = x
    h = x.astype(BF16)

    def mm(c0, c1):
        return jnp.dot(h, w_ref[:, c0:c1], preferred_element_type=F32)

    q_ref[...] = mm(C_Q, C_K).astype(BF16)
    k = mm(C_K, C_V)
    k_ref[...] = k
    kb_ref[...] = k.astype(BF16)
    v = mm(C_V, C_XR)
    v_ref[...] = v
    vb_ref[...] = v.astype(BF16)
    xr_ref[...] = mm(C_XR, C_XG)
    xg_ref[...] = mm(C_XG, C_F)
    f = mm(C_F, C_END) + bf_ref[...]
    lf_ref[...] = jnp.minimum(f, 0.0) - jnp.log1p(jnp.exp(-jnp.abs(f)))


def _inproj(x, ln_g, ln_b, w_pad, bf_pad, apply_ln):
    T = x.shape[0]
    row = lambda w: pl.BlockSpec((TM, w), lambda i: (i, 0))
    out_shape = [
        jax.ShapeDtypeStruct((T, ATT_W), BF16),
        jax.ShapeDtypeStruct((T, ATT_W), F32),
        jax.ShapeDtypeStruct((T, ATT_W), F32),
        jax.ShapeDtypeStruct((T, ATT_W), BF16),
        jax.ShapeDtypeStruct((T, ATT_W), BF16),
        jax.ShapeDtypeStruct((T, LRU_W), F32),
        jax.ShapeDtypeStruct((T, LRU_W), F32),
        jax.ShapeDtypeStruct((T, LANES), F32),
    ]
    out_specs = [row(ATT_W)] * 5 + [row(LRU_W)] * 2 + [row(LANES)]
    if apply_ln:
        out_shape = [jax.ShapeDtypeStruct((T, D_MODEL), F32)] + out_shape
        out_specs = [row(D_MODEL)] + out_specs
    return pl.pallas_call(
        functools.partial(_inproj_kernel, apply_ln),
        grid=(T // TM,),
        in_specs=[row(D_MODEL), _resident((1, D_MODEL)), _resident((1, D_MODEL)),
                  _resident((D_MODEL, C_END)), _resident((1, LANES))],
        out_specs=out_specs,
        out_shape=out_shape,
        compiler_params=_params(),
        name="inproj",
    )(x, ln_g, ln_b, w_pad, bf_pad)


def _cumsum_kernel(lf_ref, c_ref, *, S):
    carry = jnp.zeros((1, LANES), F32)
    for c0 in range(0, S, CUM_CHUNK):
        n = min(CUM_CHUNK, S - c0)
        r = lax.broadcasted_iota(I32, (n, n), 0)
        c = lax.broadcasted_iota(I32, (n, n), 1)
        tril = (r >= c).astype(BF16)
        hi, mid, lo = _split3(lf_ref[c0:c0 + n, :])
        acc = jnp.dot(tril, hi, preferred_element_type=F32)
        acc = acc + jnp.dot(tril, mid, preferred_element_type=F32)
        acc = acc + jnp.dot(tril, lo, preferred_element_type=F32)
        acc = acc + carry
        c_ref[c0:c0 + n, :] = acc
        carry = acc[n - 1:n, :]


def _cumsum(lf):
    B, S, _ = lf.shape
    spec = pl.BlockSpec((None, S, LANES), lambda b: (b, 0, 0))
    return pl.pallas_call(
        functools.partial(_cumsum_kernel, S=S),
        grid=(B,),
        in_specs=[spec],
        out_specs=spec,
        out_shape=jax.ShapeDtypeStruct((B, S, LANES), F32),
        compiler_params=_params(),
        name="cumsum_logf",
    )(lf)


def _attn_prompt_kernel(q_ref, k_ref, v_ref, c_ref, ct_ref, o_ref, *, S):
    nq = S // ATT_TQ
    for h in range(N_HEADS):
        hs = slice(h * HEAD_DIM, (h + 1) * HEAD_DIM)

        def q_body(qi, carry, h=h, hs=hs):
            q0 = pl.multiple_of(qi * ATT_TQ, ATT_TQ)
            q = q_ref[pl.ds(q0, ATT_TQ), hs]
            cq = c_ref[pl.ds(q0, ATT_TQ), h:h + 1]
            row = q0 + lax.broadcasted_iota(I32, (ATT_TQ, ATT_TK), 0)

            def kv_body(kj, st):
                m, l, acc = st
                k0 = pl.multiple_of(kj * ATT_TK, ATT_TK)
                k = k_ref[pl.ds(k0, ATT_TK), hs]
                v = v_ref[pl.ds(k0, ATT_TK), hs]
                ck = ct_ref[h, pl.ds(kj, 1), :]
                s = lax.dot_general(q, k, (((1,), (1,)), ((), ())),
                                    preferred_element_type=F32) * ATT_SCALE
                s = s + cq - ck
                col = k0 + lax.broadcasted_iota(I32, (ATT_TQ, ATT_TK), 1)
                s = jnp.where(row >= col, s, -jnp.inf)
                m_new = jnp.maximum(m, jnp.max(s, axis=1, keepdims=True))
                alpha = jnp.exp(m - m_new)
                p = jnp.exp(s - m_new)
                l = alpha * l + jnp.sum(p, axis=1, keepdims=True)
                acc = alpha * acc + jnp.dot(p.astype(BF16), v, preferred_element_type=F32)
                return m_new, l, acc

            nkv = (q0 + ATT_TQ + ATT_TK - 1) // ATT_TK
            init = (jnp.full((ATT_TQ, 1), -jnp.inf, F32), jnp.zeros((ATT_TQ, 1), F32),
                    jnp.zeros((ATT_TQ, HEAD_DIM), F32))
            m, l, acc = lax.fori_loop(0, nkv, kv_body, init)
            o_ref[pl.ds(q0, ATT_TQ), hs] = acc / l
            return carry

        lax.fori_loop(0, nq, q_body, 0)


def _attn_prompt(qb, kb, vb, c, ct, B, S):
    T = B * S
    seq = pl.BlockSpec((S, ATT_W), lambda b: (b, 0))
    return pl.pallas_call(
        functools.partial(_attn_prompt_kernel, S=S),
        grid=(B,),
        in_specs=[seq, seq, seq,
                  pl.BlockSpec((None, S, LANES), lambda b: (b, 0, 0)),
                  pl.BlockSpec((None, N_HEADS, S // ATT_TK, ATT_TK), lambda b: (b, 0, 0, 0))],
        out_specs=seq,
        out_shape=jax.ShapeDtypeStruct((T, ATT_W), F32),
        compiler_params=_params(),
        name="fox_prompt",
    )(qb, kb, vb, c, ct)


def _attn_sample_kernel(q_ref, kn_ref, vn_ref, ck_ref, cv_ref, c_ref, ct_ref, o_ref, *, P, Tn):
    r = lax.broadcasted_iota(I32, (Tn, Tn), 0)
    c = lax.broadcasted_iota(I32, (Tn, Tn), 1)
    nt = (((1,), (1,)), ((), ()))
    for h in range(N_HEADS):
        hs = slice(h * HEAD_DIM, (h + 1) * HEAD_DIM)
        q = q_ref[:, hs]
        kc = ck_ref[:, hs].astype(BF16)
        vc = cv_ref[:, hs].astype(BF16)
        cq = c_ref[P:P + Tn, h:h + 1]
        ckr = ct_ref[h:h + 1, :]
        s_c = lax.dot_general(q, kc, nt, preferred_element_type=F32) * ATT_SCALE
        s_c = s_c + cq - ckr[:, :P]
        s_n = lax.dot_general(q, kn_ref[:, hs], nt, preferred_element_type=F32) * ATT_SCALE
        s_n = jnp.where(r >= c, s_n + cq - ckr[:, P:P + Tn], -jnp.inf)
        m = jnp.maximum(jnp.max(s_c, axis=1, keepdims=True), jnp.max(s_n, axis=1, keepdims=True))
        p_c = jnp.exp(s_c - m)
        p_n = jnp.exp(s_n - m)
        l = jnp.sum(p_c, axis=1, keepdims=True) + jnp.sum(p_n, axis=1, keepdims=True)
        o = jnp.dot(p_c.astype(BF16), vc, preferred_element_type=F32)
        o = o + jnp.dot(p_n.astype(BF16), vn_ref[:, hs], preferred_element_type=F32)
        o_ref[:, hs] = o / l


def _attn_sample(qb, kb, vb, cache_k, cache_v, layer, c, ct, row0, Bd, P, Tn):
    blk0 = row0 // Tn
    new = pl.BlockSpec((Tn, ATT_W), lambda b: (blk0 + b, 0))
    cache = pl.BlockSpec((None, None, P, ATT_W), lambda b: (layer, b, 0, 0))
    return pl.pallas_call(
        functools.partial(_attn_sample_kernel, P=P, Tn=Tn),
        grid=(Bd,),
        in_specs=[new, new, new, cache, cache,
                  pl.BlockSpec((None, P + Tn, LANES), lambda b: (b, 0, 0)),
                  pl.BlockSpec((None, N_HEADS, P + Tn), lambda b: (b, 0, 0))],
        out_specs=pl.BlockSpec((Tn, ATT_W), lambda b: (b, 0)),
        out_shape=jax.ShapeDtypeStruct((Bd * Tn, ATT_W), F32),
        compiler_params=_params(),
        name="fox_sample",
    )(qb, kb, vb, cache_k, cache_v, c, ct)


def _lru_kernel(xr_ref, xg_ref, cs_ref, h0_ref, cw_ref, cb_ref, wr_ref, br_ref, wi_ref, bi_ref,
                lam_ref, o_ref, cso_ref, hl_ref, xp_ref, a_ref, u_ref, hs_ref, h_ref, *, ts):
    s = pl.program_id(1)
    pad = SUBLANES
    nprev = CONV_W - 1

    @pl.when(s == 0)
    def _():
        xp_ref[pad - nprev:pad, :] = cs_ref[...]
        h_ref[...] = h0_ref[...]

    x = xr_ref[...]
    xp_ref[pad:pad + ts, :] = x
    xc = cb_ref[...] + cw_ref[nprev:nprev + 1, :] * x
    for i in range(nprev):
        xc = xc + cw_ref[i:i + 1, :] * xp_ref[pad - nprev + i:pad - nprev + i + ts, :]
    tail = xp_ref[pad + ts - nprev:pad + ts, :]
    xp_ref[pad - nprev:pad, :] = tail
    cso_ref[...] = tail

    xcb = xc.astype(BF16)

    def block_diag(w_ref, b_ref):
        parts = [jnp.dot(xcb[:, n * LRU_BW:(n + 1) * LRU_BW], w_ref[n], preferred_element_type=F32)
                 for n in range(LRU_BLOCKS)]
        return jnp.concatenate(parts, axis=1) + b_ref[...]

    r = jax.nn.sigmoid(block_diag(wr_ref, br_ref))
    g = jax.nn.sigmoid(block_diag(wi_ref, bi_ref))
    nl = -lam_ref[...]
    softplus = jnp.maximum(nl, 0.0) + jnp.log1p(jnp.exp(-jnp.abs(nl)))
    log_a = (-LRU_C) * r * softplus
    a = jnp.exp(log_a)
    u = jnp.sqrt(1.0 - a * a) * g * xc

    sub = lax.broadcasted_iota(I32, (ts, LRU_W), 0) % SUBLANES
    d = 1
    while d < SUBLANES:
        keep = sub >= d
        u = jnp.where(keep, u + a * pltpu.roll(u, d, axis=0), u)
        a = jnp.where(keep, a * pltpu.roll(a, d, axis=0), a)
        d *= 2
    a_ref[...] = a
    u_ref[...] = u

    def group(gi, hprev):
        r0 = pl.multiple_of(gi * SUBLANES, SUBLANES)
        hh = u_ref[pl.ds(r0, SUBLANES), :] + a_ref[pl.ds(r0, SUBLANES), :] * hprev
        hs_ref[pl.ds(r0, SUBLANES), :] = hh
        return hh[SUBLANES - 1:SUBLANES, :]

    hlast = lax.fori_loop(0, ts // SUBLANES, group, h_ref[...])
    h_ref[...] = hlast
    hl_ref[...] = hlast
    o_ref[...] = hs_ref[...] * _gelu_tanh(xg_ref[...])


def _lru(xr, xg, conv_state, h0, lw, row0, nb, S, ts):
    ns = S // ts
    blk0 = row0 // ts
    rows = pl.BlockSpec((ts, LRU_W), lambda b, s: (blk0 + b * ns + s, 0))
    vec = lambda n: _resident((n, LRU_W))
    per_seq = lambda n: pl.BlockSpec((None, n, LRU_W), lambda b, s: (b, 0, 0))
    in_specs = [rows, rows, per_seq(CONV_W - 1), per_seq(1), vec(CONV_W), vec(1),
                _resident((LRU_BLOCKS, LRU_BW, LRU_BW)), vec(1),
                _resident((LRU_BLOCKS, LRU_BW, LRU_BW)), vec(1), vec(1)]
    args = [xr, xg, conv_state, h0, lw["conv_w"], lw["conv_b"], lw["w_r"], lw["b_r"],
            lw["w_i"], lw["b_i"], lw["lam"]]
    return pl.pallas_call(
        functools.partial(_lru_kernel, ts=ts),
        grid=(nb, ns),
        in_specs=in_specs,
        out_specs=[pl.BlockSpec((ts, LRU_W), lambda b, s: (b * ns + s, 0)),
                   per_seq(CONV_W - 1), per_seq(1)],
        out_shape=[jax.ShapeDtypeStruct((nb * S, LRU_W), F32),
                   jax.ShapeDtypeStruct((nb, CONV_W - 1, LRU_W), F32),
                   jax.ShapeDtypeStruct((nb, 1, LRU_W), F32)],
        scratch_shapes=[pltpu.VMEM((SUBLANES + ts, LRU_W), F32),
                        pltpu.VMEM((ts, LRU_W), F32), pltpu.VMEM((ts, LRU_W), F32),
                        pltpu.VMEM((ts, LRU_W), F32), pltpu.VMEM((1, LRU_W), F32)],
        compiler_params=_params(2),
        name="rg_lru",
    )(*args)


def _outproj_kernel(oap_ref, oas_ref, olp_ref, ols_ref, x_ref, w_ref, ga_ref, gl_ref, lg_ref,
                    lb_ref, wrh_ref, wrl_ref, rb_ref, x1_ref, x1s_ref, idx_ref, gate_ref,
                    *, alpha, n_prompt_blocks):
    is_prompt = pl.program_id(0) < n_prompt_blocks
    oa = jnp.where(is_prompt, oap_ref[...], oas_ref[...])
    ol = jnp.where(is_prompt, olp_ref[...], ols_ref[...])
    na = _rms_norm(oa, ga_ref[...]).astype(BF16)
    nl = _rms_norm(ol, gl_ref[...]).astype(BF16)
    m = jnp.dot(na, w_ref[0:ATT_W, :], preferred_element_type=F32)
    m = m + jnp.dot(nl, w_ref[ATT_W:D_MODEL, :], preferred_element_type=F32)
    x1 = _layer_norm(alpha * x_ref[...] + m, lg_ref[...], lb_ref[...])
    x1_ref[...] = x1
    _matrix_to_rows(x1s_ref, x1, x1.shape[0])

    xh = x1.astype(BF16)
    xl = (x1 - xh.astype(F32)).astype(BF16)
    logits = jnp.dot(xh, wrh_ref[...], preferred_element_type=F32)
    logits = logits + jnp.dot(xl, wrh_ref[...], preferred_element_type=F32)
    logits = logits + jnp.dot(xh, wrl_ref[...], preferred_element_type=F32)
    scores = jax.nn.sigmoid(logits)
    tm = scores.shape[0]
    lane = lax.broadcasted_iota(I32, (tm, LANES), 1)
    lane_f = lane.astype(F32)
    sel = jnp.where(lane < N_EXPERTS, scores + rb_ref[...], -jnp.inf)
    idx_out = jnp.zeros((tm, LANES), F32)
    gate_out = jnp.zeros((tm, LANES), F32)
    gsum = jnp.zeros((tm, 1), F32)
    for k in range(TOP_K):
        mx = jnp.max(sel, axis=1, keepdims=True)
        ik = jnp.min(jnp.where(sel == mx, lane_f, float(LANES)), axis=1, keepdims=True)
        hit = lane_f == ik
        gk = jnp.sum(jnp.where(hit, scores, 0.0), axis=1, keepdims=True)
        sel = jnp.where(hit, -jnp.inf, sel)
        idx_out = jnp.where(lane == k, ik, idx_out)
        gate_out = jnp.where(lane == k, gk, gate_out)
        gsum = gsum + gk
    idx_ref[...] = idx_out.astype(I32)
    gate_ref[...] = gate_out / gsum * ROUTE_SCALE


def _outproj(oa_p, oa_s, ol_p, ol_s, x, lw, alpha):
    T = x.shape[0]
    n_p = oa_p.shape[0] // TM
    row = lambda w: pl.BlockSpec((TM, w), lambda i: (i, 0))
    prompt = lambda w: pl.BlockSpec((TM, w), lambda i: (jnp.minimum(i, n_p - 1), 0))
    sample = lambda w: pl.BlockSpec((TM, w), lambda i: (jnp.maximum(i - n_p, 0), 0))
    vec = lambda n: _resident((1, n))
    return pl.pallas_call(
        functools.partial(_outproj_kernel, alpha=alpha, n_prompt_blocks=n_p),
        grid=(T // TM,),
        in_specs=[prompt(ATT_W), sample(ATT_W), prompt(LRU_W), sample(LRU_W), row(D_MODEL),
                  _resident((D_MODEL, D_MODEL)),
                  vec(ATT_W), vec(LRU_W), vec(D_MODEL), vec(D_MODEL),
                  _resident((D_MODEL, LANES)), _resident((D_MODEL, LANES)), vec(LANES)],
        out_specs=[row(D_MODEL), pl.BlockSpec((TM * ROW_CHUNKS, LANES), lambda i: (i, 0)),
                   row(LANES), row(LANES)],
        out_shape=[jax.ShapeDtypeStruct((T, D_MODEL), F32),
                   jax.ShapeDtypeStruct((T * ROW_CHUNKS, LANES), F32),
                   jax.ShapeDtypeStruct((T, LANES), I32),
                   jax.ShapeDtypeStruct((T, LANES), F32)],
        compiler_params=_params(),
        name="outproj_ln_router",
    )(oa_p, oa_s, ol_p, ol_s, x, lw["w_out"], lw["g_att"], lw["g_lru"], lw["ln1_g"], lw["ln1_b"],
      lw["wr_hi"], lw["wr_lo"], lw["b_router"])


def _rank_kernel(idx_ref, rank_ref, cnt_ref, carry_ref):
    @pl.when(pl.program_id(0) == 0)
    def _():
        carry_ref[...] = jnp.zeros_like(carry_ref)

    tm = idx_ref.shape[0]
    idx = idx_ref[...]
    lane = lax.broadcasted_iota(I32, (tm, LANES), 1)
    onehot = jnp.zeros((tm, LANES), F32)
    for k in range(TOP_K):
        onehot = onehot + (lane == idx[:, k:k + 1]).astype(F32)
    r = lax.broadcasted_iota(I32, (tm, tm), 0)
    c = lax.broadcasted_iota(I32, (tm, tm), 1)
    below = (r > c).astype(BF16)
    rank = jnp.dot(below, onehot.astype(BF16), preferred_element_type=F32) + carry_ref[...]
    out = jnp.zeros((tm, LANES), F32)
    for k in range(TOP_K):
        rk = jnp.sum(jnp.where(lane == idx[:, k:k + 1], rank, 0.0), axis=1, keepdims=True)
        out = jnp.where(lane == k, rk, out)
    rank_ref[...] = out.astype(I32)
    total = carry_ref[...] + jnp.sum(onehot, axis=0, keepdims=True)
    carry_ref[...] = total
    cnt_ref[...] = total.astype(I32)


def _rank(idx):
    T = idx.shape[0]
    row = pl.BlockSpec((TM, LANES), lambda i: (i, 0))
    return pl.pallas_call(
        _rank_kernel,
        grid=(T // TM,),
        in_specs=[row],
        out_specs=[row, pl.BlockSpec((1, LANES), lambda i: (0, 0))],
        out_shape=[jax.ShapeDtypeStruct((T, LANES), I32), jax.ShapeDtypeStruct((1, LANES), I32)],
        scratch_shapes=[pltpu.VMEM((1, LANES), F32)],
        compiler_params=_params(),
        name="route_rank",
    )(idx)


def _slot_table_kernel(ps_ref, idx_hbm, rank_hbm, zeros_hbm, tab_hbm, tab, ichunk, rchunk, sem):
    i = pl.program_id(0)
    n = ROUTE_CHUNK

    @pl.when(i == 0)
    def _():
        cp = pltpu.make_async_copy(zeros_hbm, tab, sem.at[0])
        cp.start()
        cp.wait()

    off = pl.multiple_of(i * n, n)
    c1 = pltpu.make_async_copy(idx_hbm.at[pl.ds(off, n)], ichunk, sem.at[1])
    c2 = pltpu.make_async_copy(rank_hbm.at[pl.ds(off, n)], rchunk, sem.at[2])
    c1.start()
    c2.start()
    c1.wait()
    c2.wait()

    def group(j8, carry):
        for u in range(SUBLANES):
            j = j8 * SUBLANES + u
            tab[ps_ref[ichunk[j]] + rchunk[j]] = off + j
        return carry

    lax.fori_loop(0, n // SUBLANES, group, 0)

    @pl.when(i == pl.num_programs(0) - 1)
    def _():
        cp = pltpu.make_async_copy(tab, tab_hbm, sem.at[0])
        cp.start()
        cp.wait()


def _slot_table(pad_start, idx_flat, rank_flat, n_slots):
    n = idx_flat.shape[0]
    any_spec = pl.BlockSpec(memory_space=pl.ANY)
    grid_spec = pltpu.PrefetchScalarGridSpec(
        num_scalar_prefetch=1,
        grid=(n // ROUTE_CHUNK,),
        in_specs=[any_spec, any_spec, any_spec],
        out_specs=any_spec,
        scratch_shapes=[pltpu.SMEM((n_slots,), I32), pltpu.SMEM((ROUTE_CHUNK,), I32),
                        pltpu.SMEM((ROUTE_CHUNK,), I32), pltpu.SemaphoreType.DMA((3,))],
    )
    return pl.pallas_call(
        _slot_table_kernel,
        grid_spec=grid_spec,
        out_shape=jax.ShapeDtypeStruct((n_slots,), I32),
        compiler_params=_params(),
        name="moe_slot_table",
    )(pad_start, idx_flat, rank_flat, jnp.zeros((n_slots,), I32))


def _rows_to_matrix(ref, n):
    return jnp.concatenate([ref[pl.ds(j, n, stride=ROW_CHUNKS), :] for j in range(ROW_CHUNKS)],
                           axis=1)


def _matrix_to_rows(ref, y, n):
    for j in range(ROW_CHUNKS):
        ref[pl.ds(j, n, stride=ROW_CHUNKS), :] = y[:, j * LANES:(j + 1) * LANES]


def _experts_kernel(first_ref, valid_ref, nu_ref, x1s_hbm, tab_hbm, wg_ref, wu_ref, wd_ref, yk_hbm,
                    wg_s, wu_s, wd_s, xbuf, ybuf, stk, gsem, ssem, tsem, *, n_tokens):
    e = pl.program_id(0)
    nu = nu_ref[0]
    blk_rows = MOE_BLK * ROW_CHUNKS

    def table_copy(g):
        q = g % TAB_RING
        src = tab_hbm.at[pl.ds(pl.multiple_of((g // TAB_RING) * TAB_CHUNK, TAB_CHUNK), TAB_CHUNK)]
        dst = stk.at[pl.ds(pl.multiple_of(q * TAB_CHUNK, TAB_CHUNK), TAB_CHUNK)]
        return pltpu.make_async_copy(src, dst, tsem.at[q])

    def table_base(g):
        return (g % TAB_RING) * (TAB_CHUNK + MOE_BLK)

    def issue_gather(g):
        sl = g % 2
        base = table_base(g)

        def group(i, carry):
            for u in range(SUBLANES):
                r = i * SUBLANES + u
                tok = stk[base + r] >> TOP_K_SHIFT
                src = x1s_hbm.at[pl.ds(pl.multiple_of(tok * ROW_CHUNKS, ROW_CHUNKS), ROW_CHUNKS)]
                dst = xbuf.at[sl, pl.ds(pl.multiple_of(r * ROW_CHUNKS, ROW_CHUNKS), ROW_CHUNKS)]
                pltpu.make_async_copy(src, dst, gsem.at[sl]).start()
            return carry

        lax.fori_loop(0, MOE_BLK // SUBLANES, group, 0)

    def wait_gather(g):
        sl = g % 2
        pltpu.make_async_copy(x1s_hbm.at[pl.ds(0, blk_rows)], xbuf.at[sl], gsem.at[sl]).wait()

    def issue_scatter(g):
        sl = g % 2
        base = table_base(g)
        nv = valid_ref[g]

        def one(r):
            v = stk[base + r]
            row = (v & (TOP_K - 1)) * n_tokens + (v >> TOP_K_SHIFT)
            src = ybuf.at[sl, pl.ds(pl.multiple_of(r * ROW_CHUNKS, ROW_CHUNKS), ROW_CHUNKS)]
            dst = yk_hbm.at[pl.ds(pl.multiple_of(row * ROW_CHUNKS, ROW_CHUNKS), ROW_CHUNKS)]
            pltpu.make_async_copy(src, dst, ssem.at[sl]).start()

        def group(i, carry):
            for u in range(SUBLANES):
                one(i * SUBLANES + u)
            return carry

        def single(r, carry):
            one(r)
            return carry

        n8 = nv // SUBLANES
        lax.fori_loop(0, n8, group, 0)
        lax.fori_loop(n8 * SUBLANES, nv, single, 0)

    def wait_scatter(g):
        sl = g % 2
        nv = valid_ref[g]
        bit = MOE_BLK
        while bit >= 1:
            @pl.when((nv & bit) != 0)
            def _(bit=bit):
                n = bit * ROW_CHUNKS
                pltpu.make_async_copy(ybuf.at[sl, pl.ds(0, n)], yk_hbm.at[pl.ds(0, n)],
                                      ssem.at[sl]).wait()
            bit //= 2

    @pl.when(e == 0)
    def _():
        table_copy(0).start()

        @pl.when(nu > 1)
        def _():
            table_copy(1).start()

        table_copy(0).wait()
        issue_gather(0)

    g0 = first_ref[e]
    g1 = first_ref[e + 1]

    @pl.when(g1 > g0)
    def _():
        wg_s[...] = wg_ref[...].astype(BF16)
        wu_s[...] = wu_ref[...].astype(BF16)
        wd_s[...] = wd_ref[...].astype(BF16)

    def block(g, carry):
        @pl.when(g + 2 < nu)
        def _():
            table_copy(g + 2).start()

        @pl.when(g + 1 < nu)
        def _():
            table_copy(g + 1).wait()
            issue_gather(g + 1)

        wait_gather(g)

        @pl.when(g >= 2)
        def _():
            wait_scatter(g - 2)

        sl = g % 2
        x = _rows_to_matrix(xbuf.at[sl], MOE_BLK).astype(BF16)
        gt = jnp.dot(x, wg_s[...], preferred_element_type=F32)
        up = jnp.dot(x, wu_s[...], preferred_element_type=F32)
        h = (_silu(gt) * up).astype(BF16)
        y = jnp.dot(h, wd_s[...], preferred_element_type=F32)
        _matrix_to_rows(ybuf.at[sl], y, MOE_BLK)
        issue_scatter(g)

        @pl.when(g == nu - 1)
        def _():
            @pl.when(g >= 1)
            def _():
                wait_scatter(g - 1)

            wait_scatter(g)

        return carry

    lax.fori_loop(g0, g1, block, 0)


def _experts(first_blk, valid, n_used, x1s, slot_tab, we_gate, we_up, we_down, layer):
    n_tokens = x1s.shape[0] // ROW_CHUNKS
    blk_rows = MOE_BLK * ROW_CHUNKS
    any_spec = pl.BlockSpec(memory_space=pl.ANY)
    w_in_spec = pl.BlockSpec((None, None, D_MODEL, D_EXPERT), lambda e, *_: (layer, e, 0, 0))
    grid_spec = pltpu.PrefetchScalarGridSpec(
        num_scalar_prefetch=3,
        grid=(N_EXPERTS,),
        in_specs=[any_spec, any_spec, w_in_spec, w_in_spec,
                  pl.BlockSpec((None, None, D_EXPERT, D_MODEL), lambda e, *_: (layer, e, 0, 0))],
        out_specs=any_spec,
        scratch_shapes=[pltpu.VMEM((D_MODEL, D_EXPERT), BF16), pltpu.VMEM((D_MODEL, D_EXPERT), BF16),
                        pltpu.VMEM((D_EXPERT, D_MODEL), BF16),
                        pltpu.VMEM((2, blk_rows, LANES), F32), pltpu.VMEM((2, blk_rows, LANES), F32),
                        pltpu.SMEM((TAB_RING * TAB_CHUNK,), I32),
                        pltpu.SemaphoreType.DMA((2,)), pltpu.SemaphoreType.DMA((2,)),
                        pltpu.SemaphoreType.DMA((TAB_RING,))],
    )
    return pl.pallas_call(
        functools.partial(_experts_kernel, n_tokens=n_tokens),
        grid_spec=grid_spec,
        out_shape=jax.ShapeDtypeStruct((TOP_K * n_tokens * ROW_CHUNKS, LANES), F32),
        compiler_params=_params(),
        name="moe_experts",
    )(first_blk, valid, n_used, x1s, slot_tab, we_gate, we_up, we_down)


def _combine_kernel(yk_ref, gate_ref, x_ref, wsg_ref, wsu_ref, wsd_ref, lg_ref, lb_ref, o_ref,
                    *, alpha):
    tm = x_ref.shape[0]
    x1 = x_ref[...]
    xb = x1.astype(BF16)
    hs = _silu(jnp.dot(xb, wsg_ref[...], preferred_element_type=F32))
    hs = (hs * jnp.dot(xb, wsu_ref[...], preferred_element_type=F32)).astype(BF16)
    acc = jnp.dot(hs, wsd_ref[...], preferred_element_type=F32)
    gates = gate_ref[...]
    for k in range(TOP_K):
        acc = acc + gates[:, k:k + 1] * _rows_to_matrix(yk_ref.at[k], tm)
    o_ref[...] = _layer_norm(alpha * x1 + acc, lg_ref[...], lb_ref[...])


def _combine(yk, gates, x1, lw, alpha):
    T = x1.shape[0]
    tm = TM_COMBINE
    row = lambda w: pl.BlockSpec((tm, w), lambda i: (i, 0))
    return pl.pallas_call(
        functools.partial(_combine_kernel, alpha=alpha),
        grid=(T // tm,),
        in_specs=[pl.BlockSpec((TOP_K, tm * ROW_CHUNKS, LANES), lambda i: (0, i, 0)),
                  row(LANES), row(D_MODEL),
                  _resident((D_MODEL, D_SHARED)), _resident((D_MODEL, D_SHARED)),
                  _resident((D_SHARED, D_MODEL)), _resident((1, D_MODEL)), _resident((1, D_MODEL))],
        out_specs=row(D_MODEL),
        out_shape=jax.ShapeDtypeStruct((T, D_MODEL), F32),
        compiler_params=_params(),
        name="moe_combine",
    )(yk.reshape(TOP_K, T * ROW_CHUNKS, LANES), gates, x1, lw["ws_gate"], lw["ws_up"],
      lw["ws_down"], lw["ln2_g"], lw["ln2_b"])


def _block_tables(counts, n_blocks):
    padded = (counts + MOE_BLK - 1) // MOE_BLK * MOE_BLK
    pad_end = jnp.cumsum(padded)
    pad_start = pad_end - padded
    first_blk = jnp.concatenate([pad_start, pad_end[-1:]]) // MOE_BLK
    starts = jnp.arange(n_blocks, dtype=I32) * MOE_BLK
    owner = jnp.sum((pad_end[None, :] <= starts[:, None]).astype(I32), axis=1)
    is_owner = owner[:, None] == jnp.arange(N_EXPERTS, dtype=I32)[None, :]
    used_end = jnp.sum(jnp.where(is_owner, (pad_start + counts)[None, :], 0), axis=1)
    valid = jnp.clip(used_end - starts, 0, MOE_BLK)
    n_used = pad_end[-1:] // MOE_BLK
    return pad_start.astype(I32), first_blk.astype(I32), valid.astype(I32), n_used.astype(I32)


def _moe(x1, x1s, idx, gates, lw, we_gate, we_up, we_down, layer, alpha):
    T = x1.shape[0]
    n_blocks = T * TOP_K // MOE_BLK + N_EXPERTS
    rank, cnt = _rank(idx)
    pad_start, first_blk, valid, n_used = _block_tables(cnt[0, :N_EXPERTS], n_blocks)
    slot_tab = _slot_table(pad_start, idx[:, :TOP_K].reshape(-1), rank[:, :TOP_K].reshape(-1),
                           n_blocks * MOE_BLK)
    yk = _experts(first_blk, valid, n_used, x1s, slot_tab, we_gate, we_up, we_down, layer)
    return _combine(yk, gates, x1, lw, alpha)


def _prep_layer(l, w_in, b_f, w_out, g_att, g_lru, conv_w, conv_b, w_r, b_r, w_i, b_i, lru_lambda,
                ln1_g, ln1_b, w_router, b_router, ws_gate, ws_up, ws_down, ln2_g, ln2_b):
    qkv_end = 3 * ATT_W
    wl = w_in[l]
    w_pad = jnp.concatenate(
        [wl[:, :qkv_end], wl[:, qkv_end + N_HEADS:],
         jnp.pad(wl[:, qkv_end:qkv_end + N_HEADS], ((0, 0), (0, LANES - N_HEADS)))],
        axis=1).astype(BF16)
    wr = jnp.pad(w_router[l], ((0, 0), (0, LANES - N_EXPERTS)))
    wr_hi = wr.astype(BF16)
    return dict(
        w_pad=w_pad,
        bf_pad=jnp.pad(b_f[l], (0, LANES - N_HEADS)).reshape(1, LANES),
        w_out=w_out[l].astype(BF16),
        g_att=g_att[l].reshape(1, ATT_W), g_lru=g_lru[l].reshape(1, LRU_W),
        conv_w=conv_w[l], conv_b=conv_b[l].reshape(1, LRU_W),
        w_r=w_r[l].astype(BF16), b_r=b_r[l].reshape(1, LRU_W),
        w_i=w_i[l].astype(BF16), b_i=b_i[l].reshape(1, LRU_W),
        lam=lru_lambda[l].reshape(1, LRU_W),
        ln1_g=ln1_g[l].reshape(1, D_MODEL), ln1_b=ln1_b[l].reshape(1, D_MODEL),
        wr_hi=wr_hi, wr_lo=(wr - wr_hi.astype(F32)).astype(BF16),
        b_router=jnp.pad(b_router[l], (0, LANES - N_EXPERTS)).reshape(1, LANES),
        ws_gate=ws_gate[l].astype(BF16), ws_up=ws_up[l].astype(BF16),
        ws_down=ws_down[l].astype(BF16),
        ln2_g=ln2_g[l].reshape(1, D_MODEL), ln2_b=ln2_b[l].reshape(1, D_MODEL),
    )


def kernel(x_prompt, x_sample, cache_k, cache_v, cache_logf, state_conv, state_lru, ln_in_g, ln_in_b, w_in, b_f, w_out, g_att, g_lru, conv_w, conv_b, w_r, b_r, w_i, b_i, lru_lambda, ln1_g, ln1_b, w_router, b_router, we_gate, we_up, we_down, ws_gate, ws_up, ws_down, ln2_g, ln2_b):
    B, S, _ = x_prompt.shape
    Bd, Tn, _ = x_sample.shape
    depth = w_in.shape[0]
    P = cache_k.shape[2]
    Tp = B * S
    Ts = Bd * Tn
    alpha = (2 * depth) ** 0.25

    x = jnp.concatenate([x_prompt.reshape(Tp, D_MODEL), x_sample.reshape(Ts, D_MODEL)], axis=0)
    ln_g = ln_in_g.reshape(1, D_MODEL)
    ln_b = ln_in_b.reshape(1, D_MODEL)
    zero_conv = jnp.zeros((B, CONV_W - 1, LRU_W), F32)
    zero_h = jnp.zeros((B, 1, LRU_W), F32)

    ks, vs, lfs, p_conv, p_lru, s_conv, s_lru = [], [], [], [], [], [], []
    for l in range(depth):
        lw = _prep_layer(l, w_in, b_f, w_out, g_att, g_lru, conv_w, conv_b, w_r, b_r, w_i, b_i,
                         lru_lambda, ln1_g, ln1_b, w_router, b_router, ws_gate, ws_up, ws_down,
                         ln2_g, ln2_b)
        outs = _inproj(x, ln_g, ln_b, lw["w_pad"], lw["bf_pad"], apply_ln=(l == 0))
        if l == 0:
            x, outs = outs[0], outs[1:]
        qb, k, v, kb, vb, xr, xg, lf = outs

        c_p = _cumsum(lf[:Tp].reshape(B, S, LANES))
        ct_p = c_p[:, :, :N_HEADS].transpose(0, 2, 1).reshape(B, N_HEADS, S // ATT_TK, ATT_TK)
        lf_cache = jnp.pad(cache_logf[l], ((0, 0), (0, 0), (0, LANES - N_HEADS)))
        c_s = _cumsum(jnp.concatenate([lf_cache, lf[Tp:].reshape(Bd, Tn, LANES)], axis=1))
        ct_s = c_s[:, :, :N_HEADS].transpose(0, 2, 1)

        oa_p = _attn_prompt(qb, kb, vb, c_p, ct_p, B, S)
        oa_s = _attn_sample(qb, kb, vb, cache_k.reshape(depth, Bd, P, ATT_W),
                            cache_v.reshape(depth, Bd, P, ATT_W), l, c_s, ct_s, Tp, Bd, P, Tn)

        ol_p, pc, ph = _lru(xr, xg, zero_conv, zero_h, lw, 0, B, S, LRU_TS)
        ol_s, sc, sh = _lru(xr, xg, state_conv[l], state_lru[l].reshape(Bd, 1, LRU_W), lw,
                            Tp, Bd, Tn, Tn)

        x1, x1s, idx, gates = _outproj(oa_p, oa_s, ol_p, ol_s, x, lw, alpha)
        x = _moe(x1, x1s, idx, gates, lw, we_gate, we_up, we_down, l, alpha)

        ks.append(k)
        vs.append(v)
        lfs.append(lf[:, :N_HEADS])
        p_conv.append(pc)
        p_lru.append(ph.reshape(B, LRU_W))
        s_conv.append(sc)
        s_lru.append(sh.reshape(Bd, LRU_W))

    def split(ts, tail):
        p = jnp.stack([t[:Tp].reshape((B, S) + tail) for t in ts])
        s = jnp.stack([t[Tp:].reshape((Bd, Tn) + tail) for t in ts])
        return p, s

    p_k, s_k = split(ks, (N_HEADS, HEAD_DIM))
    p_v, s_v = split(vs, (N_HEADS, HEAD_DIM))
    p_logf, s_logf = split(lfs, (N_HEADS,))
    y_prompt = x[:Tp].reshape(B, S, D_MODEL)
    y_sample = x[Tp:].reshape(Bd, Tn, D_MODEL)
    return (y_prompt, y_sample, p_k, p_v, p_logf, jnp.stack(p_conv), jnp.stack(p_lru),
            s_k, s_v, s_logf, jnp.stack(s_conv), jnp.stack(s_lru))
```

```python
import functools

import jax
import jax.numpy as jnp
from jax import lax
from jax.experimental import pallas as pl
from jax.experimental.pallas import tpu as pltpu

F32 = jnp.float32
BF16 = jnp.bfloat16
I32 = jnp.int32

D_MODEL = 2048
N_HEADS = 8
HEAD_DIM = 128
ATT_W = N_HEADS * HEAD_DIM
LRU_W = D_MODEL - ATT_W
LRU_BLOCKS = 8
LRU_BW = LRU_W // LRU_BLOCKS
CONV_W = 4
LRU_C = 8.0
N_EXPERTS = 64
TOP_K = 8
TOP_K_SHIFT = TOP_K.bit_length() - 1
assert 1 << TOP_K_SHIFT == TOP_K
D_EXPERT = 512
D_SHARED = 512
ROUTE_SCALE = 2.5
ATT_SCALE = HEAD_DIM ** -0.5
LN_EPS = 1e-5
RMS_EPS = 1e-6

LANES = 128
SUBLANES = 8
ROW_CHUNKS = D_MODEL // LANES
VMEM_LIMIT = 56 * 1024 * 1024

TM = 256
MOE_BLK = 256
TM_COMBINE = 128
ROUTE_CHUNK = 2048
TAB_CHUNK = 1024
TAB_RING = TAB_CHUNK // MOE_BLK
ATT_TQ = 512
ATT_TK = ATT_TQ
ATT_HEAD_GROUP = 2
LRU_TS = 256
CUM_CHUNK = 128

C_Q, C_K, C_V, C_XR, C_XG, C_F, C_END = 0, 1024, 2048, 3072, 4096, 5120, 5248


def _params(n_axes=1):
    return pltpu.CompilerParams(dimension_semantics=("arbitrary",) * n_axes,
                                vmem_limit_bytes=VMEM_LIMIT)


def _resident(shape):
    nd = len(shape)
    return pl.BlockSpec(shape, lambda *_: (0,) * nd, pipeline_mode=pl.Buffered(1))


def _layer_norm(x, g, b):
    mu = jnp.mean(x, axis=-1, keepdims=True)
    xc = x - mu
    var = jnp.mean(xc * xc, axis=-1, keepdims=True)
    return xc * lax.rsqrt(var + LN_EPS) * g + b


def _rms_norm(x, g):
    return x * lax.rsqrt(jnp.mean(x * x, axis=-1, keepdims=True) + RMS_EPS) * g


def _silu(x):
    return x * jax.nn.sigmoid(x)


def _gelu_tanh(x):
    return 0.5 * x * (1.0 + jnp.tanh(0.7978845608028654 * (x + 0.044715 * (x * x * x))))


def _split3(x):
    hi = x.astype(BF16)
    r1 = x - hi.astype(F32)
    mid = r1.astype(BF16)
    lo = (r1 - mid.astype(F32)).astype(BF16)
    return hi, mid, lo


def _inproj_kernel(apply_ln, x_ref, g_ref, b_ref, w_ref, bf_ref, *outs):
    if apply_ln:
        x0_ref, outs = outs[0], outs[1:]
    q_ref, k_ref, v_ref, kb_ref, vb_ref, xr_ref, xg_ref, lf_ref = outs
    x = x_ref[...]
    if apply_ln:
        x = _layer_norm(x, g_ref[...], b_ref[...])
        x0_ref[...] = x
    h = x.astype(BF16)

    def mm(c0, c1):
        return jnp.dot(h, w_ref[:, c0:c1], preferred_element_type=F32)

    q_ref[...] = mm(C_Q, C_K).astype(BF16)
    k = mm(C_K, C_V)
    k_ref[...] = k
    kb_ref[...] = k.astype(BF16)
    v = mm(C_V, C_XR)
    v_ref[...] = v
    vb_ref[...] = v.astype(BF16)
    xr_ref[...] = mm(C_XR, C_XG)
    xg_ref[...] = mm(C_XG, C_F)
    f = mm(C_F, C_END) + bf_ref[...]
    lf_ref[...] = jnp.minimum(f, 0.0) - jnp.log1p(jnp.exp(-jnp.abs(f)))


def _inproj(x, ln_g, ln_b, w_pad, bf_pad, apply_ln):
    T = x.shape[0]
    row = lambda w: pl.BlockSpec((TM, w), lambda i: (i, 0))
    out_shape = [
        jax.ShapeDtypeStruct((T, ATT_W), BF16),
        jax.ShapeDtypeStruct((T, ATT_W), F32),
        jax.ShapeDtypeStruct((T, ATT_W), F32),
        jax.ShapeDtypeStruct((T, ATT_W), BF16),
        jax.ShapeDtypeStruct((T, ATT_W), BF16),
        jax.ShapeDtypeStruct((T, LRU_W), F32),
        jax.ShapeDtypeStruct((T, LRU_W), F32),
        jax.ShapeDtypeStruct((T, LANES), F32),
    ]
    out_specs = [row(ATT_W)] * 5 + [row(LRU_W)] * 2 + [row(LANES)]
    if apply_ln:
        out_shape = [jax.ShapeDtypeStruct((T, D_MODEL), F32)] + out_shape
        out_specs = [row(D_MODEL)] + out_specs
    return pl.pallas_call(
        functools.partial(_inproj_kernel, apply_ln),
        grid=(T // TM,),
        in_specs=[row(D_MODEL), _resident((1, D_MODEL)), _resident((1, D_MODEL)),
                  _resident((D_MODEL, C_END)), _resident((1, LANES))],
        out_specs=out_specs,
        out_shape=out_shape,
        compiler_params=_params(),
        name="inproj",
    )(x, ln_g, ln_b, w_pad, bf_pad)


def _cumsum_kernel(lf_ref, c_ref, *, S):
    carry = jnp.zeros((1, LANES), F32)
    for c0 in range(0, S, CUM_CHUNK):
        n = min(CUM_CHUNK, S - c0)
        r = lax.broadcasted_iota(I32, (n, n), 0)
        c = lax.broadcasted_iota(I32, (n, n), 1)
        tril = (r >= c).astype(BF16)
        hi, mid, lo = _split3(lf_ref[c0:c0 + n, :])
        acc = jnp.dot(tril, hi, preferred_element_type=F32)
        acc = acc + jnp.dot(tril, mid, preferred_element_type=F32)
        acc = acc + jnp.dot(tril, lo, preferred_element_type=F32)
        acc = acc + carry
        c_ref[c0:c0 + n, :] = acc
        carry = acc[n - 1:n, :]


def _cumsum(lf):
    B, S, _ = lf.shape
    spec = pl.BlockSpec((None, S, LANES), lambda b: (b, 0, 0))
    return pl.pallas_call(
        functools.partial(_cumsum_kernel, S=S),
        grid=(B,),
        in_specs=[spec],
        out_specs=spec,
        out_shape=jax.ShapeDtypeStruct((B, S, LANES), F32),
        compiler_params=_params(),
        name="cumsum_logf",
    )(lf)


def _attn_prompt_kernel(q_ref, k_ref, v_ref, c_ref, ct_ref, o_ref, *, S):
    nq = S // ATT_TQ
    for h0 in range(0, N_HEADS, ATT_HEAD_GROUP):
        heads = range(h0, h0 + ATT_HEAD_GROUP)

        def q_body(qi, carry, heads=heads):
            q0 = pl.multiple_of(qi * ATT_TQ, ATT_TQ)
            hsl = [slice(h * HEAD_DIM, (h + 1) * HEAD_DIM) for h in heads]
            qs = [q_ref[pl.ds(q0, ATT_TQ), hs] for hs in hsl]
            cqs = [c_ref[pl.ds(q0, ATT_TQ), h:h + 1] for h in heads]

            def kv_step(kj, sts, diagonal):
                k0 = pl.multiple_of(kj * ATT_TK, ATT_TK)
                out = []
                for h, hs, q, cq, (m, l, acc) in zip(heads, hsl, qs, cqs, sts):
                    k = k_ref[pl.ds(k0, ATT_TK), hs]
                    v = v_ref[pl.ds(k0, ATT_TK), hs]
                    ck = ct_ref[h, pl.ds(kj, 1), :]
                    s = lax.dot_general(q, k, (((1,), (1,)), ((), ())),
                                        preferred_element_type=F32) * ATT_SCALE
                    s = s + cq - ck
                    if diagonal:
                        row = lax.broadcasted_iota(I32, (ATT_TQ, ATT_TK), 0)
                        col = lax.broadcasted_iota(I32, (ATT_TQ, ATT_TK), 1)
                        s = jnp.where(row >= col, s, -jnp.inf)
                    m_new = jnp.maximum(m, jnp.max(s, axis=1, keepdims=True))
                    alpha = jnp.exp(m - m_new)
                    p = jnp.exp(s - m_new)
                    l = alpha * l + jnp.sum(p, axis=1, keepdims=True)
                    acc = alpha * acc + jnp.dot(p.astype(BF16), v, preferred_element_type=F32)
                    out.append((m_new, l, acc))
                return tuple(out)

            init = tuple((jnp.full((ATT_TQ, 1), -jnp.inf, F32), jnp.zeros((ATT_TQ, 1), F32),
                          jnp.zeros((ATT_TQ, HEAD_DIM), F32)) for _ in heads)
            sts = lax.fori_loop(0, qi, lambda kj, sts: kv_step(kj, sts, False), init)
            sts = kv_step(qi, sts, True)
            for hs, (m, l, acc) in zip(hsl, sts):
                o_ref[pl.ds(q0, ATT_TQ), hs] = acc / l
            return carry

        lax.fori_loop(0, nq, q_body, 0)


def _attn_prompt(qb, kb, vb, c, ct, B, S):
    T = B * S
    seq = pl.BlockSpec((S, ATT_W), lambda b: (b, 0))
    return pl.pallas_call(
        functools.partial(_attn_prompt_kernel, S=S),
        grid=(B,),
        in_specs=[seq, seq, seq,
                  pl.BlockSpec((None, S, LANES), lambda b: (b, 0, 0)),
                  pl.BlockSpec((None, N_HEADS, S // ATT_TK, ATT_TK), lambda b: (b, 0, 0, 0))],
        out_specs=seq,
        out_shape=jax.ShapeDtypeStruct((T, ATT_W), F32),
        compiler_params=_params(),
        name="fox_prompt",
    )(qb, kb, vb, c, ct)


def _attn_sample_kernel(q_ref, kn_ref, vn_ref, ck_ref, cv_ref, c_ref, ct_ref, o_ref, *, P, Tn):
    r = lax.broadcasted_iota(I32, (Tn, Tn), 0)
    c = lax.broadcasted_iota(I32, (Tn, Tn), 1)
    nt = (((1,), (1,)), ((), ()))
    for h in range(N_HEADS):
        hs = slice(h * HEAD_DIM, (h + 1) * HEAD_DIM)
        q = q_ref[:, hs]
        kc = ck_ref[pl.ds(h, P, stride=N_HEADS), :].astype(BF16)
        vc = cv_ref[pl.ds(h, P, stride=N_HEADS), :].astype(BF16)
        cq = c_ref[P:P + Tn, h:h + 1]
        ckr = ct_ref[h:h + 1, :]
        s_c = lax.dot_general(q, kc, nt, preferred_element_type=F32) * ATT_SCALE
        s_c = s_c + cq - ckr[:, :P]
        s_n = lax.dot_general(q, kn_ref[:, hs], nt, preferred_element_type=F32) * ATT_SCALE
        s_n = jnp.where(r >= c, s_n + cq - ckr[:, P:P + Tn], -jnp.inf)
        m = jnp.maximum(jnp.max(s_c, axis=1, keepdims=True), jnp.max(s_n, axis=1, keepdims=True))
        p_c = jnp.exp(s_c - m)
        p_n = jnp.exp(s_n - m)
        l = jnp.sum(p_c, axis=1, keepdims=True) + jnp.sum(p_n, axis=1, keepdims=True)
        o = jnp.dot(p_c.astype(BF16), vc, preferred_element_type=F32)
        o = o + jnp.dot(p_n.astype(BF16), vn_ref[:, hs], preferred_element_type=F32)
        o_ref[:, hs] = o / l


def _attn_sample(qb, kb, vb, cache_k, cache_v, layer, c, ct, row0, Bd, P, Tn):
    blk0 = row0 // Tn
    new = pl.BlockSpec((Tn, ATT_W), lambda b: (blk0 + b, 0))
    cache = pl.BlockSpec((None, None, P * N_HEADS, HEAD_DIM), lambda b: (layer, b, 0, 0))
    return pl.pallas_call(
        functools.partial(_attn_sample_kernel, P=P, Tn=Tn),
        grid=(Bd,),
        in_specs=[new, new, new, cache, cache,
                  pl.BlockSpec((None, P + Tn, LANES), lambda b: (b, 0, 0)),
                  pl.BlockSpec((None, N_HEADS, P + Tn), lambda b: (b, 0, 0))],
        out_specs=pl.BlockSpec((Tn, ATT_W), lambda b: (b, 0)),
        out_shape=jax.ShapeDtypeStruct((Bd * Tn, ATT_W), F32),
        compiler_params=_params(),
        name="fox_sample",
    )(qb, kb, vb, cache_k, cache_v, c, ct)


def _lru_kernel(xr_ref, xg_ref, cs_ref, h0_ref, cw_ref, cb_ref, wr_ref, br_ref, wi_ref, bi_ref,
                lam_ref, o_ref, cso_ref, hl_ref, xp_ref, a_ref, u_ref, hs_ref, h_ref, *, ts):
    s = pl.program_id(1)
    pad = SUBLANES
    nprev = CONV_W - 1

    @pl.when(s == 0)
    def _():
        xp_ref[pad - nprev:pad, :] = cs_ref[...]
        h_ref[...] = h0_ref[...]

    x = xr_ref[...]
    xp_ref[pad:pad + ts, :] = x
    xc = cb_ref[...] + cw_ref[nprev:nprev + 1, :] * x
    for i in range(nprev):
        xc = xc + cw_ref[i:i + 1, :] * xp_ref[pad - nprev + i:pad - nprev + i + ts, :]
    tail = xp_ref[pad + ts - nprev:pad + ts, :]
    xp_ref[pad - nprev:pad, :] = tail
    cso_ref[...] = tail

    xcb = xc.astype(BF16)

    def block_diag(w_ref, b_ref):
        parts = [jnp.dot(xcb[:, n * LRU_BW:(n + 1) * LRU_BW], w_ref[n], preferred_element_type=F32)
                 for n in range(LRU_BLOCKS)]
        return jnp.concatenate(parts, axis=1) + b_ref[...]

    r = jax.nn.sigmoid(block_diag(wr_ref, br_ref))
    g = jax.nn.sigmoid(block_diag(wi_ref, bi_ref))
    nl = -lam_ref[...]
    softplus = jnp.maximum(nl, 0.0) + jnp.log1p(jnp.exp(-jnp.abs(nl)))
    log_a = (-LRU_C) * r * softplus
    a = jnp.exp(log_a)
    u = jnp.sqrt(1.0 - a * a) * g * xc

    sub = lax.broadcasted_iota(I32, (ts, LRU_W), 0) % SUBLANES
    d = 1
    while d < SUBLANES:
        keep = sub >= d
        u = jnp.where(keep, u + a * pltpu.roll(u, d, axis=0), u)
        a = jnp.where(keep, a * pltpu.roll(a, d, axis=0), a)
        d *= 2
    a_ref[...] = a
    u_ref[...] = u

    def group(gi, hprev):
        r0 = pl.multiple_of(gi * SUBLANES, SUBLANES)
        hh = u_ref[pl.ds(r0, SUBLANES), :] + a_ref[pl.ds(r0, SUBLANES), :] * hprev
        hs_ref[pl.ds(r0, SUBLANES), :] = hh
        return hh[SUBLANES - 1:SUBLANES, :]

    hlast = lax.fori_loop(0, ts // SUBLANES, group, h_ref[...])
    h_ref[...] = hlast
    hl_ref[...] = hlast
    o_ref[...] = hs_ref[...] * _gelu_tanh(xg_ref[...])


def _lru(xr, xg, conv_state, h0, lw, row0, nb, S, ts):
    ns = S // ts
    blk0 = row0 // ts
    rows = pl.BlockSpec((ts, LRU_W), lambda b, s: (blk0 + b * ns + s, 0))
    vec = lambda n: _resident((n, LRU_W))
    per_seq = lambda n: pl.BlockSpec((None, n, LRU_W), lambda b, s: (b, 0, 0))
    in_specs = [rows, rows, per_seq(CONV_W - 1), per_seq(1), vec(CONV_W), vec(1),
                _resident((LRU_BLOCKS, LRU_BW, LRU_BW)), vec(1),
                _resident((LRU_BLOCKS, LRU_BW, LRU_BW)), vec(1), vec(1)]
    args = [xr, xg, conv_state, h0, lw["conv_w"], lw["conv_b"], lw["w_r"], lw["b_r"],
            lw["w_i"], lw["b_i"], lw["lam"]]
    return pl.pallas_call(
        functools.partial(_lru_kernel, ts=ts),
        grid=(nb, ns),
        in_specs=in_specs,
        out_specs=[pl.BlockSpec((ts, LRU_W), lambda b, s: (b * ns + s, 0)),
                   per_seq(CONV_W - 1), per_seq(1)],
        out_shape=[jax.ShapeDtypeStruct((nb * S, LRU_W), F32),
                   jax.ShapeDtypeStruct((nb, CONV_W - 1, LRU_W), F32),
                   jax.ShapeDtypeStruct((nb, 1, LRU_W), F32)],
        scratch_shapes=[pltpu.VMEM((SUBLANES + ts, LRU_W), F32),
                        pltpu.VMEM((ts, LRU_W), F32), pltpu.VMEM((ts, LRU_W), F32),
                        pltpu.VMEM((ts, LRU_W), F32), pltpu.VMEM((1, LRU_W), F32)],
        compiler_params=_params(2),
        name="rg_lru",
    )(*args)


def _outproj_kernel(oap_ref, oas_ref, olp_ref, ols_ref, x_ref, w_ref, ga_ref, gl_ref, lg_ref,
                    lb_ref, wrh_ref, wrl_ref, rb_ref, x1_ref, x1s_ref, idx_ref, gate_ref,
                    *, alpha, n_prompt_blocks):
    is_prompt = pl.program_id(0) < n_prompt_blocks
    oa = jnp.where(is_prompt, oap_ref[...], oas_ref[...])
    ol = jnp.where(is_prompt, olp_ref[...], ols_ref[...])
    na = _rms_norm(oa, ga_ref[...]).astype(BF16)
    nl = _rms_norm(ol, gl_ref[...]).astype(BF16)
    m = jnp.dot(na, w_ref[0:ATT_W, :], preferred_element_type=F32)
    m = m + jnp.dot(nl, w_ref[ATT_W:D_MODEL, :], preferred_element_type=F32)
    x1 = _layer_norm(alpha * x_ref[...] + m, lg_ref[...], lb_ref[...])
    x1_ref[...] = x1
    _matrix_to_rows(x1s_ref, x1, x1.shape[0])

    xh = x1.astype(BF16)
    xl = (x1 - xh.astype(F32)).astype(BF16)
    logits = jnp.dot(xh, wrh_ref[...], preferred_element_type=F32)
    logits = logits + jnp.dot(xl, wrh_ref[...], preferred_element_type=F32)
    logits = logits + jnp.dot(xh, wrl_ref[...], preferred_element_type=F32)
    scores = jax.nn.sigmoid(logits)
    tm = scores.shape[0]
    lane = lax.broadcasted_iota(I32, (tm, LANES), 1)
    lane_f = lane.astype(F32)
    sel = jnp.where(lane < N_EXPERTS, scores + rb_ref[...], -jnp.inf)
    idx_out = jnp.zeros((tm, LANES), F32)
    gate_out = jnp.zeros((tm, LANES), F32)
    gsum = jnp.zeros((tm, 1), F32)
    for k in range(TOP_K):
        mx = jnp.max(sel, axis=1, keepdims=True)
        ik = jnp.min(jnp.where(sel == mx, lane_f, float(LANES)), axis=1, keepdims=True)
        hit = lane_f == ik
        gk = jnp.sum(jnp.where(hit, scores, 0.0), axis=1, keepdims=True)
        sel = jnp.where(hit, -jnp.inf, sel)
        idx_out = jnp.where(lane == k, ik, idx_out)
        gate_out = jnp.where(lane == k, gk, gate_out)
        gsum = gsum + gk
    idx_ref[...] = idx_out.astype(I32)
    gate_ref[...] = gate_out / gsum * ROUTE_SCALE


def _outproj(oa_p, oa_s, ol_p, ol_s, x, lw, alpha):
    T = x.shape[0]
    n_p = oa_p.shape[0] // TM
    row = lambda w: pl.BlockSpec((TM, w), lambda i: (i, 0))
    prompt = lambda w: pl.BlockSpec((TM, w), lambda i: (jnp.minimum(i, n_p - 1), 0))
    sample = lambda w: pl.BlockSpec((TM, w), lambda i: (jnp.maximum(i - n_p, 0), 0))
    vec = lambda n: _resident((1, n))
    return pl.pallas_call(
        functools.partial(_outproj_kernel, alpha=alpha, n_prompt_blocks=n_p),
        grid=(T // TM,),
        in_specs=[prompt(ATT_W), sample(ATT_W), prompt(LRU_W), sample(LRU_W), row(D_MODEL),
                  _resident((D_MODEL, D_MODEL)),
                  vec(ATT_W), vec(LRU_W), vec(D_MODEL), vec(D_MODEL),
                  _resident((D_MODEL, LANES)), _resident((D_MODEL, LANES)), vec(LANES)],
        out_specs=[row(D_MODEL), pl.BlockSpec((TM * ROW_CHUNKS, LANES), lambda i: (i, 0)),
                   row(LANES), row(LANES)],
        out_shape=[jax.ShapeDtypeStruct((T, D_MODEL), F32),
                   jax.ShapeDtypeStruct((T * ROW_CHUNKS, LANES), F32),
                   jax.ShapeDtypeStruct((T, LANES), I32),
                   jax.ShapeDtypeStruct((T, LANES), F32)],
        compiler_params=_params(),
        name="outproj_ln_router",
    )(oa_p, oa_s, ol_p, ol_s, x, lw["w_out"], lw["g_att"], lw["g_lru"], lw["ln1_g"], lw["ln1_b"],
      lw["wr_hi"], lw["wr_lo"], lw["b_router"])


def _rank_kernel(idx_ref, rank_ref, cnt_ref, carry_ref):
    @pl.when(pl.program_id(0) == 0)
    def _():
        carry_ref[...] = jnp.zeros_like(carry_ref)

    tm = idx_ref.shape[0]
    idx = idx_ref[...]
    lane = lax.broadcasted_iota(I32, (tm, LANES), 1)
    onehot = jnp.zeros((tm, LANES), F32)
    for k in range(TOP_K):
        onehot = onehot + (lane == idx[:, k:k + 1]).astype(F32)
    r = lax.broadcasted_iota(I32, (tm, tm), 0)
    c = lax.broadcasted_iota(I32, (tm, tm), 1)
    below = (r > c).astype(BF16)
    rank = jnp.dot(below, onehot.astype(BF16), preferred_element_type=F32) + carry_ref[...]
    out = jnp.zeros((tm, LANES), F32)
    for k in range(TOP_K):
        rk = jnp.sum(jnp.where(lane == idx[:, k:k + 1], rank, 0.0), axis=1, keepdims=True)
        out = jnp.where(lane == k, rk, out)
    rank_ref[...] = out.astype(I32)
    total = carry_ref[...] + jnp.sum(onehot, axis=0, keepdims=True)
    carry_ref[...] = total
    cnt_ref[...] = total.astype(I32)


def _rank(idx):
    T = idx.shape[0]
    row = pl.BlockSpec((TM, LANES), lambda i: (i, 0))
    return pl.pallas_call(
        _rank_kernel,
        grid=(T // TM,),
        in_specs=[row],
        out_specs=[row, pl.BlockSpec((1, LANES), lambda i: (0, 0))],
        out_shape=[jax.ShapeDtypeStruct((T, LANES), I32), jax.ShapeDtypeStruct((1, LANES), I32)],
        scratch_shapes=[pltpu.VMEM((1, LANES), F32)],
        compiler_params=_params(),
        name="route_rank",
    )(idx)


def _slot_table_kernel(ps_ref, idx_hbm, rank_hbm, init_hbm, tab_hbm, tab, ichunk, rchunk, sem):
    i = pl.program_id(0)
    n = ROUTE_CHUNK

    @pl.when(i == 0)
    def _():
        cp = pltpu.make_async_copy(init_hbm, tab, sem.at[0])
        cp.start()
        cp.wait()

    off = pl.multiple_of(i * n, n)
    c1 = pltpu.make_async_copy(idx_hbm.at[pl.ds(off, n)], ichunk, sem.at[1])
    c2 = pltpu.make_async_copy(rank_hbm.at[pl.ds(off, n)], rchunk, sem.at[2])
    c1.start()
    c2.start()
    c1.wait()
    c2.wait()

    def group(j8, carry):
        for u in range(SUBLANES):
            j = j8 * SUBLANES + u
            tab[TAB_CHUNK + ps_ref[ichunk[j]] + rchunk[j]] = off + j
        return carry

    lax.fori_loop(0, n // SUBLANES, group, 0)

    @pl.when(i == pl.num_programs(0) - 1)
    def _():
        cp = pltpu.make_async_copy(tab, tab_hbm, sem.at[0])
        cp.start()
        cp.wait()


def _slot_table(pad_start, idx_flat, rank_flat, n_slots):
    n = idx_flat.shape[0]
    n_tokens = n // TOP_K
    n_tab = n_slots + TAB_CHUNK
    sentinel = ((n_tokens + jnp.arange(n_tab, dtype=I32) % MOE_BLK) << TOP_K_SHIFT) | (TOP_K - 1)
    any_spec = pl.BlockSpec(memory_space=pl.ANY)
    grid_spec = pltpu.PrefetchScalarGridSpec(
        num_scalar_prefetch=1,
        grid=(n // ROUTE_CHUNK,),
        in_specs=[any_spec, any_spec, any_spec],
        out_specs=any_spec,
        scratch_shapes=[pltpu.SMEM((n_tab,), I32), pltpu.SMEM((ROUTE_CHUNK,), I32),
                        pltpu.SMEM((ROUTE_CHUNK,), I32), pltpu.SemaphoreType.DMA((3,))],
    )
    return pl.pallas_call(
        _slot_table_kernel,
        grid_spec=grid_spec,
        out_shape=jax.ShapeDtypeStruct((n_tab,), I32),
        compiler_params=_params(),
        name="moe_slot_table",
    )(pad_start, idx_flat, rank_flat, sentinel)


def _rows_to_matrix(ref, n):
    return jnp.concatenate([ref[pl.ds(j, n, stride=ROW_CHUNKS), :] for j in range(ROW_CHUNKS)],
                           axis=1)


def _matrix_to_rows(ref, y, n):
    for j in range(ROW_CHUNKS):
        ref[pl.ds(j, n, stride=ROW_CHUNKS), :] = y[:, j * LANES:(j + 1) * LANES]


def _experts_kernel(first_ref, nu_ref, x1s_hbm, tab_hbm, wg_ref, wu_ref, wd_ref, yk_hbm,
                    wg_s, wu_s, wd_s, xbuf0, xbuf1, ybuf0, ybuf1, stk, gsem, ssem, tsem,
                    *, n_tokens):
    e = pl.program_id(0)
    nu = nu_ref[0]
    blk_rows = MOE_BLK * ROW_CHUNKS
    xbufs = (xbuf0, xbuf1)
    ybufs = (ybuf0, ybuf1)

    def table_copy(g):
        gt = g + TAB_RING
        q = gt % TAB_RING
        src = tab_hbm.at[pl.ds(pl.multiple_of((gt // TAB_RING) * TAB_CHUNK, TAB_CHUNK), TAB_CHUNK)]
        dst = stk.at[pl.ds(pl.multiple_of(q * TAB_CHUNK, TAB_CHUNK), TAB_CHUNK)]
        return pltpu.make_async_copy(src, dst, tsem.at[q])

    def table_base(g):
        return ((g + TAB_RING) % TAB_RING) * (TAB_CHUNK + MOE_BLK)

    def gather_row(base, r, xb, sem):
        tok = jnp.minimum(stk[base + r] >> TOP_K_SHIFT, n_tokens - 1)
        src = x1s_hbm.at[pl.ds(pl.multiple_of(tok * ROW_CHUNKS, ROW_CHUNKS), ROW_CHUNKS)]
        pltpu.make_async_copy(src, xb.at[pl.ds(r * ROW_CHUNKS, ROW_CHUNKS)], sem).start()

    def scatter_row(base, r, yb, sem):
        v = stk[base + r]
        row = (v & (TOP_K - 1)) * n_tokens + (v >> TOP_K_SHIFT)
        dst = yk_hbm.at[pl.ds(pl.multiple_of(row * ROW_CHUNKS, ROW_CHUNKS), ROW_CHUNKS)]
        pltpu.make_async_copy(yb.at[pl.ds(r * ROW_CHUNKS, ROW_CHUNKS)], dst, sem).start()

    def wait_gather(xb, sem):
        pltpu.make_async_copy(x1s_hbm.at[pl.ds(0, blk_rows)], xb, sem).wait()

    def wait_scatter(yb, sem):
        pltpu.make_async_copy(yb, yk_hbm.at[pl.ds(0, blk_rows)], sem).wait()

    @pl.when(e == 0)
    def _():
        ybuf1[...] = jnp.zeros_like(ybuf1)
        table_copy(-1).start()
        table_copy(0).start()

        @pl.when(nu > 1)
        def _():
            table_copy(1).start()

        table_copy(-1).wait()
        table_copy(0).wait()
        base = table_base(0)
        for r in range(MOE_BLK):
            gather_row(base, r, xbuf0, gsem.at[0])

    g0 = first_ref[e]
    g1 = first_ref[e + 1]

    @pl.when(g1 > g0)
    def _():
        wg_s[...] = wg_ref[...].astype(BF16)
        wu_s[...] = wu_ref[...].astype(BF16)
        wd_s[...] = wd_ref[...].astype(BF16)

    def step(g, p):
        xb_cur, xb_nxt = xbufs[p], xbufs[1 - p]
        yb_cur, yb_prv = ybufs[p], ybufs[1 - p]

        @pl.when(g + 2 < nu)
        def _():
            table_copy(g + 2).start()

        @pl.when(g + 1 < nu)
        def _():
            table_copy(g + 1).wait()

        wait_gather(xb_cur, gsem.at[p])

        @pl.when(g >= 1)
        def _():
            wait_scatter(yb_cur, ssem.at[p])

        base_next = table_base(jnp.minimum(g + 1, nu - 1))
        base_prev = table_base(g - 1)
        for r in range(MOE_BLK):
            gather_row(base_next, r, xb_nxt, gsem.at[1 - p])
        for r in range(MOE_BLK):
            scatter_row(base_prev, r, yb_prv, ssem.at[1 - p])
        x = _rows_to_matrix(xb_cur, MOE_BLK).astype(BF16)
        gt = jnp.dot(x, wg_s[...], preferred_element_type=F32)
        up = jnp.dot(x, wu_s[...], preferred_element_type=F32)
        h = (_silu(gt) * up).astype(BF16)
        y = jnp.dot(h, wd_s[...], preferred_element_type=F32)
        _matrix_to_rows(yb_cur, y, MOE_BLK)

        @pl.when(g == nu - 1)
        def _():
            wait_scatter(yb_prv, ssem.at[1 - p])
            base = table_base(g)

            def group(i, carry):
                for u in range(SUBLANES):
                    r = i * SUBLANES + u
                    v = stk[base + r]
                    row = (v & (TOP_K - 1)) * n_tokens + (v >> TOP_K_SHIFT)
                    src = yb_cur.at[pl.ds(pl.multiple_of(r * ROW_CHUNKS, ROW_CHUNKS), ROW_CHUNKS)]
                    dst = yk_hbm.at[pl.ds(pl.multiple_of(row * ROW_CHUNKS, ROW_CHUNKS), ROW_CHUNKS)]
                    pltpu.make_async_copy(src, dst, ssem.at[p]).start()
                return carry

            lax.fori_loop(0, MOE_BLK // SUBLANES, group, 0)
            wait_scatter(yb_cur, ssem.at[p])
            wait_gather(xb_nxt, gsem.at[1 - p])

    def block(g, carry):
        @pl.when(g % 2 == 0)
        def _():
            step(g, 0)

        @pl.when(g % 2 == 1)
        def _():
            step(g, 1)

        return carry

    lax.fori_loop(g0, g1, block, 0)


def _experts(first_blk, n_used, x1s, slot_tab, we_gate, we_up, we_down, layer):
    n_tokens = x1s.shape[0] // ROW_CHUNKS
    blk_rows = MOE_BLK * ROW_CHUNKS
    any_spec = pl.BlockSpec(memory_space=pl.ANY)
    w_in_spec = pl.BlockSpec((None, None, D_MODEL, D_EXPERT), lambda e, *_: (layer, e, 0, 0))
    grid_spec = pltpu.PrefetchScalarGridSpec(
        num_scalar_prefetch=2,
        grid=(N_EXPERTS,),
        in_specs=[any_spec, any_spec, w_in_spec, w_in_spec,
                  pl.BlockSpec((None, None, D_EXPERT, D_MODEL), lambda e, *_: (layer, e, 0, 0))],
        out_specs=any_spec,
        scratch_shapes=[pltpu.VMEM((D_MODEL, D_EXPERT), BF16), pltpu.VMEM((D_MODEL, D_EXPERT), BF16),
                        pltpu.VMEM((D_EXPERT, D_MODEL), BF16),
                        pltpu.VMEM((blk_rows, LANES), F32), pltpu.VMEM((blk_rows, LANES), F32),
                        pltpu.VMEM((blk_rows, LANES), F32), pltpu.VMEM((blk_rows, LANES), F32),
                        pltpu.SMEM((TAB_RING * TAB_CHUNK,), I32),
                        pltpu.SemaphoreType.DMA((2,)), pltpu.SemaphoreType.DMA((2,)),
                        pltpu.SemaphoreType.DMA((TAB_RING,))],
    )
    return pl.pallas_call(
        functools.partial(_experts_kernel, n_tokens=n_tokens),
        grid_spec=grid_spec,
        out_shape=jax.ShapeDtypeStruct(((TOP_K * n_tokens + MOE_BLK) * ROW_CHUNKS, LANES), F32),
        compiler_params=_params(),
        name="moe_experts",
    )(first_blk, n_used, x1s, slot_tab, we_gate, we_up, we_down)


def _combine_kernel(*refs, alpha):
    yk_refs = refs[:TOP_K]
    gate_ref, x_ref, wsg_ref, wsu_ref, wsd_ref, lg_ref, lb_ref, o_ref = refs[TOP_K:]
    tm = x_ref.shape[0]
    x1 = x_ref[...]
    xb = x1.astype(BF16)
    hs = _silu(jnp.dot(xb, wsg_ref[...], preferred_element_type=F32))
    hs = (hs * jnp.dot(xb, wsu_ref[...], preferred_element_type=F32)).astype(BF16)
    acc = jnp.dot(hs, wsd_ref[...], preferred_element_type=F32)
    gates = gate_ref[...]
    for k in range(TOP_K):
        acc = acc + gates[:, k:k + 1] * _rows_to_matrix(yk_refs[k], tm)
    o_ref[...] = _layer_norm(alpha * x1 + acc, lg_ref[...], lb_ref[...])


def _combine(yk, gates, x1, lw, alpha):
    T = x1.shape[0]
    tm = TM_COMBINE
    row = lambda w: pl.BlockSpec((tm, w), lambda i: (i, 0))
    plane = lambda k: pl.BlockSpec((tm * ROW_CHUNKS, LANES), lambda i: (k * (T // tm) + i, 0))
    return pl.pallas_call(
        functools.partial(_combine_kernel, alpha=alpha),
        grid=(T // tm,),
        in_specs=[plane(k) for k in range(TOP_K)] + [
                  row(LANES), row(D_MODEL),
                  _resident((D_MODEL, D_SHARED)), _resident((D_MODEL, D_SHARED)),
                  _resident((D_SHARED, D_MODEL)), _resident((1, D_MODEL)), _resident((1, D_MODEL))],
        out_specs=row(D_MODEL),
        out_shape=jax.ShapeDtypeStruct((T, D_MODEL), F32),
        compiler_params=_params(),
        name="moe_combine",
    )(*([yk] * TOP_K), gates, x1, lw["ws_gate"], lw["ws_up"], lw["ws_down"], lw["ln2_g"],
      lw["ln2_b"])


def _block_tables(counts):
    padded = (counts + MOE_BLK - 1) // MOE_BLK * MOE_BLK
    pad_end = jnp.cumsum(padded)
    pad_start = pad_end - padded
    first_blk = jnp.concatenate([pad_start, pad_end[-1:]]) // MOE_BLK
    n_used = pad_end[-1:] // MOE_BLK
    return pad_start.astype(I32), first_blk.astype(I32), n_used.astype(I32)


def _moe(x1, x1s, idx, gates, lw, we_gate, we_up, we_down, layer, alpha):
    T = x1.shape[0]
    n_blocks = T * TOP_K // MOE_BLK + N_EXPERTS
    rank, cnt = _rank(idx)
    pad_start, first_blk, n_used = _block_tables(cnt[0, :N_EXPERTS])
    slot_tab = _slot_table(pad_start, idx[:, :TOP_K].reshape(-1), rank[:, :TOP_K].reshape(-1),
                           n_blocks * MOE_BLK)
    yk = _experts(first_blk, n_used, x1s, slot_tab, we_gate, we_up, we_down, layer)
    return _combine(yk, gates, x1, lw, alpha)


def _prep_layer(l, w_in, b_f, w_out, g_att, g_lru, conv_w, conv_b, w_r, b_r, w_i, b_i, lru_lambda,
                ln1_g, ln1_b, w_router, b_router, ws_gate, ws_up, ws_down, ln2_g, ln2_b):
    qkv_end = 3 * ATT_W
    wl = w_in[l]
    w_pad = jnp.concatenate(
        [wl[:, :qkv_end], wl[:, qkv_end + N_HEADS:],
         jnp.pad(wl[:, qkv_end:qkv_end + N_HEADS], ((0, 0), (0, LANES - N_HEADS)))],
        axis=1).astype(BF16)
    wr = jnp.pad(w_router[l], ((0, 0), (0, LANES - N_EXPERTS)))
    wr_hi = wr.astype(BF16)
    return dict(
        w_pad=w_pad,
        bf_pad=jnp.pad(b_f[l], (0, LANES - N_HEADS)).reshape(1, LANES),
        w_out=w_out[l].astype(BF16),
        g_att=g_att[l].reshape(1, ATT_W), g_lru=g_lru[l].reshape(1, LRU_W),
        conv_w=conv_w[l], conv_b=conv_b[l].reshape(1, LRU_W),
        w_r=w_r[l].astype(BF16), b_r=b_r[l].reshape(1, LRU_W),
        w_i=w_i[l].astype(BF16), b_i=b_i[l].reshape(1, LRU_W),
        lam=lru_lambda[l].reshape(1, LRU_W),
        ln1_g=ln1_g[l].reshape(1, D_MODEL), ln1_b=ln1_b[l].reshape(1, D_MODEL),
        wr_hi=wr_hi, wr_lo=(wr - wr_hi.astype(F32)).astype(BF16),
        b_router=jnp.pad(b_router[l], (0, LANES - N_EXPERTS)).reshape(1, LANES),
        ws_gate=ws_gate[l].astype(BF16), ws_up=ws_up[l].astype(BF16),
        ws_down=ws_down[l].astype(BF16),
        ln2_g=ln2_g[l].reshape(1, D_MODEL), ln2_b=ln2_b[l].reshape(1, D_MODEL),
    )


def kernel(x_prompt, x_sample, cache_k, cache_v, cache_logf, state_conv, state_lru, ln_in_g, ln_in_b, w_in, b_f, w_out, g_att, g_lru, conv_w, conv_b, w_r, b_r, w_i, b_i, lru_lambda, ln1_g, ln1_b, w_router, b_router, we_gate, we_up, we_down, ws_gate, ws_up, ws_down, ln2_g, ln2_b):
    B, S, _ = x_prompt.shape
    Bd, Tn, _ = x_sample.shape
    depth = w_in.shape[0]
    P = cache_k.shape[2]
    Tp = B * S
    Ts = Bd * Tn
    alpha = (2 * depth) ** 0.25

    x = jnp.concatenate([x_prompt.reshape(Tp, D_MODEL), x_sample.reshape(Ts, D_MODEL)], axis=0)
    ln_g = ln_in_g.reshape(1, D_MODEL)
    ln_b = ln_in_b.reshape(1, D_MODEL)
    zero_conv = jnp.zeros((B, CONV_W - 1, LRU_W), F32)
    zero_h = jnp.zeros((B, 1, LRU_W), F32)

    ks, vs, lfs, p_conv, p_lru, s_conv, s_lru = [], [], [], [], [], [], []
    for l in range(depth):
        lw = _prep_layer(l, w_in, b_f, w_out, g_att, g_lru, conv_w, conv_b, w_r, b_r, w_i, b_i,
                         lru_lambda, ln1_g, ln1_b, w_router, b_router, ws_gate, ws_up, ws_down,
                         ln2_g, ln2_b)
        outs = _inproj(x, ln_g, ln_b, lw["w_pad"], lw["bf_pad"], apply_ln=(l == 0))
        if l == 0:
            x, outs = outs[0], outs[1:]
        qb, k, v, kb, vb, xr, xg, lf = outs

        c_p = _cumsum(lf[:Tp].reshape(B, S, LANES))
        ct_p = c_p[:, :, :N_HEADS].transpose(0, 2, 1).reshape(B, N_HEADS, S // ATT_TK, ATT_TK)
        lf_cache = jnp.pad(cache_logf[l], ((0, 0), (0, 0), (0, LANES - N_HEADS)))
        c_s = _cumsum(jnp.concatenate([lf_cache, lf[Tp:].reshape(Bd, Tn, LANES)], axis=1))
        ct_s = c_s[:, :, :N_HEADS].transpose(0, 2, 1)

        oa_p = _attn_prompt(qb, kb, vb, c_p, ct_p, B, S)
        oa_s = _attn_sample(qb, kb, vb, cache_k.reshape(depth, Bd, P * N_HEADS, HEAD_DIM),
                            cache_v.reshape(depth, Bd, P * N_HEADS, HEAD_DIM), l, c_s, ct_s,
                            Tp, Bd, P, Tn)

        ol_p, pc, ph = _lru(xr, xg, zero_conv, zero_h, lw, 0, B, S, LRU_TS)
        ol_s, sc, sh = _lru(xr, xg, state_conv[l], state_lru[l].reshape(Bd, 1, LRU_W), lw,
                            Tp, Bd, Tn, Tn)

        x1, x1s, idx, gates = _outproj(oa_p, oa_s, ol_p, ol_s, x, lw, alpha)
        x = _moe(x1, x1s, idx, gates, lw, we_gate, we_up, we_down, l, alpha)

        ks.append(k)
        vs.append(v)
        lfs.append(lf[:, :N_HEADS])
        p_conv.append(pc)
        p_lru.append(ph.reshape(B, LRU_W))
        s_conv.append(sc)
        s_lru.append(sh.reshape(Bd, LRU_W))

    def split(ts, tail):
        p = jnp.stack([t[:Tp].reshape((B, S) + tail) for t in ts])
        s = jnp.stack([t[Tp:].reshape((Bd, Tn) + tail) for t in ts])
        return p, s

    p_k, s_k = split(ks, (N_HEADS, HEAD_DIM))
    p_v, s_v = split(vs, (N_HEADS, HEAD_DIM))
    p_logf, s_logf = split(lfs, (N_HEADS,))
    y_prompt = x[:Tp].reshape(B, S, D_MODEL)
    y_sample = x[Tp:].reshape(Bd, Tn, D_MODEL)
    return (y_prompt, y_sample, p_k, p_v, p_logf, jnp.stack(p_conv), jnp.stack(p_lru),
            s_k, s_v, s_logf, jnp.stack(s_conv), jnp.stack(s_lru))
```

```python
import functools

import jax
import jax.numpy as jnp
from jax import lax
from jax.experimental import pallas as pl
from jax.experimental.pallas import tpu as pltpu

F32 = jnp.float32
BF16 = jnp.bfloat16
I32 = jnp.int32
U32 = jnp.uint32

D_MODEL = 2048
N_HEADS = 8
HEAD_DIM = 128
ATT_W = N_HEADS * HEAD_DIM
LRU_W = D_MODEL - ATT_W
LRU_BLOCKS = 8
LRU_BW = LRU_W // LRU_BLOCKS
CONV_W = 4
LRU_C = 8.0
N_EXPERTS = 64
TOP_K = 8
TOP_K_SHIFT = TOP_K.bit_length() - 1
assert 1 << TOP_K_SHIFT == TOP_K
D_EXPERT = 512
D_SHARED = 512
ROUTE_SCALE = 2.5
ATT_SCALE = HEAD_DIM ** -0.5
LN_EPS = 1e-5
RMS_EPS = 1e-6

LANES = 128
SUBLANES = 8
SLAB_ROWS = D_MODEL // (2 * LANES)
VMEM_LIMIT = 56 * 1024 * 1024

TM = 256
MOE_BLK = 256
TM_COMBINE = 128
ROUTE_CHUNK = 2048
TAB_CHUNK = 1024
TAB_RING = TAB_CHUNK // MOE_BLK
ATT_TQ = 512
ATT_TK = ATT_TQ
ATT_HEAD_GROUP = 2
LRU_TS = 256
CUM_CHUNK = 128

C_Q, C_K, C_V, C_XR, C_XG, C_F, C_END = 0, 1024, 2048, 3072, 4096, 5120, 5248


def _params(n_axes=1):
    return pltpu.CompilerParams(dimension_semantics=("arbitrary",) * n_axes,
                                vmem_limit_bytes=VMEM_LIMIT)


def _resident(shape):
    nd = len(shape)
    return pl.BlockSpec(shape, lambda *_: (0,) * nd, pipeline_mode=pl.Buffered(1))


def _layer_norm(x, g, b):
    mu = jnp.mean(x, axis=-1, keepdims=True)
    xc = x - mu
    var = jnp.mean(xc * xc, axis=-1, keepdims=True)
    return xc * lax.rsqrt(var + LN_EPS) * g + b


def _rms_norm(x, g):
    return x * lax.rsqrt(jnp.mean(x * x, axis=-1, keepdims=True) + RMS_EPS) * g


def _silu(x):
    return x * jax.nn.sigmoid(x)


def _gelu_tanh(x):
    return 0.5 * x * (1.0 + jnp.tanh(0.7978845608028654 * (x + 0.044715 * (x * x * x))))


def _split3(x):
    hi = x.astype(BF16)
    r1 = x - hi.astype(F32)
    mid = r1.astype(BF16)
    lo = (r1 - mid.astype(F32)).astype(BF16)
    return hi, mid, lo


def _inproj_kernel(apply_ln, x_ref, g_ref, b_ref, w_ref, bf_ref, *outs):
    if apply_ln:
        x0_ref, outs = outs[0], outs[1:]
    q_ref, k_ref, v_ref, kb_ref, vb_ref, xr_ref, xg_ref, lf_ref = outs
    x = x_ref[...]
    if apply_ln:
        x = _layer_norm(x, g_ref[...], b_ref[...])
        x0_ref[...] = x
    h = x.astype(BF16)

    def mm(c0, c1):
        return jnp.dot(h, w_ref[:, c0:c1], preferred_element_type=F32)

    q_ref[...] = mm(C_Q, C_K).astype(BF16)
    k = mm(C_K, C_V)
    k_ref[...] = k
    kb_ref[...] = k.astype(BF16)
    v = mm(C_V, C_XR)
    v_ref[...] = v
    vb_ref[...] = v.astype(BF16)
    xr_ref[...] = mm(C_XR, C_XG)
    xg_ref[...] = mm(C_XG, C_F)
    f = mm(C_F, C_END) + bf_ref[...]
    lf_ref[...] = jnp.minimum(f, 0.0) - jnp.log1p(jnp.exp(-jnp.abs(f)))


def _inproj(x, ln_g, ln_b, w_pad, bf_pad, apply_ln):
    T = x.shape[0]
    row = lambda w: pl.BlockSpec((TM, w), lambda i: (i, 0))
    out_shape = [
        jax.ShapeDtypeStruct((T, ATT_W), BF16),
        jax.ShapeDtypeStruct((T, ATT_W), F32),
        jax.ShapeDtypeStruct((T, ATT_W), F32),
        jax.ShapeDtypeStruct((T, ATT_W), BF16),
        jax.ShapeDtypeStruct((T, ATT_W), BF16),
        jax.ShapeDtypeStruct((T, LRU_W), F32),
        jax.ShapeDtypeStruct((T, LRU_W), F32),
        jax.ShapeDtypeStruct((T, LANES), F32),
    ]
    out_specs = [row(ATT_W)] * 5 + [row(LRU_W)] * 2 + [row(LANES)]
    if apply_ln:
        out_shape = [jax.ShapeDtypeStruct((T, D_MODEL), F32)] + out_shape
        out_specs = [row(D_MODEL)] + out_specs
    return pl.pallas_call(
        functools.partial(_inproj_kernel, apply_ln),
        grid=(T // TM,),
        in_specs=[row(D_MODEL), _resident((1, D_MODEL)), _resident((1, D_MODEL)),
                  _resident((D_MODEL, C_END)), _resident((1, LANES))],
        out_specs=out_specs,
        out_shape=out_shape,
        compiler_params=_params(),
        name="inproj",
    )(x, ln_g, ln_b, w_pad, bf_pad)


def _cumsum_kernel(lf_ref, c_ref, *, S):
    carry = jnp.zeros((1, LANES), F32)
    for c0 in range(0, S, CUM_CHUNK):
        n = min(CUM_CHUNK, S - c0)
        r = lax.broadcasted_iota(I32, (n, n), 0)
        c = lax.broadcasted_iota(I32, (n, n), 1)
        tril = (r >= c).astype(BF16)
        hi, mid, lo = _split3(lf_ref[c0:c0 + n, :])
        acc = jnp.dot(tril, hi, preferred_element_type=F32)
        acc = acc + jnp.dot(tril, mid, preferred_element_type=F32)
        acc = acc + jnp.dot(tril, lo, preferred_element_type=F32)
        acc = acc + carry
        c_ref[c0:c0 + n, :] = acc
        carry = acc[n - 1:n, :]


def _cumsum(lf):
    B, S, _ = lf.shape
    spec = pl.BlockSpec((None, S, LANES), lambda b: (b, 0, 0))
    return pl.pallas_call(
        functools.partial(_cumsum_kernel, S=S),
        grid=(B,),
        in_specs=[spec],
        out_specs=spec,
        out_shape=jax.ShapeDtypeStruct((B, S, LANES), F32),
        compiler_params=_params(),
        name="cumsum_logf",
    )(lf)


def _attn_prompt_kernel(q_ref, k_ref, v_ref, c_ref, ct_ref, o_ref, *, S):
    nq = S // ATT_TQ
    for h0 in range(0, N_HEADS, ATT_HEAD_GROUP):
        heads = range(h0, h0 + ATT_HEAD_GROUP)

        def q_body(qi, carry, heads=heads):
            q0 = pl.multiple_of(qi * ATT_TQ, ATT_TQ)
            hsl = [slice(h * HEAD_DIM, (h + 1) * HEAD_DIM) for h in heads]
            qs = [q_ref[pl.ds(q0, ATT_TQ), hs] for hs in hsl]
            cqs = [c_ref[pl.ds(q0, ATT_TQ), h:h + 1] for h in heads]

            def kv_step(kj, sts, diagonal):
                k0 = pl.multiple_of(kj * ATT_TK, ATT_TK)
                out = []
                for h, hs, q, cq, (m, l, acc) in zip(heads, hsl, qs, cqs, sts):
                    k = k_ref[pl.ds(k0, ATT_TK), hs]
                    v = v_ref[pl.ds(k0, ATT_TK), hs]
                    ck = ct_ref[h, pl.ds(kj, 1), :]
                    s = lax.dot_general(q, k, (((1,), (1,)), ((), ())),
                                        preferred_element_type=F32) * ATT_SCALE
                    s = s + cq - ck
                    if diagonal:
                        row = lax.broadcasted_iota(I32, (ATT_TQ, ATT_TK), 0)
                        col = lax.broadcasted_iota(I32, (ATT_TQ, ATT_TK), 1)
                        s = jnp.where(row >= col, s, -jnp.inf)
                    m_new = jnp.maximum(m, jnp.max(s, axis=1, keepdims=True))
                    alpha = jnp.exp(m - m_new)
                    p = jnp.exp(s - m_new)
                    l = alpha * l + jnp.sum(p, axis=1, keepdims=True)
                    acc = alpha * acc + jnp.dot(p.astype(BF16), v, preferred_element_type=F32)
                    out.append((m_new, l, acc))
                return tuple(out)

            init = tuple((jnp.full((ATT_TQ, 1), -jnp.inf, F32), jnp.zeros((ATT_TQ, 1), F32),
                          jnp.zeros((ATT_TQ, HEAD_DIM), F32)) for _ in heads)
            sts = lax.fori_loop(0, qi, lambda kj, sts: kv_step(kj, sts, False), init)
            sts = kv_step(qi, sts, True)
            for hs, (m, l, acc) in zip(hsl, sts):
                o_ref[pl.ds(q0, ATT_TQ), hs] = acc / l
            return carry

        lax.fori_loop(0, nq, q_body, 0)


def _attn_prompt(qb, kb, vb, c, ct, B, S):
    T = B * S
    seq = pl.BlockSpec((S, ATT_W), lambda b: (b, 0))
    return pl.pallas_call(
        functools.partial(_attn_prompt_kernel, S=S),
        grid=(B,),
        in_specs=[seq, seq, seq,
                  pl.BlockSpec((None, S, LANES), lambda b: (b, 0, 0)),
                  pl.BlockSpec((None, N_HEADS, S // ATT_TK, ATT_TK), lambda b: (b, 0, 0, 0))],
        out_specs=seq,
        out_shape=jax.ShapeDtypeStruct((T, ATT_W), F32),
        compiler_params=_params(),
        name="fox_prompt",
    )(qb, kb, vb, c, ct)


def _attn_sample_kernel(q_ref, kn_ref, vn_ref, ck_ref, cv_ref, c_ref, ct_ref, o_ref, *, P, Tn):
    r = lax.broadcasted_iota(I32, (Tn, Tn), 0)
    c = lax.broadcasted_iota(I32, (Tn, Tn), 1)
    nt = (((1,), (1,)), ((), ()))
    for h in range(N_HEADS):
        hs = slice(h * HEAD_DIM, (h + 1) * HEAD_DIM)
        q = q_ref[:, hs]
        kc = ck_ref[pl.ds(h, P, stride=N_HEADS), :].astype(BF16)
        vc = cv_ref[pl.ds(h, P, stride=N_HEADS), :].astype(BF16)
        cq = c_ref[P:P + Tn, h:h + 1]
        ckr = ct_ref[h:h + 1, :]
        s_c = lax.dot_general(q, kc, nt, preferred_element_type=F32) * ATT_SCALE
        s_c = s_c + cq - ckr[:, :P]
        s_n = lax.dot_general(q, kn_ref[:, hs], nt, preferred_element_type=F32) * ATT_SCALE
        s_n = jnp.where(r >= c, s_n + cq - ckr[:, P:P + Tn], -jnp.inf)
        m = jnp.maximum(jnp.max(s_c, axis=1, keepdims=True), jnp.max(s_n, axis=1, keepdims=True))
        p_c = jnp.exp(s_c - m)
        p_n = jnp.exp(s_n - m)
        l = jnp.sum(p_c, axis=1, keepdims=True) + jnp.sum(p_n, axis=1, keepdims=True)
        o = jnp.dot(p_c.astype(BF16), vc, preferred_element_type=F32)
        o = o + jnp.dot(p_n.astype(BF16), vn_ref[:, hs], preferred_element_type=F32)
        o_ref[:, hs] = o / l


def _attn_sample(qb, kb, vb, cache_k, cache_v, layer, c, ct, row0, Bd, P, Tn):
    blk0 = row0 // Tn
    new = pl.BlockSpec((Tn, ATT_W), lambda b: (blk0 + b, 0))
    cache = pl.BlockSpec((None, None, P * N_HEADS, HEAD_DIM), lambda b: (layer, b, 0, 0))
    return pl.pallas_call(
        functools.partial(_attn_sample_kernel, P=P, Tn=Tn),
        grid=(Bd,),
        in_specs=[new, new, new, cache, cache,
                  pl.BlockSpec((None, P + Tn, LANES), lambda b: (b, 0, 0)),
                  pl.BlockSpec((None, N_HEADS, P + Tn), lambda b: (b, 0, 0))],
        out_specs=pl.BlockSpec((Tn, ATT_W), lambda b: (b, 0)),
        out_shape=jax.ShapeDtypeStruct((Bd * Tn, ATT_W), F32),
        compiler_params=_params(),
        name="fox_sample",
    )(qb, kb, vb, cache_k, cache_v, c, ct)


def _lru_kernel(xr_ref, xg_ref, cs_ref, h0_ref, cw_ref, cb_ref, wr_ref, br_ref, wi_ref, bi_ref,
                lam_ref, o_ref, cso_ref, hl_ref, xp_ref, a_ref, u_ref, hs_ref, h_ref, *, ts):
    s = pl.program_id(1)
    pad = SUBLANES
    nprev = CONV_W - 1

    @pl.when(s == 0)
    def _():
        xp_ref[pad - nprev:pad, :] = cs_ref[...]
        h_ref[...] = h0_ref[...]

    x = xr_ref[...]
    xp_ref[pad:pad + ts, :] = x
    xc = cb_ref[...] + cw_ref[nprev:nprev + 1, :] * x
    for i in range(nprev):
        xc = xc + cw_ref[i:i + 1, :] * xp_ref[pad - nprev + i:pad - nprev + i + ts, :]
    tail = xp_ref[pad + ts - nprev:pad + ts, :]
    xp_ref[pad - nprev:pad, :] = tail
    cso_ref[...] = tail

    xcb = xc.astype(BF16)

    def block_diag(w_ref, b_ref):
        parts = [jnp.dot(xcb[:, n * LRU_BW:(n + 1) * LRU_BW], w_ref[n], preferred_element_type=F32)
                 for n in range(LRU_BLOCKS)]
        return jnp.concatenate(parts, axis=1) + b_ref[...]

    r = jax.nn.sigmoid(block_diag(wr_ref, br_ref))
    g = jax.nn.sigmoid(block_diag(wi_ref, bi_ref))
    nl = -lam_ref[...]
    softplus = jnp.maximum(nl, 0.0) + jnp.log1p(jnp.exp(-jnp.abs(nl)))
    log_a = (-LRU_C) * r * softplus
    a = jnp.exp(log_a)
    u = jnp.sqrt(1.0 - a * a) * g * xc

    sub = lax.broadcasted_iota(I32, (ts, LRU_W), 0) % SUBLANES
    d = 1
    while d < SUBLANES:
        keep = sub >= d
        u = jnp.where(keep, u + a * pltpu.roll(u, d, axis=0), u)
        a = jnp.where(keep, a * pltpu.roll(a, d, axis=0), a)
        d *= 2
    a_ref[...] = a
    u_ref[...] = u

    def group(gi, hprev):
        r0 = pl.multiple_of(gi * SUBLANES, SUBLANES)
        hh = u_ref[pl.ds(r0, SUBLANES), :] + a_ref[pl.ds(r0, SUBLANES), :] * hprev
        hs_ref[pl.ds(r0, SUBLANES), :] = hh
        return hh[SUBLANES - 1:SUBLANES, :]

    hlast = lax.fori_loop(0, ts // SUBLANES, group, h_ref[...])
    h_ref[...] = hlast
    hl_ref[...] = hlast
    o_ref[...] = hs_ref[...] * _gelu_tanh(xg_ref[...])


def _lru(xr, xg, conv_state, h0, lw, row0, nb, S, ts):
    ns = S // ts
    blk0 = row0 // ts
    rows = pl.BlockSpec((ts, LRU_W), lambda b, s: (blk0 + b * ns + s, 0))
    vec = lambda n: _resident((n, LRU_W))
    per_seq = lambda n: pl.BlockSpec((None, n, LRU_W), lambda b, s: (b, 0, 0))
    in_specs = [rows, rows, per_seq(CONV_W - 1), per_seq(1), vec(CONV_W), vec(1),
                _resident((LRU_BLOCKS, LRU_BW, LRU_BW)), vec(1),
                _resident((LRU_BLOCKS, LRU_BW, LRU_BW)), vec(1), vec(1)]
    args = [xr, xg, conv_state, h0, lw["conv_w"], lw["conv_b"], lw["w_r"], lw["b_r"],
            lw["w_i"], lw["b_i"], lw["lam"]]
    return pl.pallas_call(
        functools.partial(_lru_kernel, ts=ts),
        grid=(nb, ns),
        in_specs=in_specs,
        out_specs=[pl.BlockSpec((ts, LRU_W), lambda b, s: (b * ns + s, 0)),
                   per_seq(CONV_W - 1), per_seq(1)],
        out_shape=[jax.ShapeDtypeStruct((nb * S, LRU_W), F32),
                   jax.ShapeDtypeStruct((nb, CONV_W - 1, LRU_W), F32),
                   jax.ShapeDtypeStruct((nb, 1, LRU_W), F32)],
        scratch_shapes=[pltpu.VMEM((SUBLANES + ts, LRU_W), F32),
                        pltpu.VMEM((ts, LRU_W), F32), pltpu.VMEM((ts, LRU_W), F32),
                        pltpu.VMEM((ts, LRU_W), F32), pltpu.VMEM((1, LRU_W), F32)],
        compiler_params=_params(2),
        name="rg_lru",
    )(*args)


def _outproj_kernel(oap_ref, oas_ref, olp_ref, ols_ref, x_ref, w_ref, ga_ref, gl_ref, lg_ref,
                    lb_ref, wrh_ref, wrl_ref, rb_ref, x1_ref, x1s_ref, idx_ref, gate_ref,
                    *, alpha, n_prompt_blocks):
    is_prompt = pl.program_id(0) < n_prompt_blocks
    oa = jnp.where(is_prompt, oap_ref[...], oas_ref[...])
    ol = jnp.where(is_prompt, olp_ref[...], ols_ref[...])
    na = _rms_norm(oa, ga_ref[...]).astype(BF16)
    nl = _rms_norm(ol, gl_ref[...]).astype(BF16)
    m = jnp.dot(na, w_ref[0:ATT_W, :], preferred_element_type=F32)
    m = m + jnp.dot(nl, w_ref[ATT_W:D_MODEL, :], preferred_element_type=F32)
    x1 = _layer_norm(alpha * x_ref[...] + m, lg_ref[...], lb_ref[...])
    x1_ref[...] = x1
    _matrix_to_rows(x1s_ref, x1, x1.shape[0])

    xh = x1.astype(BF16)
    xl = (x1 - xh.astype(F32)).astype(BF16)
    logits = jnp.dot(xh, wrh_ref[...], preferred_element_type=F32)
    logits = logits + jnp.dot(xl, wrh_ref[...], preferred_element_type=F32)
    logits = logits + jnp.dot(xh, wrl_ref[...], preferred_element_type=F32)
    scores = jax.nn.sigmoid(logits)
    tm = scores.shape[0]
    lane = lax.broadcasted_iota(I32, (tm, LANES), 1)
    lane_f = lane.astype(F32)
    sel = jnp.where(lane < N_EXPERTS, scores + rb_ref[...], -jnp.inf)
    idx_out = jnp.zeros((tm, LANES), F32)
    gate_out = jnp.zeros((tm, LANES), F32)
    gsum = jnp.zeros((tm, 1), F32)
    for k in range(TOP_K):
        mx = jnp.max(sel, axis=1, keepdims=True)
        ik = jnp.min(jnp.where(sel == mx, lane_f, float(LANES)), axis=1, keepdims=True)
        hit = lane_f == ik
        gk = jnp.sum(jnp.where(hit, scores, 0.0), axis=1, keepdims=True)
        sel = jnp.where(hit, -jnp.inf, sel)
        idx_out = jnp.where(lane == k, ik, idx_out)
        gate_out = jnp.where(lane == k, gk, gate_out)
        gsum = gsum + gk
    idx_ref[...] = idx_out.astype(I32)
    gate_ref[...] = gate_out / gsum * ROUTE_SCALE


def _outproj(oa_p, oa_s, ol_p, ol_s, x, lw, alpha):
    T = x.shape[0]
    n_p = oa_p.shape[0] // TM
    row = lambda w: pl.BlockSpec((TM, w), lambda i: (i, 0))
    prompt = lambda w: pl.BlockSpec((TM, w), lambda i: (jnp.minimum(i, n_p - 1), 0))
    sample = lambda w: pl.BlockSpec((TM, w), lambda i: (jnp.maximum(i - n_p, 0), 0))
    vec = lambda n: _resident((1, n))
    return pl.pallas_call(
        functools.partial(_outproj_kernel, alpha=alpha, n_prompt_blocks=n_p),
        grid=(T // TM,),
        in_specs=[prompt(ATT_W), sample(ATT_W), prompt(LRU_W), sample(LRU_W), row(D_MODEL),
                  _resident((D_MODEL, D_MODEL)),
                  vec(ATT_W), vec(LRU_W), vec(D_MODEL), vec(D_MODEL),
                  _resident((D_MODEL, LANES)), _resident((D_MODEL, LANES)), vec(LANES)],
        out_specs=[row(D_MODEL), pl.BlockSpec((TM * SLAB_ROWS, LANES), lambda i: (i, 0)),
                   row(LANES), row(LANES)],
        out_shape=[jax.ShapeDtypeStruct((T, D_MODEL), F32),
                   jax.ShapeDtypeStruct((T * SLAB_ROWS, LANES), U32),
                   jax.ShapeDtypeStruct((T, LANES), I32),
                   jax.ShapeDtypeStruct((T, LANES), F32)],
        compiler_params=_params(),
        name="outproj_ln_router",
    )(oa_p, oa_s, ol_p, ol_s, x, lw["w_out"], lw["g_att"], lw["g_lru"], lw["ln1_g"], lw["ln1_b"],
      lw["wr_hi"], lw["wr_lo"], lw["b_router"])


def _rank_kernel(idx_ref, dest_ref, cnt_ref, carry_ref, start_ref):
    phase = pl.program_id(0)
    i = pl.program_id(1)

    @pl.when((phase == 0) & (i == 0))
    def _():
        carry_ref[...] = jnp.zeros_like(carry_ref)

    tm = idx_ref.shape[0]
    idx = idx_ref[...]
    lane = lax.broadcasted_iota(I32, (tm, LANES), 1)
    onehot = jnp.zeros((tm, LANES), F32)
    for k in range(TOP_K):
        onehot = onehot + (lane == idx[:, k:k + 1]).astype(F32)
    colsum = jnp.sum(onehot, axis=0, keepdims=True)

    @pl.when(phase == 0)
    def _():
        carry_ref[...] = carry_ref[...] + colsum
        dest_ref[...] = jnp.zeros_like(dest_ref)

    @pl.when((phase == 1) & (i == 0))
    def _():
        counts = carry_ref[...]
        cnt_ref[...] = counts.astype(I32)
        padded = jnp.floor((counts + (MOE_BLK - 1)) * (1.0 / MOE_BLK)) * MOE_BLK
        r = lax.broadcasted_iota(I32, (LANES, LANES), 0)
        c = lax.broadcasted_iota(I32, (LANES, LANES), 1)
        before = (r < c).astype(BF16)
        start = jnp.zeros((SUBLANES, LANES), F32)
        for term in _split3(jnp.broadcast_to(padded, (SUBLANES, LANES))):
            start = start + jnp.dot(term, before, preferred_element_type=F32)
        start_ref[...] = start[0:1, :] + float(TAB_CHUNK)
        carry_ref[...] = jnp.zeros_like(carry_ref)

    @pl.when(phase == 1)
    def _():
        r = lax.broadcasted_iota(I32, (tm, tm), 0)
        c = lax.broadcasted_iota(I32, (tm, tm), 1)
        below = (r > c).astype(BF16)
        entry = jnp.dot(below, onehot.astype(BF16), preferred_element_type=F32)
        entry = entry + carry_ref[...] + start_ref[...]
        out = jnp.zeros((tm, LANES), F32)
        for k in range(TOP_K):
            ek = jnp.sum(jnp.where(lane == idx[:, k:k + 1], entry, 0.0), axis=1, keepdims=True)
            out = jnp.where(lane == k, ek, out)
        dest_ref[...] = out.astype(I32)
        carry_ref[...] = carry_ref[...] + colsum


def _rank(idx):
    T = idx.shape[0]
    row = pl.BlockSpec((TM, LANES), lambda p, i: (i, 0))
    return pl.pallas_call(
        _rank_kernel,
        grid=(2, T // TM),
        in_specs=[row],
        out_specs=[pl.BlockSpec((TM, LANES), lambda p, i: (i * p, 0)),
                   pl.BlockSpec((1, LANES), lambda p, i: (0, 0))],
        out_shape=[jax.ShapeDtypeStruct((T, LANES), I32), jax.ShapeDtypeStruct((1, LANES), I32)],
        scratch_shapes=[pltpu.VMEM((1, LANES), F32), pltpu.VMEM((1, LANES), F32)],
        compiler_params=_params(2),
        name="route_rank",
    )(idx)


def _slot_table_kernel(dest_hbm, init_hbm, tab_hbm, tab, dchunk, sem):
    i = pl.program_id(0)
    n = ROUTE_CHUNK

    @pl.when(i == 0)
    def _():
        cp = pltpu.make_async_copy(init_hbm, tab, sem.at[0])
        cp.start()
        cp.wait()

    off = pl.multiple_of(i * n, n)
    cp = pltpu.make_async_copy(dest_hbm.at[pl.ds(off, n)], dchunk, sem.at[1])
    cp.start()
    cp.wait()

    def group(j8, carry):
        for u in range(SUBLANES):
            j = j8 * SUBLANES + u
            tab[dchunk[j]] = off + j
        return carry

    lax.fori_loop(0, n // SUBLANES, group, 0)

    @pl.when(i == pl.num_programs(0) - 1)
    def _():
        cp = pltpu.make_async_copy(tab, tab_hbm, sem.at[0])
        cp.start()
        cp.wait()


def _slot_table(dest_flat, n_slots):
    n = dest_flat.shape[0]
    n_tokens = n // TOP_K
    n_tab = n_slots + TAB_CHUNK
    sentinel = ((n_tokens + jnp.arange(n_tab, dtype=I32) % MOE_BLK) << TOP_K_SHIFT) | (TOP_K - 1)
    any_spec = pl.BlockSpec(memory_space=pl.ANY)
    return pl.pallas_call(
        _slot_table_kernel,
        grid=(n // ROUTE_CHUNK,),
        in_specs=[any_spec, any_spec],
        out_specs=any_spec,
        out_shape=jax.ShapeDtypeStruct((n_tab,), I32),
        scratch_shapes=[pltpu.SMEM((n_tab,), I32), pltpu.SMEM((ROUTE_CHUNK,), I32),
                        pltpu.SemaphoreType.DMA((2,))],
        compiler_params=_params(),
        name="moe_slot_table",
    )(dest_flat, sentinel)


def _rows_to_matrix(ref, n):
    cols = []
    for r in range(SLAB_ROWS):
        w = ref[pl.ds(r, n, stride=SLAB_ROWS), :]
        cols.append(pltpu.bitcast(w << 16, F32))
        cols.append(pltpu.bitcast(w & jnp.uint32(0xFFFF0000), F32))
    return jnp.concatenate(cols, axis=1)


def _matrix_to_rows(ref, y, n):
    for r in range(SLAB_ROWS):
        lo = y[:, (2 * r) * LANES:(2 * r + 1) * LANES].astype(BF16).astype(F32)
        hi = y[:, (2 * r + 1) * LANES:(2 * r + 2) * LANES].astype(BF16).astype(F32)
        ref[pl.ds(r, n, stride=SLAB_ROWS), :] = (
            pltpu.bitcast(hi, U32) | (pltpu.bitcast(lo, U32) >> 16))


def _experts_kernel(first_ref, nu_ref, x1s_hbm, tab_hbm, wg_ref, wu_ref, wd_ref, yk_hbm,
                    wg_s, wu_s, wd_s, xbuf0, xbuf1, ybuf0, ybuf1, stk, gsem, ssem, tsem,
                    *, n_tokens):
    e = pl.program_id(0)
    nu = nu_ref[0]
    blk_rows = MOE_BLK * SLAB_ROWS
    xbufs = (xbuf0, xbuf1)
    ybufs = (ybuf0, ybuf1)

    def table_copy(g):
        gt = g + TAB_RING
        q = gt % TAB_RING
        src = tab_hbm.at[pl.ds(pl.multiple_of((gt // TAB_RING) * TAB_CHUNK, TAB_CHUNK), TAB_CHUNK)]
        dst = stk.at[pl.ds(pl.multiple_of(q * TAB_CHUNK, TAB_CHUNK), TAB_CHUNK)]
        return pltpu.make_async_copy(src, dst, tsem.at[q])

    def table_base(g):
        return ((g + TAB_RING) % TAB_RING) * (TAB_CHUNK + MOE_BLK)

    def gather_row(base, r, xb, sem):
        tok = jnp.minimum(stk[base + r] >> TOP_K_SHIFT, n_tokens - 1)
        src = x1s_hbm.at[pl.ds(pl.multiple_of(tok * SLAB_ROWS, SLAB_ROWS), SLAB_ROWS)]
        pltpu.make_async_copy(src, xb.at[pl.ds(r * SLAB_ROWS, SLAB_ROWS)], sem).start()

    def scatter_row(base, r, yb, sem):
        v = stk[base + r]
        row = (v & (TOP_K - 1)) * n_tokens + (v >> TOP_K_SHIFT)
        dst = yk_hbm.at[pl.ds(pl.multiple_of(row * SLAB_ROWS, SLAB_ROWS), SLAB_ROWS)]
        pltpu.make_async_copy(yb.at[pl.ds(r * SLAB_ROWS, SLAB_ROWS)], dst, sem).start()

    def wait_gather(xb, sem):
        pltpu.make_async_copy(x1s_hbm.at[pl.ds(0, blk_rows)], xb, sem).wait()

    def wait_scatter(yb, sem):
        pltpu.make_async_copy(yb, yk_hbm.at[pl.ds(0, blk_rows)], sem).wait()

    @pl.when(e == 0)
    def _():
        ybuf1[...] = jnp.zeros_like(ybuf1)
        table_copy(-1).start()
        table_copy(0).start()

        @pl.when(nu > 1)
        def _():
            table_copy(1).start()

        table_copy(-1).wait()
        table_copy(0).wait()
        base = table_base(0)
        for r in range(MOE_BLK):
            gather_row(base, r, xbuf0, gsem.at[0])

    g0 = first_ref[e]
    g1 = first_ref[e + 1]

    @pl.when(g1 > g0)
    def _():
        wg_s[...] = wg_ref[...].astype(BF16)
        wu_s[...] = wu_ref[...].astype(BF16)
        wd_s[...] = wd_ref[...].astype(BF16)

    def step(g, p):
        xb_cur, xb_nxt = xbufs[p], xbufs[1 - p]
        yb_cur, yb_prv = ybufs[p], ybufs[1 - p]

        @pl.when(g + 2 < nu)
        def _():
            table_copy(g + 2).start()

        @pl.when(g + 1 < nu)
        def _():
            table_copy(g + 1).wait()

        wait_gather(xb_cur, gsem.at[p])

        @pl.when(g >= 1)
        def _():
            wait_scatter(yb_cur, ssem.at[p])

        base_next = table_base(jnp.minimum(g + 1, nu - 1))
        base_prev = table_base(g - 1)
        for r in range(MOE_BLK):
            gather_row(base_next, r, xb_nxt, gsem.at[1 - p])
        for r in range(MOE_BLK):
            scatter_row(base_prev, r, yb_prv, ssem.at[1 - p])
        x = _rows_to_matrix(xb_cur, MOE_BLK).astype(BF16)
        gt = jnp.dot(x, wg_s[...], preferred_element_type=F32)
        up = jnp.dot(x, wu_s[...], preferred_element_type=F32)
        h = (_silu(gt) * up).astype(BF16)
        y = jnp.dot(h, wd_s[...], preferred_element_type=F32)
        _matrix_to_rows(yb_cur, y, MOE_BLK)

        @pl.when(g == nu - 1)
        def _():
            wait_scatter(yb_prv, ssem.at[1 - p])
            base = table_base(g)

            def group(i, carry):
                for u in range(SUBLANES):
                    r = i * SUBLANES + u
                    v = stk[base + r]
                    row = (v & (TOP_K - 1)) * n_tokens + (v >> TOP_K_SHIFT)
                    src = yb_cur.at[pl.ds(pl.multiple_of(r * SLAB_ROWS, SLAB_ROWS), SLAB_ROWS)]
                    dst = yk_hbm.at[pl.ds(pl.multiple_of(row * SLAB_ROWS, SLAB_ROWS), SLAB_ROWS)]
                    pltpu.make_async_copy(src, dst, ssem.at[p]).start()
                return carry

            lax.fori_loop(0, MOE_BLK // SUBLANES, group, 0)
            wait_scatter(yb_cur, ssem.at[p])
            wait_gather(xb_nxt, gsem.at[1 - p])

    def block(g, carry):
        @pl.when(g % 2 == 0)
        def _():
            step(g, 0)

        @pl.when(g % 2 == 1)
        def _():
            step(g, 1)

        return carry

    lax.fori_loop(g0, g1, block, 0)


def _experts(first_blk, n_used, x1s, slot_tab, we_gate, we_up, we_down, layer):
    n_tokens = x1s.shape[0] // SLAB_ROWS
    blk_rows = MOE_BLK * SLAB_ROWS
    any_spec = pl.BlockSpec(memory_space=pl.ANY)
    w_in_spec = pl.BlockSpec((None, None, D_MODEL, D_EXPERT), lambda e, *_: (layer, e, 0, 0))
    grid_spec = pltpu.PrefetchScalarGridSpec(
        num_scalar_prefetch=2,
        grid=(N_EXPERTS,),
        in_specs=[any_spec, any_spec, w_in_spec, w_in_spec,
                  pl.BlockSpec((None, None, D_EXPERT, D_MODEL), lambda e, *_: (layer, e, 0, 0))],
        out_specs=any_spec,
        scratch_shapes=[pltpu.VMEM((D_MODEL, D_EXPERT), BF16), pltpu.VMEM((D_MODEL, D_EXPERT), BF16),
                        pltpu.VMEM((D_EXPERT, D_MODEL), BF16),
                        pltpu.VMEM((blk_rows, LANES), U32), pltpu.VMEM((blk_rows, LANES), U32),
                        pltpu.VMEM((blk_rows, LANES), U32), pltpu.VMEM((blk_rows, LANES), U32),
                        pltpu.SMEM((TAB_RING * TAB_CHUNK,), I32),
                        pltpu.SemaphoreType.DMA((2,)), pltpu.SemaphoreType.DMA((2,)),
                        pltpu.SemaphoreType.DMA((TAB_RING,))],
    )
    return pl.pallas_call(
        functools.partial(_experts_kernel, n_tokens=n_tokens),
        grid_spec=grid_spec,
        out_shape=jax.ShapeDtypeStruct(((TOP_K * n_tokens + MOE_BLK) * SLAB_ROWS, LANES), U32),
        compiler_params=_params(),
        name="moe_experts",
    )(first_blk, n_used, x1s, slot_tab, we_gate, we_up, we_down)


def _combine_kernel(*refs, alpha):
    yk_refs = refs[:TOP_K]
    gate_ref, x_ref, wsg_ref, wsu_ref, wsd_ref, lg_ref, lb_ref, o_ref = refs[TOP_K:]
    tm = x_ref.shape[0]
    x1 = x_ref[...]
    xb = x1.astype(BF16)
    hs = _silu(jnp.dot(xb, wsg_ref[...], preferred_element_type=F32))
    hs = (hs * jnp.dot(xb, wsu_ref[...], preferred_element_type=F32)).astype(BF16)
    acc = jnp.dot(hs, wsd_ref[...], preferred_element_type=F32)
    gates = gate_ref[...]
    for k in range(TOP_K):
        acc = acc + gates[:, k:k + 1] * _rows_to_matrix(yk_refs[k], tm)
    o_ref[...] = _layer_norm(alpha * x1 + acc, lg_ref[...], lb_ref[...])


def _combine(yk, gates, x1, lw, alpha):
    T = x1.shape[0]
    tm = TM_COMBINE
    row = lambda w: pl.BlockSpec((tm, w), lambda i: (i, 0))
    plane = lambda k: pl.BlockSpec((tm * SLAB_ROWS, LANES), lambda i: (k * (T // tm) + i, 0))
    return pl.pallas_call(
        functools.partial(_combine_kernel, alpha=alpha),
        grid=(T // tm,),
        in_specs=[plane(k) for k in range(TOP_K)] + [
                  row(LANES), row(D_MODEL),
                  _resident((D_MODEL, D_SHARED)), _resident((D_MODEL, D_SHARED)),
                  _resident((D_SHARED, D_MODEL)), _resident((1, D_MODEL)), _resident((1, D_MODEL))],
        out_specs=row(D_MODEL),
        out_shape=jax.ShapeDtypeStruct((T, D_MODEL), F32),
        compiler_params=_params(),
        name="moe_combine",
    )(*([yk] * TOP_K), gates, x1, lw["ws_gate"], lw["ws_up"], lw["ws_down"], lw["ln2_g"],
      lw["ln2_b"])


def _block_tables(counts):
    padded = (counts + MOE_BLK - 1) // MOE_BLK * MOE_BLK
    pad_end = jnp.cumsum(padded)
    pad_start = pad_end - padded
    first_blk = jnp.concatenate([pad_start, pad_end[-1:]]) // MOE_BLK
    n_used = pad_end[-1:] // MOE_BLK
    return first_blk.astype(I32), n_used.astype(I32)


def _moe(x1, x1s, idx, gates, lw, we_gate, we_up, we_down, layer, alpha):
    T = x1.shape[0]
    n_blocks = T * TOP_K // MOE_BLK + N_EXPERTS
    dest, cnt = _rank(idx)
    first_blk, n_used = _block_tables(cnt[0, :N_EXPERTS])
    slot_tab = _slot_table(dest[:, :TOP_K].reshape(-1), n_blocks * MOE_BLK)
    yk = _experts(first_blk, n_used, x1s, slot_tab, we_gate, we_up, we_down, layer)
    return _combine(yk, gates, x1, lw, alpha)


def _prep_layer(l, w_in, b_f, w_out, g_att, g_lru, conv_w, conv_b, w_r, b_r, w_i, b_i, lru_lambda,
                ln1_g, ln1_b, w_router, b_router, ws_gate, ws_up, ws_down, ln2_g, ln2_b):
    qkv_end = 3 * ATT_W
    wl = w_in[l]
    w_pad = jnp.concatenate(
        [wl[:, :qkv_end], wl[:, qkv_end + N_HEADS:],
         jnp.pad(wl[:, qkv_end:qkv_end + N_HEADS], ((0, 0), (0, LANES - N_HEADS)))],
        axis=1).astype(BF16)
    wr = jnp.pad(w_router[l], ((0, 0), (0, LANES - N_EXPERTS)))
    wr_hi = wr.astype(BF16)
    return dict(
        w_pad=w_pad,
        bf_pad=jnp.pad(b_f[l], (0, LANES - N_HEADS)).reshape(1, LANES),
        w_out=w_out[l].astype(BF16),
        g_att=g_att[l].reshape(1, ATT_W), g_lru=g_lru[l].reshape(1, LRU_W),
        conv_w=conv_w[l], conv_b=conv_b[l].reshape(1, LRU_W),
        w_r=w_r[l].astype(BF16), b_r=b_r[l].reshape(1, LRU_W),
        w_i=w_i[l].astype(BF16), b_i=b_i[l].reshape(1, LRU_W),
        lam=lru_lambda[l].reshape(1, LRU_W),
        ln1_g=ln1_g[l].reshape(1, D_MODEL), ln1_b=ln1_b[l].reshape(1, D_MODEL),
        wr_hi=wr_hi, wr_lo=(wr - wr_hi.astype(F32)).astype(BF16),
        b_router=jnp.pad(b_router[l], (0, LANES - N_EXPERTS)).reshape(1, LANES),
        ws_gate=ws_gate[l].astype(BF16), ws_up=ws_up[l].astype(BF16),
        ws_down=ws_down[l].astype(BF16),
        ln2_g=ln2_g[l].reshape(1, D_MODEL), ln2_b=ln2_b[l].reshape(1, D_MODEL),
    )


def kernel(x_prompt, x_sample, cache_k, cache_v, cache_logf, state_conv, state_lru, ln_in_g, ln_in_b, w_in, b_f, w_out, g_att, g_lru, conv_w, conv_b, w_r, b_r, w_i, b_i, lru_lambda, ln1_g, ln1_b, w_router, b_router, we_gate, we_up, we_down, ws_gate, ws_up, ws_down, ln2_g, ln2_b):
    B, S, _ = x_prompt.shape
    Bd, Tn, _ = x_sample.shape
    depth = w_in.shape[0]
    P = cache_k.shape[2]
    Tp = B * S
    Ts = Bd * Tn
    alpha = (2 * depth) ** 0.25

    x = jnp.concatenate([x_prompt.reshape(Tp, D_MODEL), x_sample.reshape(Ts, D_MODEL)], axis=0)
    ln_g = ln_in_g.reshape(1, D_MODEL)
    ln_b = ln_in_b.reshape(1, D_MODEL)
    zero_conv = jnp.zeros((B, CONV_W - 1, LRU_W), F32)
    zero_h = jnp.zeros((B, 1, LRU_W), F32)

    ks, vs, lfs, p_conv, p_lru, s_conv, s_lru = [], [], [], [], [], [], []
    for l in range(depth):
        lw = _prep_layer(l, w_in, b_f, w_out, g_att, g_lru, conv_w, conv_b, w_r, b_r, w_i, b_i,
                         lru_lambda, ln1_g, ln1_b, w_router, b_router, ws_gate, ws_up, ws_down,
                         ln2_g, ln2_b)
        outs = _inproj(x, ln_g, ln_b, lw["w_pad"], lw["bf_pad"], apply_ln=(l == 0))
        if l == 0:
            x, outs = outs[0], outs[1:]
        qb, k, v, kb, vb, xr, xg, lf = outs

        c_p = _cumsum(lf[:Tp].reshape(B, S, LANES))
        ct_p = c_p[:, :, :N_HEADS].transpose(0, 2, 1).reshape(B, N_HEADS, S // ATT_TK, ATT_TK)
        lf_cache = jnp.pad(cache_logf[l], ((0, 0), (0, 0), (0, LANES - N_HEADS)))
        c_s = _cumsum(jnp.concatenate([lf_cache, lf[Tp:].reshape(Bd, Tn, LANES)], axis=1))
        ct_s = c_s[:, :, :N_HEADS].transpose(0, 2, 1)

        oa_p = _attn_prompt(qb, kb, vb, c_p, ct_p, B, S)
        oa_s = _attn_sample(qb, kb, vb, cache_k.reshape(depth, Bd, P * N_HEADS, HEAD_DIM),
                            cache_v.reshape(depth, Bd, P * N_HEADS, HEAD_DIM), l, c_s, ct_s,
                            Tp, Bd, P, Tn)

        ol_p, pc, ph = _lru(xr, xg, zero_conv, zero_h, lw, 0, B, S, LRU_TS)
        ol_s, sc, sh = _lru(xr, xg, state_conv[l], state_lru[l].reshape(Bd, 1, LRU_W), lw,
                            Tp, Bd, Tn, Tn)

        x1, x1s, idx, gates = _outproj(oa_p, oa_s, ol_p, ol_s, x, lw, alpha)
        x = _moe(x1, x1s, idx, gates, lw, we_gate, we_up, we_down, l, alpha)

        ks.append(k)
        vs.append(v)
        lfs.append(lf[:, :N_HEADS])
        p_conv.append(pc)
        p_lru.append(ph.reshape(B, LRU_W))
        s_conv.append(sc)
        s_lru.append(sh.reshape(Bd, LRU_W))

    def split(ts, tail):
        p = jnp.stack([t[:Tp].reshape((B, S) + tail) for t in ts])
        s = jnp.stack([t[Tp:].reshape((Bd, Tn) + tail) for t in ts])
        return p, s

    p_k, s_k = split(ks, (N_HEADS, HEAD_DIM))
    p_v, s_v = split(vs, (N_HEADS, HEAD_DIM))
    p_logf, s_logf = split(lfs, (N_HEADS,))
    y_prompt = x[:Tp].reshape(B, S, D_MODEL)
    y_sample = x[Tp:].reshape(Bd, Tn, D_MODEL)
    return (y_prompt, y_sample, p_k, p_v, p_logf, jnp.stack(p_conv), jnp.stack(p_lru),
            s_k, s_v, s_logf, jnp.stack(s_conv), jnp.stack(s_lru))
```

```python
import functools

import jax
import jax.numpy as jnp
from jax import lax
from jax.experimental import pallas as pl
from jax.experimental.pallas import tpu as pltpu

F32 = jnp.float32
BF16 = jnp.bfloat16
I32 = jnp.int32
U32 = jnp.uint32

D_MODEL = 2048
N_HEADS = 8
HEAD_DIM = 128
ATT_W = N_HEADS * HEAD_DIM
LRU_W = D_MODEL - ATT_W
LRU_BLOCKS = 8
LRU_BW = LRU_W // LRU_BLOCKS
CONV_W = 4
LRU_C = 8.0
N_EXPERTS = 64
TOP_K = 8
TOP_K_SHIFT = TOP_K.bit_length() - 1
assert 1 << TOP_K_SHIFT == TOP_K
D_EXPERT = 512
D_SHARED = 512
ROUTE_SCALE = 2.5
ATT_SCALE = HEAD_DIM ** -0.5
LN_EPS = 1e-5
RMS_EPS = 1e-6

LANES = 128
SUBLANES = 8
SLAB_ROWS = D_MODEL // (2 * LANES)
VMEM_LIMIT = 56 * 1024 * 1024

TM = 256
MOE_BLK = 256
TM_COMBINE = 128
ROUTE_CHUNK = 2048
TAB_CHUNK = 1024
TAB_RING = TAB_CHUNK // MOE_BLK
ATT_TQ = 512
ATT_TK = ATT_TQ
ATT_HEAD_GROUP = 2
LRU_TS = 256
CUM_CHUNK = 128

C_Q, C_K, C_V, C_XR, C_XG, C_F, C_END = 0, 1024, 2048, 3072, 4096, 5120, 5248


def _params(n_axes=1):
    return pltpu.CompilerParams(dimension_semantics=("arbitrary",) * n_axes,
                                vmem_limit_bytes=VMEM_LIMIT)


def _resident(shape):
    nd = len(shape)
    return pl.BlockSpec(shape, lambda *_: (0,) * nd, pipeline_mode=pl.Buffered(1))


def _layer_norm(x, g, b):
    mu = jnp.mean(x, axis=-1, keepdims=True)
    xc = x - mu
    var = jnp.mean(xc * xc, axis=-1, keepdims=True)
    return xc * lax.rsqrt(var + LN_EPS) * g + b


def _rms_norm(x, g):
    return x * lax.rsqrt(jnp.mean(x * x, axis=-1, keepdims=True) + RMS_EPS) * g


def _silu(x):
    return x * jax.nn.sigmoid(x)


def _gelu_tanh(x):
    return 0.5 * x * (1.0 + jnp.tanh(0.7978845608028654 * (x + 0.044715 * (x * x * x))))


def _split3(x):
    hi = x.astype(BF16)
    r1 = x - hi.astype(F32)
    mid = r1.astype(BF16)
    lo = (r1 - mid.astype(F32)).astype(BF16)
    return hi, mid, lo


def _inproj_kernel(apply_ln, x_ref, g_ref, b_ref, w_ref, bf_ref, *outs):
    if apply_ln:
        x0_ref, outs = outs[0], outs[1:]
    q_ref, k_ref, v_ref, kb_ref, vb_ref, xr_ref, xg_ref, lf_ref = outs
    x = x_ref[...]
    if apply_ln:
        x = _layer_norm(x, g_ref[...], b_ref[...])
        x0_ref[...] = x
    h = x.astype(BF16)

    def mm(c0, c1):
        return jnp.dot(h, w_ref[:, c0:c1], preferred_element_type=F32)

    q_ref[...] = mm(C_Q, C_K).astype(BF16)
    k = mm(C_K, C_V)
    k_ref[...] = k
    kb_ref[...] = k.astype(BF16)
    v = mm(C_V, C_XR)
    v_ref[...] = v
    vb_ref[...] = v.astype(BF16)
    xr_ref[...] = mm(C_XR, C_XG)
    xg_ref[...] = mm(C_XG, C_F)
    f = mm(C_F, C_END) + bf_ref[...]
    lf_ref[...] = jnp.minimum(f, 0.0) - jnp.log1p(jnp.exp(-jnp.abs(f)))


def _inproj(x, ln_g, ln_b, w_pad, bf_pad, apply_ln):
    T = x.shape[0]
    row = lambda w: pl.BlockSpec((TM, w), lambda i: (i, 0))
    out_shape = [
        jax.ShapeDtypeStruct((T, ATT_W), BF16),
        jax.ShapeDtypeStruct((T, ATT_W), F32),
        jax.ShapeDtypeStruct((T, ATT_W), F32),
        jax.ShapeDtypeStruct((T, ATT_W), BF16),
        jax.ShapeDtypeStruct((T, ATT_W), BF16),
        jax.ShapeDtypeStruct((T, LRU_W), F32),
        jax.ShapeDtypeStruct((T, LRU_W), F32),
        jax.ShapeDtypeStruct((T, LANES), F32),
    ]
    out_specs = [row(ATT_W)] * 5 + [row(LRU_W)] * 2 + [row(LANES)]
    if apply_ln:
        out_shape = [jax.ShapeDtypeStruct((T, D_MODEL), F32)] + out_shape
        out_specs = [row(D_MODEL)] + out_specs
    return pl.pallas_call(
        functools.partial(_inproj_kernel, apply_ln),
        grid=(T // TM,),
        in_specs=[row(D_MODEL), _resident((1, D_MODEL)), _resident((1, D_MODEL)),
                  _resident((D_MODEL, C_END)), _resident((1, LANES))],
        out_specs=out_specs,
        out_shape=out_shape,
        compiler_params=_params(),
        name="inproj",
    )(x, ln_g, ln_b, w_pad, bf_pad)


def _cumsum_kernel(lf_ref, c_ref, *, S):
    carry = jnp.zeros((1, LANES), F32)
    for c0 in range(0, S, CUM_CHUNK):
        n = min(CUM_CHUNK, S - c0)
        r = lax.broadcasted_iota(I32, (n, n), 0)
        c = lax.broadcasted_iota(I32, (n, n), 1)
        tril = (r >= c).astype(BF16)
        hi, mid, lo = _split3(lf_ref[c0:c0 + n, :])
        acc = jnp.dot(tril, hi, preferred_element_type=F32)
        acc = acc + jnp.dot(tril, mid, preferred_element_type=F32)
        acc = acc + jnp.dot(tril, lo, preferred_element_type=F32)
        acc = acc + carry
        c_ref[c0:c0 + n, :] = acc
        carry = acc[n - 1:n, :]


def _cumsum(lf):
    B, S, _ = lf.shape
    spec = pl.BlockSpec((None, S, LANES), lambda b: (b, 0, 0))
    return pl.pallas_call(
        functools.partial(_cumsum_kernel, S=S),
        grid=(B,),
        in_specs=[spec],
        out_specs=spec,
        out_shape=jax.ShapeDtypeStruct((B, S, LANES), F32),
        compiler_params=_params(),
        name="cumsum_logf",
    )(lf)


def _attn_prompt_kernel(q_ref, k_ref, v_ref, c_ref, ct_ref, o_ref, *, S):
    nq = S // ATT_TQ
    for h0 in range(0, N_HEADS, ATT_HEAD_GROUP):
        heads = range(h0, h0 + ATT_HEAD_GROUP)

        def q_body(qi, carry, heads=heads):
            q0 = pl.multiple_of(qi * ATT_TQ, ATT_TQ)
            hsl = [slice(h * HEAD_DIM, (h + 1) * HEAD_DIM) for h in heads]
            qs = [q_ref[pl.ds(q0, ATT_TQ), hs] for hs in hsl]
            cqs = [c_ref[pl.ds(q0, ATT_TQ), h:h + 1] for h in heads]

            def kv_step(kj, sts, diagonal):
                k0 = pl.multiple_of(kj * ATT_TK, ATT_TK)
                out = []
                for h, hs, q, cq, (m, l, acc) in zip(heads, hsl, qs, cqs, sts):
                    k = k_ref[pl.ds(k0, ATT_TK), hs]
                    v = v_ref[pl.ds(k0, ATT_TK), hs]
                    ck = ct_ref[h, pl.ds(kj, 1), :]
                    s = lax.dot_general(q, k, (((1,), (1,)), ((), ())),
                                        preferred_element_type=F32) * ATT_SCALE
                    s = s + cq - ck
                    if diagonal:
                        row = lax.broadcasted_iota(I32, (ATT_TQ, ATT_TK), 0)
                        col = lax.broadcasted_iota(I32, (ATT_TQ, ATT_TK), 1)
                        s = jnp.where(row >= col, s, -jnp.inf)
                    m_new = jnp.maximum(m, jnp.max(s, axis=1, keepdims=True))
                    alpha = jnp.exp(m - m_new)
                    p = jnp.exp(s - m_new)
                    l = alpha * l + jnp.sum(p, axis=1, keepdims=True)
                    acc = alpha * acc + jnp.dot(p.astype(BF16), v, preferred_element_type=F32)
                    out.append((m_new, l, acc))
                return tuple(out)

            init = tuple((jnp.full((ATT_TQ, 1), -jnp.inf, F32), jnp.zeros((ATT_TQ, 1), F32),
                          jnp.zeros((ATT_TQ, HEAD_DIM), F32)) for _ in heads)
            sts = lax.fori_loop(0, qi, lambda kj, sts: kv_step(kj, sts, False), init)
            sts = kv_step(qi, sts, True)
            for hs, (m, l, acc) in zip(hsl, sts):
                o_ref[pl.ds(q0, ATT_TQ), hs] = acc / l
            return carry

        lax.fori_loop(0, nq, q_body, 0)


def _attn_prompt(qb, kb, vb, c, ct, B, S):
    T = B * S
    seq = pl.BlockSpec((S, ATT_W), lambda b: (b, 0))
    return pl.pallas_call(
        functools.partial(_attn_prompt_kernel, S=S),
        grid=(B,),
        in_specs=[seq, seq, seq,
                  pl.BlockSpec((None, S, LANES), lambda b: (b, 0, 0)),
                  pl.BlockSpec((None, N_HEADS, S // ATT_TK, ATT_TK), lambda b: (b, 0, 0, 0))],
        out_specs=seq,
        out_shape=jax.ShapeDtypeStruct((T, ATT_W), F32),
        compiler_params=_params(),
        name="fox_prompt",
    )(qb, kb, vb, c, ct)


def _attn_sample_kernel(q_ref, kn_ref, vn_ref, ck_ref, cv_ref, c_ref, ct_ref, o_ref, *, P, Tn):
    r = lax.broadcasted_iota(I32, (Tn, Tn), 0)
    c = lax.broadcasted_iota(I32, (Tn, Tn), 1)
    nt = (((1,), (1,)), ((), ()))
    for h in range(N_HEADS):
        hs = slice(h * HEAD_DIM, (h + 1) * HEAD_DIM)
        q = q_ref[:, hs]
        kc = ck_ref[pl.ds(h, P, stride=N_HEADS), :].astype(BF16)
        vc = cv_ref[pl.ds(h, P, stride=N_HEADS), :].astype(BF16)
        cq = c_ref[P:P + Tn, h:h + 1]
        ckr = ct_ref[h:h + 1, :]
        s_c = lax.dot_general(q, kc, nt, preferred_element_type=F32) * ATT_SCALE
        s_c = s_c + cq - ckr[:, :P]
        s_n = lax.dot_general(q, kn_ref[:, hs], nt, preferred_element_type=F32) * ATT_SCALE
        s_n = jnp.where(r >= c, s_n + cq - ckr[:, P:P + Tn], -jnp.inf)
        m = jnp.maximum(jnp.max(s_c, axis=1, keepdims=True), jnp.max(s_n, axis=1, keepdims=True))
        p_c = jnp.exp(s_c - m)
        p_n = jnp.exp(s_n - m)
        l = jnp.sum(p_c, axis=1, keepdims=True) + jnp.sum(p_n, axis=1, keepdims=True)
        o = jnp.dot(p_c.astype(BF16), vc, preferred_element_type=F32)
        o = o + jnp.dot(p_n.astype(BF16), vn_ref[:, hs], preferred_element_type=F32)
        o_ref[:, hs] = o / l


def _attn_sample(qb, kb, vb, cache_k, cache_v, layer, c, ct, row0, Bd, P, Tn):
    blk0 = row0 // Tn
    new = pl.BlockSpec((Tn, ATT_W), lambda b: (blk0 + b, 0))
    cache = pl.BlockSpec((None, None, P * N_HEADS, HEAD_DIM), lambda b: (layer, b, 0, 0))
    return pl.pallas_call(
        functools.partial(_attn_sample_kernel, P=P, Tn=Tn),
        grid=(Bd,),
        in_specs=[new, new, new, cache, cache,
                  pl.BlockSpec((None, P + Tn, LANES), lambda b: (b, 0, 0)),
                  pl.BlockSpec((None, N_HEADS, P + Tn), lambda b: (b, 0, 0))],
        out_specs=pl.BlockSpec((Tn, ATT_W), lambda b: (b, 0)),
        out_shape=jax.ShapeDtypeStruct((Bd * Tn, ATT_W), F32),
        compiler_params=_params(),
        name="fox_sample",
    )(qb, kb, vb, cache_k, cache_v, c, ct)


def _lru_kernel(xr_ref, xg_ref, cs_ref, h0_ref, cw_ref, cb_ref, wr_ref, br_ref, wi_ref, bi_ref,
                lam_ref, o_ref, cso_ref, hl_ref, xp_ref, a_ref, u_ref, hs_ref, h_ref, *, ts):
    s = pl.program_id(1)
    pad = SUBLANES
    nprev = CONV_W - 1

    @pl.when(s == 0)
    def _():
        xp_ref[pad - nprev:pad, :] = cs_ref[...]
        h_ref[...] = h0_ref[...]

    x = xr_ref[...]
    xp_ref[pad:pad + ts, :] = x
    xc = cb_ref[...] + cw_ref[nprev:nprev + 1, :] * x
    for i in range(nprev):
        xc = xc + cw_ref[i:i + 1, :] * xp_ref[pad - nprev + i:pad - nprev + i + ts, :]
    tail = xp_ref[pad + ts - nprev:pad + ts, :]
    xp_ref[pad - nprev:pad, :] = tail
    cso_ref[...] = tail

    xcb = xc.astype(BF16)

    def block_diag(w_ref, b_ref):
        parts = [jnp.dot(xcb[:, n * LRU_BW:(n + 1) * LRU_BW], w_ref[n], preferred_element_type=F32)
                 for n in range(LRU_BLOCKS)]
        return jnp.concatenate(parts, axis=1) + b_ref[...]

    r = jax.nn.sigmoid(block_diag(wr_ref, br_ref))
    g = jax.nn.sigmoid(block_diag(wi_ref, bi_ref))
    nl = -lam_ref[...]
    softplus = jnp.maximum(nl, 0.0) + jnp.log1p(jnp.exp(-jnp.abs(nl)))
    log_a = (-LRU_C) * r * softplus
    a = jnp.exp(log_a)
    u = jnp.sqrt(1.0 - a * a) * g * xc

    sub = lax.broadcasted_iota(I32, (ts, LRU_W), 0) % SUBLANES
    d = 1
    while d < SUBLANES:
        keep = sub >= d
        u = jnp.where(keep, u + a * pltpu.roll(u, d, axis=0), u)
        a = jnp.where(keep, a * pltpu.roll(a, d, axis=0), a)
        d *= 2
    a_ref[...] = a
    u_ref[...] = u

    def group(gi, hprev):
        r0 = pl.multiple_of(gi * SUBLANES, SUBLANES)
        hh = u_ref[pl.ds(r0, SUBLANES), :] + a_ref[pl.ds(r0, SUBLANES), :] * hprev
        hs_ref[pl.ds(r0, SUBLANES), :] = hh
        return hh[SUBLANES - 1:SUBLANES, :]

    hlast = lax.fori_loop(0, ts // SUBLANES, group, h_ref[...])
    h_ref[...] = hlast
    hl_ref[...] = hlast
    o_ref[...] = hs_ref[...] * _gelu_tanh(xg_ref[...])


def _lru(xr, xg, conv_state, h0, lw, row0, nb, S, ts):
    ns = S // ts
    blk0 = row0 // ts
    rows = pl.BlockSpec((ts, LRU_W), lambda b, s: (blk0 + b * ns + s, 0))
    vec = lambda n: _resident((n, LRU_W))
    per_seq = lambda n: pl.BlockSpec((None, n, LRU_W), lambda b, s: (b, 0, 0))
    in_specs = [rows, rows, per_seq(CONV_W - 1), per_seq(1), vec(CONV_W), vec(1),
                _resident((LRU_BLOCKS, LRU_BW, LRU_BW)), vec(1),
                _resident((LRU_BLOCKS, LRU_BW, LRU_BW)), vec(1), vec(1)]
    args = [xr, xg, conv_state, h0, lw["conv_w"], lw["conv_b"], lw["w_r"], lw["b_r"],
            lw["w_i"], lw["b_i"], lw["lam"]]
    return pl.pallas_call(
        functools.partial(_lru_kernel, ts=ts),
        grid=(nb, ns),
        in_specs=in_specs,
        out_specs=[pl.BlockSpec((ts, LRU_W), lambda b, s: (b * ns + s, 0)),
                   per_seq(CONV_W - 1), per_seq(1)],
        out_shape=[jax.ShapeDtypeStruct((nb * S, LRU_W), F32),
                   jax.ShapeDtypeStruct((nb, CONV_W - 1, LRU_W), F32),
                   jax.ShapeDtypeStruct((nb, 1, LRU_W), F32)],
        scratch_shapes=[pltpu.VMEM((SUBLANES + ts, LRU_W), F32),
                        pltpu.VMEM((ts, LRU_W), F32), pltpu.VMEM((ts, LRU_W), F32),
                        pltpu.VMEM((ts, LRU_W), F32), pltpu.VMEM((1, LRU_W), F32)],
        compiler_params=_params(2),
        name="rg_lru",
    )(*args)


def _outproj_kernel(oap_ref, oas_ref, olp_ref, ols_ref, x_ref, w_ref, ga_ref, gl_ref, lg_ref,
                    lb_ref, wrh_ref, wrl_ref, rb_ref, x1_ref, x1s_ref, idx_ref, gate_ref,
                    *, alpha, n_prompt_blocks):
    is_prompt = pl.program_id(0) < n_prompt_blocks
    oa = jnp.where(is_prompt, oap_ref[...], oas_ref[...])
    ol = jnp.where(is_prompt, olp_ref[...], ols_ref[...])
    na = _rms_norm(oa, ga_ref[...]).astype(BF16)
    nl = _rms_norm(ol, gl_ref[...]).astype(BF16)
    m = jnp.dot(na, w_ref[0:ATT_W, :], preferred_element_type=F32)
    m = m + jnp.dot(nl, w_ref[ATT_W:D_MODEL, :], preferred_element_type=F32)
    x1 = _layer_norm(alpha * x_ref[...] + m, lg_ref[...], lb_ref[...])
    x1_ref[...] = x1
    _matrix_to_rows(x1s_ref, x1, x1.shape[0])

    xh = x1.astype(BF16)
    xl = (x1 - xh.astype(F32)).astype(BF16)
    logits = jnp.dot(xh, wrh_ref[...], preferred_element_type=F32)
    logits = logits + jnp.dot(xl, wrh_ref[...], preferred_element_type=F32)
    logits = logits + jnp.dot(xh, wrl_ref[...], preferred_element_type=F32)
    scores = jax.nn.sigmoid(logits)
    tm = scores.shape[0]
    lane = lax.broadcasted_iota(I32, (tm, LANES), 1)
    lane_f = lane.astype(F32)
    sel = jnp.where(lane < N_EXPERTS, scores + rb_ref[...], -jnp.inf)
    idx_out = jnp.zeros((tm, LANES), F32)
    gate_out = jnp.zeros((tm, LANES), F32)
    gsum = jnp.zeros((tm, 1), F32)
    for k in range(TOP_K):
        mx = jnp.max(sel, axis=1, keepdims=True)
        ik = jnp.min(jnp.where(sel == mx, lane_f, float(LANES)), axis=1, keepdims=True)
        hit = lane_f == ik
        gk = jnp.sum(jnp.where(hit, scores, 0.0), axis=1, keepdims=True)
        sel = jnp.where(hit, -jnp.inf, sel)
        idx_out = jnp.where(lane == k, ik, idx_out)
        gate_out = jnp.where(lane == k, gk, gate_out)
        gsum = gsum + gk
    idx_ref[...] = idx_out.astype(I32)
    gate_ref[...] = gate_out / gsum * ROUTE_SCALE


def _outproj(oa_p, oa_s, ol_p, ol_s, x, lw, alpha):
    T = x.shape[0]
    n_p = oa_p.shape[0] // TM
    row = lambda w: pl.BlockSpec((TM, w), lambda i: (i, 0))
    prompt = lambda w: pl.BlockSpec((TM, w), lambda i: (jnp.minimum(i, n_p - 1), 0))
    sample = lambda w: pl.BlockSpec((TM, w), lambda i: (jnp.maximum(i - n_p, 0), 0))
    vec = lambda n: _resident((1, n))
    return pl.pallas_call(
        functools.partial(_outproj_kernel, alpha=alpha, n_prompt_blocks=n_p),
        grid=(T // TM,),
        in_specs=[prompt(ATT_W), sample(ATT_W), prompt(LRU_W), sample(LRU_W), row(D_MODEL),
                  _resident((D_MODEL, D_MODEL)),
                  vec(ATT_W), vec(LRU_W), vec(D_MODEL), vec(D_MODEL),
                  _resident((D_MODEL, LANES)), _resident((D_MODEL, LANES)), vec(LANES)],
        out_specs=[row(D_MODEL), pl.BlockSpec((TM * SLAB_ROWS, LANES), lambda i: (i, 0)),
                   row(LANES), row(LANES)],
        out_shape=[jax.ShapeDtypeStruct((T, D_MODEL), F32),
                   jax.ShapeDtypeStruct((T * SLAB_ROWS, LANES), U32),
                   jax.ShapeDtypeStruct((T, LANES), I32),
                   jax.ShapeDtypeStruct((T, LANES), F32)],
        compiler_params=_params(),
        name="outproj_ln_router",
    )(oa_p, oa_s, ol_p, ol_s, x, lw["w_out"], lw["g_att"], lw["g_lru"], lw["ln1_g"], lw["ln1_b"],
      lw["wr_hi"], lw["wr_lo"], lw["b_router"])


def _rank_kernel(idx_ref, dest_ref, cnt_ref, carry_ref, start_ref):
    phase = pl.program_id(0)
    i = pl.program_id(1)

    @pl.when((phase == 0) & (i == 0))
    def _():
        carry_ref[...] = jnp.zeros_like(carry_ref)

    tm = idx_ref.shape[0]
    idx = idx_ref[...]
    lane = lax.broadcasted_iota(I32, (tm, LANES), 1)
    onehot = jnp.zeros((tm, LANES), F32)
    for k in range(TOP_K):
        onehot = onehot + (lane == idx[:, k:k + 1]).astype(F32)
    colsum = jnp.sum(onehot, axis=0, keepdims=True)

    @pl.when(phase == 0)
    def _():
        carry_ref[...] = carry_ref[...] + colsum
        dest_ref[...] = jnp.zeros_like(dest_ref)

    @pl.when((phase == 1) & (i == 0))
    def _():
        counts = carry_ref[...]
        cnt_ref[...] = counts.astype(I32)
        padded = jnp.floor((counts + (MOE_BLK - 1)) * (1.0 / MOE_BLK)) * MOE_BLK
        r = lax.broadcasted_iota(I32, (LANES, LANES), 0)
        c = lax.broadcasted_iota(I32, (LANES, LANES), 1)
        before = (r < c).astype(BF16)
        start = jnp.zeros((SUBLANES, LANES), F32)
        for term in _split3(jnp.broadcast_to(padded, (SUBLANES, LANES))):
            start = start + jnp.dot(term, before, preferred_element_type=F32)
        start_ref[...] = start[0:1, :] + float(TAB_CHUNK)
        carry_ref[...] = jnp.zeros_like(carry_ref)

    @pl.when(phase == 1)
    def _():
        r = lax.broadcasted_iota(I32, (tm, tm), 0)
        c = lax.broadcasted_iota(I32, (tm, tm), 1)
        below = (r > c).astype(BF16)
        entry = jnp.dot(below, onehot.astype(BF16), preferred_element_type=F32)
        entry = entry + carry_ref[...] + start_ref[...]
        out = jnp.zeros((tm, LANES), F32)
        for k in range(TOP_K):
            ek = jnp.sum(jnp.where(lane == idx[:, k:k + 1], entry, 0.0), axis=1, keepdims=True)
            out = jnp.where(lane == k, ek, out)
        dest_ref[...] = out.astype(I32)
        carry_ref[...] = carry_ref[...] + colsum


def _rank(idx):
    T = idx.shape[0]
    row = pl.BlockSpec((TM, LANES), lambda p, i: (i, 0))
    return pl.pallas_call(
        _rank_kernel,
        grid=(2, T // TM),
        in_specs=[row],
        out_specs=[pl.BlockSpec((TM, LANES), lambda p, i: (i * p, 0)),
                   pl.BlockSpec((1, LANES), lambda p, i: (0, 0))],
        out_shape=[jax.ShapeDtypeStruct((T, LANES), I32), jax.ShapeDtypeStruct((1, LANES), I32)],
        scratch_shapes=[pltpu.VMEM((1, LANES), F32), pltpu.VMEM((1, LANES), F32)],
        compiler_params=_params(2),
        name="route_rank",
    )(idx)


def _slot_table_kernel(dest_hbm, init_hbm, tab_hbm, tab, dchunk, sem):
    i = pl.program_id(0)
    n = ROUTE_CHUNK

    @pl.when(i == 0)
    def _():
        cp = pltpu.make_async_copy(init_hbm, tab, sem.at[0])
        cp.start()
        cp.wait()

    off = pl.multiple_of(i * n, n)
    cp = pltpu.make_async_copy(dest_hbm.at[pl.ds(off, n)], dchunk, sem.at[1])
    cp.start()
    cp.wait()

    def group(j8, carry):
        for u in range(SUBLANES):
            j = j8 * SUBLANES + u
            tab[dchunk[j]] = off + j
        return carry

    lax.fori_loop(0, n // SUBLANES, group, 0)

    @pl.when(i == pl.num_programs(0) - 1)
    def _():
        cp = pltpu.make_async_copy(tab, tab_hbm, sem.at[0])
        cp.start()
        cp.wait()


def _slot_table(dest_flat, n_slots):
    n = dest_flat.shape[0]
    n_tokens = n // TOP_K
    n_tab = n_slots + TAB_CHUNK
    sentinel = ((n_tokens + jnp.arange(n_tab, dtype=I32) % MOE_BLK) << TOP_K_SHIFT) | (TOP_K - 1)
    any_spec = pl.BlockSpec(memory_space=pl.ANY)
    return pl.pallas_call(
        _slot_table_kernel,
        grid=(n // ROUTE_CHUNK,),
        in_specs=[any_spec, any_spec],
        out_specs=any_spec,
        out_shape=jax.ShapeDtypeStruct((n_tab,), I32),
        scratch_shapes=[pltpu.SMEM((n_tab,), I32), pltpu.SMEM((ROUTE_CHUNK,), I32),
                        pltpu.SemaphoreType.DMA((2,))],
        compiler_params=_params(),
        name="moe_slot_table",
    )(dest_flat, sentinel)


def _rows_to_matrix(ref, n):
    cols = []
    for r in range(SLAB_ROWS):
        w = ref[pl.ds(r, n, stride=SLAB_ROWS), :]
        cols.append(pltpu.bitcast(w << 16, F32))
        cols.append(pltpu.bitcast(w & jnp.uint32(0xFFFF0000), F32))
    return jnp.concatenate(cols, axis=1)


def _matrix_to_rows(ref, y, n):
    for r in range(SLAB_ROWS):
        lo = y[:, (2 * r) * LANES:(2 * r + 1) * LANES].astype(BF16).astype(F32)
        hi = y[:, (2 * r + 1) * LANES:(2 * r + 2) * LANES].astype(BF16).astype(F32)
        ref[pl.ds(r, n, stride=SLAB_ROWS), :] = (
            pltpu.bitcast(hi, U32) | (pltpu.bitcast(lo, U32) >> 16))


def _experts_kernel(first_ref, nu_ref, x1s_hbm, tab_hbm, wg_ref, wu_ref, wd_ref, yk_hbm,
                    wg_s, wu_s, wd_s, xbuf0, xbuf1, ybuf0, ybuf1, stk, gsem, ssem, tsem,
                    *, n_tokens):
    e = pl.program_id(0)
    nu = nu_ref[0]
    blk_rows = MOE_BLK * SLAB_ROWS
    xbufs = (xbuf0, xbuf1)
    ybufs = (ybuf0, ybuf1)

    def table_copy(g):
        gt = g + TAB_RING
        q = gt % TAB_RING
        src = tab_hbm.at[pl.ds(pl.multiple_of((gt // TAB_RING) * TAB_CHUNK, TAB_CHUNK), TAB_CHUNK)]
        dst = stk.at[pl.ds(pl.multiple_of(q * TAB_CHUNK, TAB_CHUNK), TAB_CHUNK)]
        return pltpu.make_async_copy(src, dst, tsem.at[q])

    def table_base(g):
        return ((g + TAB_RING) % TAB_RING) * (TAB_CHUNK + MOE_BLK)

    def gather_row(base, r, xb, sem):
        tok = jnp.minimum(stk[base + r] >> TOP_K_SHIFT, n_tokens - 1)
        src = x1s_hbm.at[pl.ds(pl.multiple_of(tok * SLAB_ROWS, SLAB_ROWS), SLAB_ROWS)]
        pltpu.make_async_copy(src, xb.at[pl.ds(r * SLAB_ROWS, SLAB_ROWS)], sem).start()

    def scatter_row(base, r, yb, sem):
        v = stk[base + r]
        row = (v & (TOP_K - 1)) * n_tokens + (v >> TOP_K_SHIFT)
        dst = yk_hbm.at[pl.ds(pl.multiple_of(row * SLAB_ROWS, SLAB_ROWS), SLAB_ROWS)]
        pltpu.make_async_copy(yb.at[pl.ds(r * SLAB_ROWS, SLAB_ROWS)], dst, sem).start(priority=1)

    def wait_gather(xb, sem):
        pltpu.make_async_copy(x1s_hbm.at[pl.ds(0, blk_rows)], xb, sem).wait()

    def wait_scatter(yb, sem):
        pltpu.make_async_copy(yb, yk_hbm.at[pl.ds(0, blk_rows)], sem).wait()

    @pl.when(e == 0)
    def _():
        ybuf1[...] = jnp.zeros_like(ybuf1)
        table_copy(-1).start()
        table_copy(0).start()

        @pl.when(nu > 1)
        def _():
            table_copy(1).start()

        table_copy(-1).wait()
        table_copy(0).wait()
        base = table_base(0)
        for r in range(MOE_BLK):
            gather_row(base, r, xbuf0, gsem.at[0])

    g0 = first_ref[e]
    g1 = first_ref[e + 1]

    @pl.when(g1 > g0)
    def _():
        wg_s[...] = wg_ref[...].astype(BF16)
        wu_s[...] = wu_ref[...].astype(BF16)
        wd_s[...] = wd_ref[...].astype(BF16)

    def step(g, p):
        xb_cur, xb_nxt = xbufs[p], xbufs[1 - p]
        yb_cur, yb_prv = ybufs[p], ybufs[1 - p]

        @pl.when(g + 2 < nu)
        def _():
            table_copy(g + 2).start()

        @pl.when(g + 1 < nu)
        def _():
            table_copy(g + 1).wait()

        wait_gather(xb_cur, gsem.at[p])

        @pl.when(g >= 1)
        def _():
            wait_scatter(yb_cur, ssem.at[p])

        base_next = table_base(jnp.minimum(g + 1, nu - 1))
        base_prev = table_base(g - 1)
        for r in range(MOE_BLK):
            gather_row(base_next, r, xb_nxt, gsem.at[1 - p])
        for r in range(MOE_BLK):
            scatter_row(base_prev, r, yb_prv, ssem.at[1 - p])
        x = _rows_to_matrix(xb_cur, MOE_BLK).astype(BF16)
        gt = jnp.dot(x, wg_s[...], preferred_element_type=F32)
        up = jnp.dot(x, wu_s[...], preferred_element_type=F32)
        h = (_silu(gt) * up).astype(BF16)
        y = jnp.dot(h, wd_s[...], preferred_element_type=F32)
        _matrix_to_rows(yb_cur, y, MOE_BLK)

        @pl.when(g == nu - 1)
        def _():
            wait_scatter(yb_prv, ssem.at[1 - p])
            base = table_base(g)

            def group(i, carry):
                for u in range(SUBLANES):
                    r = i * SUBLANES + u
                    v = stk[base + r]
                    row = (v & (TOP_K - 1)) * n_tokens + (v >> TOP_K_SHIFT)
                    src = yb_cur.at[pl.ds(pl.multiple_of(r * SLAB_ROWS, SLAB_ROWS), SLAB_ROWS)]
                    dst = yk_hbm.at[pl.ds(pl.multiple_of(row * SLAB_ROWS, SLAB_ROWS), SLAB_ROWS)]
                    pltpu.make_async_copy(src, dst, ssem.at[p]).start()
                return carry

            lax.fori_loop(0, MOE_BLK // SUBLANES, group, 0)
            wait_scatter(yb_cur, ssem.at[p])
            wait_gather(xb_nxt, gsem.at[1 - p])

    def block(g, carry):
        @pl.when(g % 2 == 0)
        def _():
            step(g, 0)

        @pl.when(g % 2 == 1)
        def _():
            step(g, 1)

        return carry

    lax.fori_loop(g0, g1, block, 0)


def _experts(first_blk, n_used, x1s, slot_tab, we_gate, we_up, we_down, layer):
    n_tokens = x1s.shape[0] // SLAB_ROWS
    blk_rows = MOE_BLK * SLAB_ROWS
    any_spec = pl.BlockSpec(memory_space=pl.ANY)
    w_in_spec = pl.BlockSpec((None, None, D_MODEL, D_EXPERT), lambda e, *_: (layer, e, 0, 0))
    grid_spec = pltpu.PrefetchScalarGridSpec(
        num_scalar_prefetch=2,
        grid=(N_EXPERTS,),
        in_specs=[any_spec, any_spec, w_in_spec, w_in_spec,
                  pl.BlockSpec((None, None, D_EXPERT, D_MODEL), lambda e, *_: (layer, e, 0, 0))],
        out_specs=any_spec,
        scratch_shapes=[pltpu.VMEM((D_MODEL, D_EXPERT), BF16), pltpu.VMEM((D_MODEL, D_EXPERT), BF16),
                        pltpu.VMEM((D_EXPERT, D_MODEL), BF16),
                        pltpu.VMEM((blk_rows, LANES), U32), pltpu.VMEM((blk_rows, LANES), U32),
                        pltpu.VMEM((blk_rows, LANES), U32), pltpu.VMEM((blk_rows, LANES), U32),
                        pltpu.SMEM((TAB_RING * TAB_CHUNK,), I32),
                        pltpu.SemaphoreType.DMA((2,)), pltpu.SemaphoreType.DMA((2,)),
                        pltpu.SemaphoreType.DMA((TAB_RING,))],
    )
    return pl.pallas_call(
        functools.partial(_experts_kernel, n_tokens=n_tokens),
        grid_spec=grid_spec,
        out_shape=jax.ShapeDtypeStruct(((TOP_K * n_tokens + MOE_BLK) * SLAB_ROWS, LANES), U32),
        compiler_params=_params(),
        name="moe_experts",
    )(first_blk, n_used, x1s, slot_tab, we_gate, we_up, we_down)


def _combine_kernel(*refs, alpha):
    yk_refs = refs[:TOP_K]
    gate_ref, x_ref, wsg_ref, wsu_ref, wsd_ref, lg_ref, lb_ref, o_ref = refs[TOP_K:]
    tm = x_ref.shape[0]
    x1 = x_ref[...]
    xb = x1.astype(BF16)
    hs = _silu(jnp.dot(xb, wsg_ref[...], preferred_element_type=F32))
    hs = (hs * jnp.dot(xb, wsu_ref[...], preferred_element_type=F32)).astype(BF16)
    acc = jnp.dot(hs, wsd_ref[...], preferred_element_type=F32)
    gates = gate_ref[...]
    for k in range(TOP_K):
        acc = acc + gates[:, k:k + 1] * _rows_to_matrix(yk_refs[k], tm)
    o_ref[...] = _layer_norm(alpha * x1 + acc, lg_ref[...], lb_ref[...])


def _combine(yk, gates, x1, lw, alpha):
    T = x1.shape[0]
    tm = TM_COMBINE
    row = lambda w: pl.BlockSpec((tm, w), lambda i: (i, 0))
    plane = lambda k: pl.BlockSpec((tm * SLAB_ROWS, LANES), lambda i: (k * (T // tm) + i, 0))
    return pl.pallas_call(
        functools.partial(_combine_kernel, alpha=alpha),
        grid=(T // tm,),
        in_specs=[plane(k) for k in range(TOP_K)] + [
                  row(LANES), row(D_MODEL),
                  _resident((D_MODEL, D_SHARED)), _resident((D_MODEL, D_SHARED)),
                  _resident((D_SHARED, D_MODEL)), _resident((1, D_MODEL)), _resident((1, D_MODEL))],
        out_specs=row(D_MODEL),
        out_shape=jax.ShapeDtypeStruct((T, D_MODEL), F32),
        compiler_params=_params(),
        name="moe_combine",
    )(*([yk] * TOP_K), gates, x1, lw["ws_gate"], lw["ws_up"], lw["ws_down"], lw["ln2_g"],
      lw["ln2_b"])


def _block_tables(counts):
    padded = (counts + MOE_BLK - 1) // MOE_BLK * MOE_BLK
    pad_end = jnp.cumsum(padded)
    pad_start = pad_end - padded
    first_blk = jnp.concatenate([pad_start, pad_end[-1:]]) // MOE_BLK
    n_used = pad_end[-1:] // MOE_BLK
    return first_blk.astype(I32), n_used.astype(I32)


def _moe(x1, x1s, idx, gates, lw, we_gate, we_up, we_down, layer, alpha):
    T = x1.shape[0]
    n_blocks = T * TOP_K // MOE_BLK + N_EXPERTS
    dest, cnt = _rank(idx)
    first_blk, n_used = _block_tables(cnt[0, :N_EXPERTS])
    slot_tab = _slot_table(dest[:, :TOP_K].reshape(-1), n_blocks * MOE_BLK)
    yk = _experts(first_blk, n_used, x1s, slot_tab, we_gate, we_up, we_down, layer)
    return _combine(yk, gates, x1, lw, alpha)


def _prep_layer(l, w_in, b_f, w_out, g_att, g_lru, conv_w, conv_b, w_r, b_r, w_i, b_i, lru_lambda,
                ln1_g, ln1_b, w_router, b_router, ws_gate, ws_up, ws_down, ln2_g, ln2_b):
    qkv_end = 3 * ATT_W
    wl = w_in[l]
    w_pad = jnp.concatenate(
        [wl[:, :qkv_end], wl[:, qkv_end + N_HEADS:],
         jnp.pad(wl[:, qkv_end:qkv_end + N_HEADS], ((0, 0), (0, LANES - N_HEADS)))],
        axis=1).astype(BF16)
    wr = jnp.pad(w_router[l], ((0, 0), (0, LANES - N_EXPERTS)))
    wr_hi = wr.astype(BF16)
    return dict(
        w_pad=w_pad,
        bf_pad=jnp.pad(b_f[l], (0, LANES - N_HEADS)).reshape(1, LANES),
        w_out=w_out[l].astype(BF16),
        g_att=g_att[l].reshape(1, ATT_W), g_lru=g_lru[l].reshape(1, LRU_W),
        conv_w=conv_w[l], conv_b=conv_b[l].reshape(1, LRU_W),
        w_r=w_r[l].astype(BF16), b_r=b_r[l].reshape(1, LRU_W),
        w_i=w_i[l].astype(BF16), b_i=b_i[l].reshape(1, LRU_W),
        lam=lru_lambda[l].reshape(1, LRU_W),
        ln1_g=ln1_g[l].reshape(1, D_MODEL), ln1_b=ln1_b[l].reshape(1, D_MODEL),
        wr_hi=wr_hi, wr_lo=(wr - wr_hi.astype(F32)).astype(BF16),
        b_router=jnp.pad(b_router[l], (0, LANES - N_EXPERTS)).reshape(1, LANES),
        ws_gate=ws_gate[l].astype(BF16), ws_up=ws_up[l].astype(BF16),
        ws_down=ws_down[l].astype(BF16),
        ln2_g=ln2_g[l].reshape(1, D_MODEL), ln2_b=ln2_b[l].reshape(1, D_MODEL),
    )


def kernel(x_prompt, x_sample, cache_k, cache_v, cache_logf, state_conv, state_lru, ln_in_g, ln_in_b, w_in, b_f, w_out, g_att, g_lru, conv_w, conv_b, w_r, b_r, w_i, b_i, lru_lambda, ln1_g, ln1_b, w_router, b_router, we_gate, we_up, we_down, ws_gate, ws_up, ws_down, ln2_g, ln2_b):
    B, S, _ = x_prompt.shape
    Bd, Tn, _ = x_sample.shape
    depth = w_in.shape[0]
    P = cache_k.shape[2]
    Tp = B * S
    Ts = Bd * Tn
    alpha = (2 * depth) ** 0.25

    x = jnp.concatenate([x_prompt.reshape(Tp, D_MODEL), x_sample.reshape(Ts, D_MODEL)], axis=0)
    ln_g = ln_in_g.reshape(1, D_MODEL)
    ln_b = ln_in_b.reshape(1, D_MODEL)
    zero_conv = jnp.zeros((B, CONV_W - 1, LRU_W), F32)
    zero_h = jnp.zeros((B, 1, LRU_W), F32)

    ks, vs, lfs, p_conv, p_lru, s_conv, s_lru = [], [], [], [], [], [], []
    for l in range(depth):
        lw = _prep_layer(l, w_in, b_f, w_out, g_att, g_lru, conv_w, conv_b, w_r, b_r, w_i, b_i,
                         lru_lambda, ln1_g, ln1_b, w_router, b_router, ws_gate, ws_up, ws_down,
                         ln2_g, ln2_b)
        outs = _inproj(x, ln_g, ln_b, lw["w_pad"], lw["bf_pad"], apply_ln=(l == 0))
        if l == 0:
            x, outs = outs[0], outs[1:]
        qb, k, v, kb, vb, xr, xg, lf = outs

        c_p = _cumsum(lf[:Tp].reshape(B, S, LANES))
        ct_p = c_p[:, :, :N_HEADS].transpose(0, 2, 1).reshape(B, N_HEADS, S // ATT_TK, ATT_TK)
        lf_cache = jnp.pad(cache_logf[l], ((0, 0), (0, 0), (0, LANES - N_HEADS)))
        c_s = _cumsum(jnp.concatenate([lf_cache, lf[Tp:].reshape(Bd, Tn, LANES)], axis=1))
        ct_s = c_s[:, :, :N_HEADS].transpose(0, 2, 1)

        oa_p = _attn_prompt(qb, kb, vb, c_p, ct_p, B, S)
        oa_s = _attn_sample(qb, kb, vb, cache_k.reshape(depth, Bd, P * N_HEADS, HEAD_DIM),
                            cache_v.reshape(depth, Bd, P * N_HEADS, HEAD_DIM), l, c_s, ct_s,
                            Tp, Bd, P, Tn)

        ol_p, pc, ph = _lru(xr, xg, zero_conv, zero_h, lw, 0, B, S, LRU_TS)
        ol_s, sc, sh = _lru(xr, xg, state_conv[l], state_lru[l].reshape(Bd, 1, LRU_W), lw,
                            Tp, Bd, Tn, Tn)

        x1, x1s, idx, gates = _outproj(oa_p, oa_s, ol_p, ol_s, x, lw, alpha)
        x = _moe(x1, x1s, idx, gates, lw, we_gate, we_up, we_down, l, alpha)

        ks.append(k)
        vs.append(v)
        lfs.append(lf[:, :N_HEADS])
        p_conv.append(pc)
        p_lru.append(ph.reshape(B, LRU_W))
        s_conv.append(sc)
        s_lru.append(sh.reshape(Bd, LRU_W))

    def split(ts, tail):
        p = jnp.stack([t[:Tp].reshape((B, S) + tail) for t in ts])
        s = jnp.stack([t[Tp:].reshape((Bd, Tn) + tail) for t in ts])
        return p, s

    p_k, s_k = split(ks, (N_HEADS, HEAD_DIM))
    p_v, s_v = split(vs, (N_HEADS, HEAD_DIM))
    p_logf, s_logf = split(lfs, (N_HEADS,))
    y_prompt = x[:Tp].reshape(B, S, D_MODEL)
    y_sample = x[Tp:].reshape(Bd, Tn, D_MODEL)
    return (y_prompt, y_sample, p_k, p_v, p_logf, jnp.stack(p_conv), jnp.stack(p_lru),
            s_k, s_v, s_logf, jnp.stack(s_conv), jnp.stack(s_lru))
```

```python
import functools

import jax
import jax.numpy as jnp
from jax import lax
from jax.experimental import pallas as pl
from jax.experimental.pallas import tpu as pltpu

F32 = jnp.float32
BF16 = jnp.bfloat16
I32 = jnp.int32
U32 = jnp.uint32

D_MODEL = 2048
N_HEADS = 8
HEAD_DIM = 128
ATT_W = N_HEADS * HEAD_DIM
LRU_W = D_MODEL - ATT_W
LRU_BLOCKS = 8
LRU_BW = LRU_W // LRU_BLOCKS
CONV_W = 4
LRU_C = 8.0
N_EXPERTS = 64
TOP_K = 8
TOP_K_SHIFT = TOP_K.bit_length() - 1
assert 1 << TOP_K_SHIFT == TOP_K
D_EXPERT = 512
D_SHARED = 512
ROUTE_SCALE = 2.5
ATT_SCALE = HEAD_DIM ** -0.5
LN_EPS = 1e-5
RMS_EPS = 1e-6

LANES = 128
SUBLANES = 8
SLAB_ROWS = D_MODEL // (2 * LANES)
VMEM_LIMIT = 56 * 1024 * 1024

TM = 256
MOE_BLK = 256
TM_COMBINE = 128
ROUTE_CHUNK = 2048
ATT_TQ = 512
ATT_TK = ATT_TQ
ATT_HEAD_GROUP = 2
LRU_TS = 256
CUM_CHUNK = 128

C_Q, C_K, C_V, C_XR, C_XG, C_F, C_END = 0, 1024, 2048, 3072, 4096, 5120, 5248


def _params(n_axes=1):
    return pltpu.CompilerParams(dimension_semantics=("arbitrary",) * n_axes,
                                vmem_limit_bytes=VMEM_LIMIT)


def _resident(shape):
    nd = len(shape)
    return pl.BlockSpec(shape, lambda *_: (0,) * nd, pipeline_mode=pl.Buffered(1))


def _layer_norm(x, g, b):
    mu = jnp.mean(x, axis=-1, keepdims=True)
    xc = x - mu
    var = jnp.mean(xc * xc, axis=-1, keepdims=True)
    return xc * lax.rsqrt(var + LN_EPS) * g + b


def _rms_norm(x, g):
    return x * lax.rsqrt(jnp.mean(x * x, axis=-1, keepdims=True) + RMS_EPS) * g


def _silu(x):
    return x * jax.nn.sigmoid(x)


def _gelu_tanh(x):
    return 0.5 * x * (1.0 + jnp.tanh(0.7978845608028654 * (x + 0.044715 * (x * x * x))))


def _split3(x):
    hi = x.astype(BF16)
    r1 = x - hi.astype(F32)
    mid = r1.astype(BF16)
    lo = (r1 - mid.astype(F32)).astype(BF16)
    return hi, mid, lo


def _inproj_kernel(apply_ln, x_ref, g_ref, b_ref, w_ref, bf_ref, *outs):
    if apply_ln:
        x0_ref, outs = outs[0], outs[1:]
    q_ref, k_ref, v_ref, kb_ref, vb_ref, xr_ref, xg_ref, lf_ref = outs
    x = x_ref[...]
    if apply_ln:
        x = _layer_norm(x, g_ref[...], b_ref[...])
        x0_ref[...] = x
    h = x.astype(BF16)

    def mm(c0, c1):
        return jnp.dot(h, w_ref[:, c0:c1], preferred_element_type=F32)

    q_ref[...] = mm(C_Q, C_K).astype(BF16)
    k = mm(C_K, C_V)
    k_ref[...] = k
    kb_ref[...] = k.astype(BF16)
    v = mm(C_V, C_XR)
    v_ref[...] = v
    vb_ref[...] = v.astype(BF16)
    xr_ref[...] = mm(C_XR, C_XG)
    xg_ref[...] = mm(C_XG, C_F)
    f = mm(C_F, C_END) + bf_ref[...]
    lf_ref[...] = jnp.minimum(f, 0.0) - jnp.log1p(jnp.exp(-jnp.abs(f)))


def _inproj(x, ln_g, ln_b, w_pad, bf_pad, apply_ln):
    T = x.shape[0]
    row = lambda w: pl.BlockSpec((TM, w), lambda i: (i, 0))
    out_shape = [
        jax.ShapeDtypeStruct((T, ATT_W), BF16),
        jax.ShapeDtypeStruct((T, ATT_W), F32),
        jax.ShapeDtypeStruct((T, ATT_W), F32),
        jax.ShapeDtypeStruct((T, ATT_W), BF16),
        jax.ShapeDtypeStruct((T, ATT_W), BF16),
        jax.ShapeDtypeStruct((T, LRU_W), F32),
        jax.ShapeDtypeStruct((T, LRU_W), F32),
        jax.ShapeDtypeStruct((T, LANES), F32),
    ]
    out_specs = [row(ATT_W)] * 5 + [row(LRU_W)] * 2 + [row(LANES)]
    if apply_ln:
        out_shape = [jax.ShapeDtypeStruct((T, D_MODEL), F32)] + out_shape
        out_specs = [row(D_MODEL)] + out_specs
    return pl.pallas_call(
        functools.partial(_inproj_kernel, apply_ln),
        grid=(T // TM,),
        in_specs=[row(D_MODEL), _resident((1, D_MODEL)), _resident((1, D_MODEL)),
                  _resident((D_MODEL, C_END)), _resident((1, LANES))],
        out_specs=out_specs,
        out_shape=out_shape,
        compiler_params=_params(),
        name="inproj",
    )(x, ln_g, ln_b, w_pad, bf_pad)


def _cumsum_kernel(lf_ref, c_ref, *, S):
    carry = jnp.zeros((1, LANES), F32)
    for c0 in range(0, S, CUM_CHUNK):
        n = min(CUM_CHUNK, S - c0)
        r = lax.broadcasted_iota(I32, (n, n), 0)
        c = lax.broadcasted_iota(I32, (n, n), 1)
        tril = (r >= c).astype(BF16)
        hi, mid, lo = _split3(lf_ref[c0:c0 + n, :])
        acc = jnp.dot(tril, hi, preferred_element_type=F32)
        acc = acc + jnp.dot(tril, mid, preferred_element_type=F32)
        acc = acc + jnp.dot(tril, lo, preferred_element_type=F32)
        acc = acc + carry
        c_ref[c0:c0 + n, :] = acc
        carry = acc[n - 1:n, :]


def _cumsum(lf):
    B, S, _ = lf.shape
    spec = pl.BlockSpec((None, S, LANES), lambda b: (b, 0, 0))
    return pl.pallas_call(
        functools.partial(_cumsum_kernel, S=S),
        grid=(B,),
        in_specs=[spec],
        out_specs=spec,
        out_shape=jax.ShapeDtypeStruct((B, S, LANES), F32),
        compiler_params=_params(),
        name="cumsum_logf",
    )(lf)


def _attn_prompt_kernel(q_ref, k_ref, v_ref, c_ref, ct_ref, o_ref, *, S):
    nq = S // ATT_TQ
    for h0 in range(0, N_HEADS, ATT_HEAD_GROUP):
        heads = range(h0, h0 + ATT_HEAD_GROUP)

        def q_body(qi, carry, heads=heads):
            q0 = pl.multiple_of(qi * ATT_TQ, ATT_TQ)
            hsl = [slice(h * HEAD_DIM, (h + 1) * HEAD_DIM) for h in heads]
            qs = [q_ref[pl.ds(q0, ATT_TQ), hs] for hs in hsl]
            cqs = [c_ref[pl.ds(q0, ATT_TQ), h:h + 1] for h in heads]

            def kv_step(kj, sts, diagonal):
                k0 = pl.multiple_of(kj * ATT_TK, ATT_TK)
                out = []
                for h, hs, q, cq, (m, l, acc) in zip(heads, hsl, qs, cqs, sts):
                    k = k_ref[pl.ds(k0, ATT_TK), hs]
                    v = v_ref[pl.ds(k0, ATT_TK), hs]
                    ck = ct_ref[h, pl.ds(kj, 1), :]
                    s = lax.dot_general(q, k, (((1,), (1,)), ((), ())),
                                        preferred_element_type=F32) * ATT_SCALE
                    s = s + cq - ck
                    if diagonal:
                        row = lax.broadcasted_iota(I32, (ATT_TQ, ATT_TK), 0)
                        col = lax.broadcasted_iota(I32, (ATT_TQ, ATT_TK), 1)
                        s = jnp.where(row >= col, s, -jnp.inf)
                    m_new = jnp.maximum(m, jnp.max(s, axis=1, keepdims=True))
                    alpha = jnp.exp(m - m_new)
                    p = jnp.exp(s - m_new)
                    l = alpha * l + jnp.sum(p, axis=1, keepdims=True)
                    acc = alpha * acc + jnp.dot(p.astype(BF16), v, preferred_element_type=F32)
                    out.append((m_new, l, acc))
                return tuple(out)

            init = tuple((jnp.full((ATT_TQ, 1), -jnp.inf, F32), jnp.zeros((ATT_TQ, 1), F32),
                          jnp.zeros((ATT_TQ, HEAD_DIM), F32)) for _ in heads)
            sts = lax.fori_loop(0, qi, lambda kj, sts: kv_step(kj, sts, False), init)
            sts = kv_step(qi, sts, True)
            for hs, (m, l, acc) in zip(hsl, sts):
                o_ref[pl.ds(q0, ATT_TQ), hs] = acc / l
            return carry

        lax.fori_loop(0, nq, q_body, 0)


def _attn_prompt(qb, kb, vb, c, ct, B, S):
    T = B * S
    seq = pl.BlockSpec((S, ATT_W), lambda b: (b, 0))
    return pl.pallas_call(
        functools.partial(_attn_prompt_kernel, S=S),
        grid=(B,),
        in_specs=[seq, seq, seq,
                  pl.BlockSpec((None, S, LANES), lambda b: (b, 0, 0)),
                  pl.BlockSpec((None, N_HEADS, S // ATT_TK, ATT_TK), lambda b: (b, 0, 0, 0))],
        out_specs=seq,
        out_shape=jax.ShapeDtypeStruct((T, ATT_W), F32),
        compiler_params=_params(),
        name="fox_prompt",
    )(qb, kb, vb, c, ct)


def _attn_sample_kernel(q_ref, kn_ref, vn_ref, ck_ref, cv_ref, c_ref, ct_ref, o_ref, *, P, Tn):
    r = lax.broadcasted_iota(I32, (Tn, Tn), 0)
    c = lax.broadcasted_iota(I32, (Tn, Tn), 1)
    nt = (((1,), (1,)), ((), ()))
    for h in range(N_HEADS):
        hs = slice(h * HEAD_DIM, (h + 1) * HEAD_DIM)
        q = q_ref[:, hs]
        kc = ck_ref[pl.ds(h, P, stride=N_HEADS), :].astype(BF16)
        vc = cv_ref[pl.ds(h, P, stride=N_HEADS), :].astype(BF16)
        cq = c_ref[P:P + Tn, h:h + 1]
        ckr = ct_ref[h:h + 1, :]
        s_c = lax.dot_general(q, kc, nt, preferred_element_type=F32) * ATT_SCALE
        s_c = s_c + cq - ckr[:, :P]
        s_n = lax.dot_general(q, kn_ref[:, hs], nt, preferred_element_type=F32) * ATT_SCALE
        s_n = jnp.where(r >= c, s_n + cq - ckr[:, P:P + Tn], -jnp.inf)
        m = jnp.maximum(jnp.max(s_c, axis=1, keepdims=True), jnp.max(s_n, axis=1, keepdims=True))
        p_c = jnp.exp(s_c - m)
        p_n = jnp.exp(s_n - m)
        l = jnp.sum(p_c, axis=1, keepdims=True) + jnp.sum(p_n, axis=1, keepdims=True)
        o = jnp.dot(p_c.astype(BF16), vc, preferred_element_type=F32)
        o = o + jnp.dot(p_n.astype(BF16), vn_ref[:, hs], preferred_element_type=F32)
        o_ref[:, hs] = o / l


def _attn_sample(qb, kb, vb, cache_k, cache_v, layer, c, ct, row0, Bd, P, Tn):
    blk0 = row0 // Tn
    new = pl.BlockSpec((Tn, ATT_W), lambda b: (blk0 + b, 0))
    cache = pl.BlockSpec((None, None, P * N_HEADS, HEAD_DIM), lambda b: (layer, b, 0, 0))
    return pl.pallas_call(
        functools.partial(_attn_sample_kernel, P=P, Tn=Tn),
        grid=(Bd,),
        in_specs=[new, new, new, cache, cache,
                  pl.BlockSpec((None, P + Tn, LANES), lambda b: (b, 0, 0)),
                  pl.BlockSpec((None, N_HEADS, P + Tn), lambda b: (b, 0, 0))],
        out_specs=pl.BlockSpec((Tn, ATT_W), lambda b: (b, 0)),
        out_shape=jax.ShapeDtypeStruct((Bd * Tn, ATT_W), F32),
        compiler_params=_params(),
        name="fox_sample",
    )(qb, kb, vb, cache_k, cache_v, c, ct)


def _lru_kernel(xr_ref, xg_ref, cs_ref, h0_ref, cw_ref, cb_ref, wr_ref, br_ref, wi_ref, bi_ref,
                lam_ref, o_ref, cso_ref, hl_ref, xp_ref, a_ref, u_ref, hs_ref, h_ref, *, ts):
    s = pl.program_id(1)
    pad = SUBLANES
    nprev = CONV_W - 1

    @pl.when(s == 0)
    def _():
        xp_ref[pad - nprev:pad, :] = cs_ref[...]
        h_ref[...] = h0_ref[...]

    x = xr_ref[...]
    xp_ref[pad:pad + ts, :] = x
    xc = cb_ref[...] + cw_ref[nprev:nprev + 1, :] * x
    for i in range(nprev):
        xc = xc + cw_ref[i:i + 1, :] * xp_ref[pad - nprev + i:pad - nprev + i + ts, :]
    tail = xp_ref[pad + ts - nprev:pad + ts, :]
    xp_ref[pad - nprev:pad, :] = tail
    cso_ref[...] = tail

    xcb = xc.astype(BF16)

    def block_diag(w_ref, b_ref):
        parts = [jnp.dot(xcb[:, n * LRU_BW:(n + 1) * LRU_BW], w_ref[n], preferred_element_type=F32)
                 for n in range(LRU_BLOCKS)]
        return jnp.concatenate(parts, axis=1) + b_ref[...]

    r = jax.nn.sigmoid(block_diag(wr_ref, br_ref))
    g = jax.nn.sigmoid(block_diag(wi_ref, bi_ref))
    nl = -lam_ref[...]
    softplus = jnp.maximum(nl, 0.0) + jnp.log1p(jnp.exp(-jnp.abs(nl)))
    log_a = (-LRU_C) * r * softplus
    a = jnp.exp(log_a)
    u = jnp.sqrt(1.0 - a * a) * g * xc

    sub = lax.broadcasted_iota(I32, (ts, LRU_W), 0) % SUBLANES
    d = 1
    while d < SUBLANES:
        keep = sub >= d
        u = jnp.where(keep, u + a * pltpu.roll(u, d, axis=0), u)
        a = jnp.where(keep, a * pltpu.roll(a, d, axis=0), a)
        d *= 2
    a_ref[...] = a
    u_ref[...] = u

    def group(gi, hprev):
        r0 = pl.multiple_of(gi * SUBLANES, SUBLANES)
        hh = u_ref[pl.ds(r0, SUBLANES), :] + a_ref[pl.ds(r0, SUBLANES), :] * hprev
        hs_ref[pl.ds(r0, SUBLANES), :] = hh
        return hh[SUBLANES - 1:SUBLANES, :]

    hlast = lax.fori_loop(0, ts // SUBLANES, group, h_ref[...])
    h_ref[...] = hlast
    hl_ref[...] = hlast
    o_ref[...] = hs_ref[...] * _gelu_tanh(xg_ref[...])


def _lru(xr, xg, conv_state, h0, lw, row0, nb, S, ts):
    ns = S // ts
    blk0 = row0 // ts
    rows = pl.BlockSpec((ts, LRU_W), lambda b, s: (blk0 + b * ns + s, 0))
    vec = lambda n: _resident((n, LRU_W))
    per_seq = lambda n: pl.BlockSpec((None, n, LRU_W), lambda b, s: (b, 0, 0))
    in_specs = [rows, rows, per_seq(CONV_W - 1), per_seq(1), vec(CONV_W), vec(1),
                _resident((LRU_BLOCKS, LRU_BW, LRU_BW)), vec(1),
                _resident((LRU_BLOCKS, LRU_BW, LRU_BW)), vec(1), vec(1)]
    args = [xr, xg, conv_state, h0, lw["conv_w"], lw["conv_b"], lw["w_r"], lw["b_r"],
            lw["w_i"], lw["b_i"], lw["lam"]]
    return pl.pallas_call(
        functools.partial(_lru_kernel, ts=ts),
        grid=(nb, ns),
        in_specs=in_specs,
        out_specs=[pl.BlockSpec((ts, LRU_W), lambda b, s: (b * ns + s, 0)),
                   per_seq(CONV_W - 1), per_seq(1)],
        out_shape=[jax.ShapeDtypeStruct((nb * S, LRU_W), F32),
                   jax.ShapeDtypeStruct((nb, CONV_W - 1, LRU_W), F32),
                   jax.ShapeDtypeStruct((nb, 1, LRU_W), F32)],
        scratch_shapes=[pltpu.VMEM((SUBLANES + ts, LRU_W), F32),
                        pltpu.VMEM((ts, LRU_W), F32), pltpu.VMEM((ts, LRU_W), F32),
                        pltpu.VMEM((ts, LRU_W), F32), pltpu.VMEM((1, LRU_W), F32)],
        compiler_params=_params(2),
        name="rg_lru",
    )(*args)


def _outproj_kernel(oap_ref, oas_ref, olp_ref, ols_ref, x_ref, w_ref, ga_ref, gl_ref, lg_ref,
                    lb_ref, wrh_ref, wrl_ref, rb_ref, x1_ref, x1s_ref, idx_ref, gate_ref,
                    *, alpha, n_prompt_blocks):
    is_prompt = pl.program_id(0) < n_prompt_blocks
    oa = jnp.where(is_prompt, oap_ref[...], oas_ref[...])
    ol = jnp.where(is_prompt, olp_ref[...], ols_ref[...])
    na = _rms_norm(oa, ga_ref[...]).astype(BF16)
    nl = _rms_norm(ol, gl_ref[...]).astype(BF16)
    m = jnp.dot(na, w_ref[0:ATT_W, :], preferred_element_type=F32)
    m = m + jnp.dot(nl, w_ref[ATT_W:D_MODEL, :], preferred_element_type=F32)
    x1 = _layer_norm(alpha * x_ref[...] + m, lg_ref[...], lb_ref[...])
    x1_ref[...] = x1
    _matrix_to_rows(x1s_ref, x1, x1.shape[0])

    xh = x1.astype(BF16)
    xl = (x1 - xh.astype(F32)).astype(BF16)
    logits = jnp.dot(xh, wrh_ref[...], preferred_element_type=F32)
    logits = logits + jnp.dot(xl, wrh_ref[...], preferred_element_type=F32)
    logits = logits + jnp.dot(xh, wrl_ref[...], preferred_element_type=F32)
    scores = jax.nn.sigmoid(logits)
    tm = scores.shape[0]
    lane = lax.broadcasted_iota(I32, (tm, LANES), 1)
    lane_f = lane.astype(F32)
    sel = jnp.where(lane < N_EXPERTS, scores + rb_ref[...], -jnp.inf)
    idx_out = jnp.zeros((tm, LANES), F32)
    gate_out = jnp.zeros((tm, LANES), F32)
    gsum = jnp.zeros((tm, 1), F32)
    for k in range(TOP_K):
        mx = jnp.max(sel, axis=1, keepdims=True)
        ik = jnp.min(jnp.where(sel == mx, lane_f, float(LANES)), axis=1, keepdims=True)
        hit = lane_f == ik
        gk = jnp.sum(jnp.where(hit, scores, 0.0), axis=1, keepdims=True)
        sel = jnp.where(hit, -jnp.inf, sel)
        idx_out = jnp.where(lane == k, ik, idx_out)
        gate_out = jnp.where(lane == k, gk, gate_out)
        gsum = gsum + gk
    idx_ref[...] = idx_out.astype(I32)
    gate_ref[...] = gate_out / gsum * ROUTE_SCALE


def _outproj(oa_p, oa_s, ol_p, ol_s, x, lw, alpha):
    T = x.shape[0]
    n_p = oa_p.shape[0] // TM
    row = lambda w: pl.BlockSpec((TM, w), lambda i: (i, 0))
    prompt = lambda w: pl.BlockSpec((TM, w), lambda i: (jnp.minimum(i, n_p - 1), 0))
    sample = lambda w: pl.BlockSpec((TM, w), lambda i: (jnp.maximum(i - n_p, 0), 0))
    vec = lambda n: _resident((1, n))
    return pl.pallas_call(
        functools.partial(_outproj_kernel, alpha=alpha, n_prompt_blocks=n_p),
        grid=(T // TM,),
        in_specs=[prompt(ATT_W), sample(ATT_W), prompt(LRU_W), sample(LRU_W), row(D_MODEL),
                  _resident((D_MODEL, D_MODEL)),
                  vec(ATT_W), vec(LRU_W), vec(D_MODEL), vec(D_MODEL),
                  _resident((D_MODEL, LANES)), _resident((D_MODEL, LANES)), vec(LANES)],
        out_specs=[row(D_MODEL), pl.BlockSpec((TM * SLAB_ROWS, LANES), lambda i: (i, 0)),
                   row(LANES), row(LANES)],
        out_shape=[jax.ShapeDtypeStruct((T, D_MODEL), F32),
                   jax.ShapeDtypeStruct((T * SLAB_ROWS, LANES), U32),
                   jax.ShapeDtypeStruct((T, LANES), I32),
                   jax.ShapeDtypeStruct((T, LANES), F32)],
        compiler_params=_params(),
        name="outproj_ln_router",
    )(oa_p, oa_s, ol_p, ol_s, x, lw["w_out"], lw["g_att"], lw["g_lru"], lw["ln1_g"], lw["ln1_b"],
      lw["wr_hi"], lw["wr_lo"], lw["b_router"])


def _rank_kernel(idx_ref, dest_ref, cnt_ref, carry_ref, start_ref):
    phase = pl.program_id(0)
    i = pl.program_id(1)

    @pl.when((phase == 0) & (i == 0))
    def _():
        carry_ref[...] = jnp.zeros_like(carry_ref)

    tm = idx_ref.shape[0]
    idx = idx_ref[...]
    lane = lax.broadcasted_iota(I32, (tm, LANES), 1)
    onehot = jnp.zeros((tm, LANES), F32)
    for k in range(TOP_K):
        onehot = onehot + (lane == idx[:, k:k + 1]).astype(F32)
    colsum = jnp.sum(onehot, axis=0, keepdims=True)

    @pl.when(phase == 0)
    def _():
        carry_ref[...] = carry_ref[...] + colsum
        dest_ref[...] = jnp.zeros_like(dest_ref)

    @pl.when((phase == 1) & (i == 0))
    def _():
        counts = carry_ref[...]
        cnt_ref[...] = counts.astype(I32)
        padded = jnp.floor((counts + (MOE_BLK - 1)) * (1.0 / MOE_BLK)) * MOE_BLK
        r = lax.broadcasted_iota(I32, (LANES, LANES), 0)
        c = lax.broadcasted_iota(I32, (LANES, LANES), 1)
        before = (r < c).astype(BF16)
        start = jnp.zeros((SUBLANES, LANES), F32)
        for term in _split3(jnp.broadcast_to(padded, (SUBLANES, LANES))):
            start = start + jnp.dot(term, before, preferred_element_type=F32)
        start_ref[...] = start[0:1, :]
        carry_ref[...] = jnp.zeros_like(carry_ref)

    @pl.when(phase == 1)
    def _():
        r = lax.broadcasted_iota(I32, (tm, tm), 0)
        c = lax.broadcasted_iota(I32, (tm, tm), 1)
        below = (r > c).astype(BF16)
        entry = jnp.dot(below, onehot.astype(BF16), preferred_element_type=F32)
        entry = entry + carry_ref[...] + start_ref[...]
        out = jnp.zeros((tm, LANES), F32)
        for k in range(TOP_K):
            ek = jnp.sum(jnp.where(lane == idx[:, k:k + 1], entry, 0.0), axis=1, keepdims=True)
            out = jnp.where(lane == k, ek, out)
        dest_ref[...] = out.astype(I32)
        carry_ref[...] = carry_ref[...] + colsum


def _rank(idx):
    T = idx.shape[0]
    row = pl.BlockSpec((TM, LANES), lambda p, i: (i, 0))
    return pl.pallas_call(
        _rank_kernel,
        grid=(2, T // TM),
        in_specs=[row],
        out_specs=[pl.BlockSpec((TM, LANES), lambda p, i: (i * p, 0)),
                   pl.BlockSpec((1, LANES), lambda p, i: (0, 0))],
        out_shape=[jax.ShapeDtypeStruct((T, LANES), I32), jax.ShapeDtypeStruct((1, LANES), I32)],
        scratch_shapes=[pltpu.VMEM((1, LANES), F32), pltpu.VMEM((1, LANES), F32)],
        compiler_params=_params(2),
        name="route_rank",
    )(idx)


def _slot_table_kernel(dest_hbm, init_hbm, tab_hbm, tab, dchunk, sem):
    i = pl.program_id(0)
    n = ROUTE_CHUNK

    @pl.when(i == 0)
    def _():
        cp = pltpu.make_async_copy(init_hbm, tab, sem.at[0])
        cp.start()
        cp.wait()

    off = pl.multiple_of(i * n, n)
    cp = pltpu.make_async_copy(dest_hbm.at[pl.ds(off, n)], dchunk, sem.at[1])
    cp.start()
    cp.wait()

    def group(j8, carry):
        for u in range(SUBLANES):
            j = j8 * SUBLANES + u
            tab[dchunk[j]] = off + j
        return carry

    lax.fori_loop(0, n // SUBLANES, group, 0)

    @pl.when(i == pl.num_programs(0) - 1)
    def _():
        cp = pltpu.make_async_copy(tab, tab_hbm, sem.at[0])
        cp.start()
        cp.wait()


def _slot_table(dest_flat, n_slots):
    n = dest_flat.shape[0]
    n_tab = n_slots
    sentinel = jnp.zeros((n_tab,), I32)
    any_spec = pl.BlockSpec(memory_space=pl.ANY)
    return pl.pallas_call(
        _slot_table_kernel,
        grid=(n // ROUTE_CHUNK,),
        in_specs=[any_spec, any_spec],
        out_specs=any_spec,
        out_shape=jax.ShapeDtypeStruct((n_tab,), I32),
        scratch_shapes=[pltpu.SMEM((n_tab,), I32), pltpu.SMEM((ROUTE_CHUNK,), I32),
                        pltpu.SemaphoreType.DMA((2,))],
        compiler_params=_params(),
        name="moe_slot_table",
    )(dest_flat, sentinel)


def _rows_to_matrix(ref, n):
    cols = []
    for r in range(SLAB_ROWS):
        w = ref[pl.ds(r, n, stride=SLAB_ROWS), :]
        cols.append(pltpu.bitcast(w << 16, F32))
        cols.append(pltpu.bitcast(w & jnp.uint32(0xFFFF0000), F32))
    return jnp.concatenate(cols, axis=1)


def _matrix_to_rows(ref, y, n):
    for r in range(SLAB_ROWS):
        lo = y[:, (2 * r) * LANES:(2 * r + 1) * LANES].astype(BF16).astype(F32)
        hi = y[:, (2 * r + 1) * LANES:(2 * r + 2) * LANES].astype(BF16).astype(F32)
        ref[pl.ds(r, n, stride=SLAB_ROWS), :] = (
            pltpu.bitcast(hi, U32) | (pltpu.bitcast(lo, U32) >> 16))


def _experts_kernel(first_ref, nu_ref, tab_ref, x1s_hbm, wg_ref, wu_ref, wd_ref, ys_hbm,
                    wg_s, wu_s, wd_s, xbuf0, xbuf1, ybuf0, ybuf1, gsem, ssem, *, n_blocks):
    e = pl.program_id(0)
    nu = nu_ref[0]
    blk_rows = MOE_BLK * SLAB_ROWS
    xbufs = (xbuf0, xbuf1)
    ybufs = (ybuf0, ybuf1)

    def gather_block(g, xb, sem):
        base = g * MOE_BLK
        for r in range(MOE_BLK):
            tok = tab_ref[base + r] >> TOP_K_SHIFT
            src = x1s_hbm.at[pl.ds(pl.multiple_of(tok * SLAB_ROWS, SLAB_ROWS), SLAB_ROWS)]
            pltpu.make_async_copy(src, xb.at[pl.ds(r * SLAB_ROWS, SLAB_ROWS)], sem).start()

    def wait_gather(xb, sem):
        pltpu.make_async_copy(x1s_hbm.at[pl.ds(0, blk_rows)], xb, sem).wait()

    def result_copy(g, yb, sem):
        dst = ys_hbm.at[pl.ds(pl.multiple_of(g * blk_rows, blk_rows), blk_rows)]
        return pltpu.make_async_copy(yb, dst, sem)

    @pl.when(e == 0)
    def _():
        gather_block(0, xbuf0, gsem.at[0])

    g0 = first_ref[e]
    g1 = first_ref[e + 1]

    @pl.when(g1 > g0)
    def _():
        wg_s[...] = wg_ref[...].astype(BF16)
        wu_s[...] = wu_ref[...].astype(BF16)
        wd_s[...] = wd_ref[...].astype(BF16)

    def step(g, p):
        xb_cur, xb_nxt = xbufs[p], xbufs[1 - p]
        yb_cur = ybufs[p]
        wait_gather(xb_cur, gsem.at[p])

        @pl.when(g >= 2)
        def _():
            result_copy(g - 2, yb_cur, ssem.at[p]).wait()

        gather_block(jnp.minimum(g + 1, nu - 1), xb_nxt, gsem.at[1 - p])
        x = _rows_to_matrix(xb_cur, MOE_BLK).astype(BF16)
        gt = jnp.dot(x, wg_s[...], preferred_element_type=F32)
        up = jnp.dot(x, wu_s[...], preferred_element_type=F32)
        h = (_silu(gt) * up).astype(BF16)
        y = jnp.dot(h, wd_s[...], preferred_element_type=F32)
        _matrix_to_rows(yb_cur, y, MOE_BLK)
        result_copy(g, yb_cur, ssem.at[p]).start()

        @pl.when(g == nu - 1)
        def _():
            @pl.when(g >= 1)
            def _():
                result_copy(g - 1, ybufs[1 - p], ssem.at[1 - p]).wait()

            result_copy(g, yb_cur, ssem.at[p]).wait()
            wait_gather(xb_nxt, gsem.at[1 - p])
            yb_cur[...] = jnp.zeros_like(yb_cur)

            def fill_start(gb, carry):
                result_copy(gb, yb_cur, ssem.at[p]).start()
                return carry

            def fill_wait(gb, carry):
                result_copy(gb, yb_cur, ssem.at[p]).wait()
                return carry

            lax.fori_loop(nu, n_blocks, fill_start, 0)
            lax.fori_loop(nu, n_blocks, fill_wait, 0)

    def block(g, carry):
        @pl.when(g % 2 == 0)
        def _():
            step(g, 0)

        @pl.when(g % 2 == 1)
        def _():
            step(g, 1)

        return carry

    lax.fori_loop(g0, g1, block, 0)


def _experts(first_blk, n_used, slot_tab, x1s, we_gate, we_up, we_down, layer):
    n_blocks = slot_tab.shape[0] // MOE_BLK
    blk_rows = MOE_BLK * SLAB_ROWS
    any_spec = pl.BlockSpec(memory_space=pl.ANY)
    w_in_spec = pl.BlockSpec((None, None, D_MODEL, D_EXPERT), lambda e, *_: (layer, e, 0, 0))
    grid_spec = pltpu.PrefetchScalarGridSpec(
        num_scalar_prefetch=3,
        grid=(N_EXPERTS,),
        in_specs=[any_spec, w_in_spec, w_in_spec,
                  pl.BlockSpec((None, None, D_EXPERT, D_MODEL), lambda e, *_: (layer, e, 0, 0))],
        out_specs=any_spec,
        scratch_shapes=[pltpu.VMEM((D_MODEL, D_EXPERT), BF16), pltpu.VMEM((D_MODEL, D_EXPERT), BF16),
                        pltpu.VMEM((D_EXPERT, D_MODEL), BF16),
                        pltpu.VMEM((blk_rows, LANES), U32), pltpu.VMEM((blk_rows, LANES), U32),
                        pltpu.VMEM((blk_rows, LANES), U32), pltpu.VMEM((blk_rows, LANES), U32),
                        pltpu.SemaphoreType.DMA((2,)), pltpu.SemaphoreType.DMA((2,))],
    )
    return pl.pallas_call(
        functools.partial(_experts_kernel, n_blocks=n_blocks),
        grid_spec=grid_spec,
        out_shape=jax.ShapeDtypeStruct((n_blocks * blk_rows, LANES), U32),
        compiler_params=_params(),
        name="moe_experts",
    )(first_blk, n_used, slot_tab, x1s, we_gate, we_up, we_down)


def _combine_kernel(dest_ref, ys_hbm, gate_ref, x_ref, wsg_ref, wsu_ref, wsd_ref, lg_ref, lb_ref,
                    o_ref, buf0, buf1, gsem, *, alpha):
    i = pl.program_id(0)
    tm = x_ref.shape[0]
    bufs = (buf0, buf1)

    def gather_rows(step, buf, sem):
        base = step * (tm * TOP_K)

        def per_token(t, carry):
            d0 = pl.multiple_of(t * SLAB_ROWS, SLAB_ROWS)
            for k in range(TOP_K):
                slot = dest_ref[base + t * TOP_K + k]
                src = ys_hbm.at[pl.ds(pl.multiple_of(slot * SLAB_ROWS, SLAB_ROWS), SLAB_ROWS)]
                pltpu.make_async_copy(src, buf.at[k, pl.ds(d0, SLAB_ROWS)], sem).start()
            return carry

        lax.fori_loop(0, tm, per_token, 0)

    @pl.when(i == 0)
    def _():
        gather_rows(0, buf0, gsem.at[0])

    def step(p):
        @pl.when(i + 1 < pl.num_programs(0))
        def _():
            gather_rows(i + 1, bufs[1 - p], gsem.at[1 - p])

        x1 = x_ref[...]
        xb = x1.astype(BF16)
        hs = _silu(jnp.dot(xb, wsg_ref[...], preferred_element_type=F32))
        hs = (hs * jnp.dot(xb, wsu_ref[...], preferred_element_type=F32)).astype(BF16)
        acc = jnp.dot(hs, wsd_ref[...], preferred_element_type=F32)
        for k in range(TOP_K):
            pltpu.make_async_copy(ys_hbm.at[pl.ds(0, tm * SLAB_ROWS)], bufs[p].at[k],
                                  gsem.at[p]).wait()
        gates = gate_ref[...]
        for k in range(TOP_K):
            acc = acc + gates[:, k:k + 1] * _rows_to_matrix(bufs[p].at[k], tm)
        o_ref[...] = _layer_norm(alpha * x1 + acc, lg_ref[...], lb_ref[...])

    @pl.when(i % 2 == 0)
    def _():
        step(0)

    @pl.when(i % 2 == 1)
    def _():
        step(1)


def _combine(dest_flat, ys, gates, x1, lw, alpha):
    T = x1.shape[0]
    tm = TM_COMBINE
    row = lambda w: pl.BlockSpec((tm, w), lambda i, *_: (i, 0))
    res = _resident
    grid_spec = pltpu.PrefetchScalarGridSpec(
        num_scalar_prefetch=1,
        grid=(T // tm,),
        in_specs=[pl.BlockSpec(memory_space=pl.ANY), row(LANES), row(D_MODEL),
                  res((D_MODEL, D_SHARED)), res((D_MODEL, D_SHARED)), res((D_SHARED, D_MODEL)),
                  res((1, D_MODEL)), res((1, D_MODEL))],
        out_specs=row(D_MODEL),
        scratch_shapes=[pltpu.VMEM((TOP_K, tm * SLAB_ROWS, LANES), U32),
                        pltpu.VMEM((TOP_K, tm * SLAB_ROWS, LANES), U32),
                        pltpu.SemaphoreType.DMA((2,))],
    )
    return pl.pallas_call(
        functools.partial(_combine_kernel, alpha=alpha),
        grid_spec=grid_spec,
        out_shape=jax.ShapeDtypeStruct((T, D_MODEL), F32),
        compiler_params=_params(),
        name="moe_combine",
    )(dest_flat, ys, gates, x1, lw["ws_gate"], lw["ws_up"], lw["ws_down"], lw["ln2_g"],
      lw["ln2_b"])


def _block_tables(counts):
    padded = (counts + MOE_BLK - 1) // MOE_BLK * MOE_BLK
    pad_end = jnp.cumsum(padded)
    pad_start = pad_end - padded
    first_blk = jnp.concatenate([pad_start, pad_end[-1:]]) // MOE_BLK
    n_used = pad_end[-1:] // MOE_BLK
    return first_blk.astype(I32), n_used.astype(I32)


def _moe(x1, x1s, idx, gates, lw, we_gate, we_up, we_down, layer, alpha):
    T = x1.shape[0]
    n_blocks = T * TOP_K // MOE_BLK + N_EXPERTS
    dest, cnt = _rank(idx)
    first_blk, n_used = _block_tables(cnt[0, :N_EXPERTS])
    dest_flat = dest[:, :TOP_K].reshape(-1)
    slot_tab = _slot_table(dest_flat, n_blocks * MOE_BLK)
    ys = _experts(first_blk, n_used, slot_tab, x1s, we_gate, we_up, we_down, layer)
    return _combine(dest_flat, ys, gates, x1, lw, alpha)


def _prep_layer(l, w_in, b_f, w_out, g_att, g_lru, conv_w, conv_b, w_r, b_r, w_i, b_i, lru_lambda,
                ln1_g, ln1_b, w_router, b_router, ws_gate, ws_up, ws_down, ln2_g, ln2_b):
    qkv_end = 3 * ATT_W
    wl = w_in[l]
    w_pad = jnp.concatenate(
        [wl[:, :qkv_end], wl[:, qkv_end + N_HEADS:],
         jnp.pad(wl[:, qkv_end:qkv_end + N_HEADS], ((0, 0), (0, LANES - N_HEADS)))],
        axis=1).astype(BF16)
    wr = jnp.pad(w_router[l], ((0, 0), (0, LANES - N_EXPERTS)))
    wr_hi = wr.astype(BF16)
    return dict(
        w_pad=w_pad,
        bf_pad=jnp.pad(b_f[l], (0, LANES - N_HEADS)).reshape(1, LANES),
        w_out=w_out[l].astype(BF16),
        g_att=g_att[l].reshape(1, ATT_W), g_lru=g_lru[l].reshape(1, LRU_W),
        conv_w=conv_w[l], conv_b=conv_b[l].reshape(1, LRU_W),
        w_r=w_r[l].astype(BF16), b_r=b_r[l].reshape(1, LRU_W),
        w_i=w_i[l].astype(BF16), b_i=b_i[l].reshape(1, LRU_W),
        lam=lru_lambda[l].reshape(1, LRU_W),
        ln1_g=ln1_g[l].reshape(1, D_MODEL), ln1_b=ln1_b[l].reshape(1, D_MODEL),
        wr_hi=wr_hi, wr_lo=(wr - wr_hi.astype(F32)).astype(BF16),
        b_router=jnp.pad(b_router[l], (0, LANES - N_EXPERTS)).reshape(1, LANES),
        ws_gate=ws_gate[l].astype(BF16), ws_up=ws_up[l].astype(BF16),
        ws_down=ws_down[l].astype(BF16),
        ln2_g=ln2_g[l].reshape(1, D_MODEL), ln2_b=ln2_b[l].reshape(1, D_MODEL),
    )


def kernel(x_prompt, x_sample, cache_k, cache_v, cache_logf, state_conv, state_lru, ln_in_g, ln_in_b, w_in, b_f, w_out, g_att, g_lru, conv_w, conv_b, w_r, b_r, w_i, b_i, lru_lambda, ln1_g, ln1_b, w_router, b_router, we_gate, we_up, we_down, ws_gate, ws_up, ws_down, ln2_g, ln2_b):
    B, S, _ = x_prompt.shape
    Bd, Tn, _ = x_sample.shape
    depth = w_in.shape[0]
    P = cache_k.shape[2]
    Tp = B * S
    Ts = Bd * Tn
    alpha = (2 * depth) ** 0.25

    x = jnp.concatenate([x_prompt.reshape(Tp, D_MODEL), x_sample.reshape(Ts, D_MODEL)], axis=0)
    ln_g = ln_in_g.reshape(1, D_MODEL)
    ln_b = ln_in_b.reshape(1, D_MODEL)
    zero_conv = jnp.zeros((B, CONV_W - 1, LRU_W), F32)
    zero_h = jnp.zeros((B, 1, LRU_W), F32)

    ks, vs, lfs, p_conv, p_lru, s_conv, s_lru = [], [], [], [], [], [], []
    for l in range(depth):
        lw = _prep_layer(l, w_in, b_f, w_out, g_att, g_lru, conv_w, conv_b, w_r, b_r, w_i, b_i,
                         lru_lambda, ln1_g, ln1_b, w_router, b_router, ws_gate, ws_up, ws_down,
                         ln2_g, ln2_b)
        outs = _inproj(x, ln_g, ln_b, lw["w_pad"], lw["bf_pad"], apply_ln=(l == 0))
        if l == 0:
            x, outs = outs[0], outs[1:]
        qb, k, v, kb, vb, xr, xg, lf = outs

        c_p = _cumsum(lf[:Tp].reshape(B, S, LANES))
        ct_p = c_p[:, :, :N_HEADS].transpose(0, 2, 1).reshape(B, N_HEADS, S // ATT_TK, ATT_TK)
        lf_cache = jnp.pad(cache_logf[l], ((0, 0), (0, 0), (0, LANES - N_HEADS)))
        c_s = _cumsum(jnp.concatenate([lf_cache, lf[Tp:].reshape(Bd, Tn, LANES)], axis=1))
        ct_s = c_s[:, :, :N_HEADS].transpose(0, 2, 1)

        oa_p = _attn_prompt(qb, kb, vb, c_p, ct_p, B, S)
        oa_s = _attn_sample(qb, kb, vb, cache_k.reshape(depth, Bd, P * N_HEADS, HEAD_DIM),
                            cache_v.reshape(depth, Bd, P * N_HEADS, HEAD_DIM), l, c_s, ct_s,
                            Tp, Bd, P, Tn)

        ol_p, pc, ph = _lru(xr, xg, zero_conv, zero_h, lw, 0, B, S, LRU_TS)
        ol_s, sc, sh = _lru(xr, xg, state_conv[l], state_lru[l].reshape(Bd, 1, LRU_W), lw,
                            Tp, Bd, Tn, Tn)

        x1, x1s, idx, gates = _outproj(oa_p, oa_s, ol_p, ol_s, x, lw, alpha)
        x = _moe(x1, x1s, idx, gates, lw, we_gate, we_up, we_down, l, alpha)

        ks.append(k)
        vs.append(v)
        lfs.append(lf[:, :N_HEADS])
        p_conv.append(pc)
        p_lru.append(ph.reshape(B, LRU_W))
        s_conv.append(sc)
        s_lru.append(sh.reshape(Bd, LRU_W))

    def split(ts, tail):
        p = jnp.stack([t[:Tp].reshape((B, S) + tail) for t in ts])
        s = jnp.stack([t[Tp:].reshape((Bd, Tn) + tail) for t in ts])
        return p, s

    p_k, s_k = split(ks, (N_HEADS, HEAD_DIM))
    p_v, s_v = split(vs, (N_HEADS, HEAD_DIM))
    p_logf, s_logf = split(lfs, (N_HEADS,))
    y_prompt = x[:Tp].reshape(B, S, D_MODEL)
    y_sample = x[Tp:].reshape(Bd, Tn, D_MODEL)
    return (y_prompt, y_sample, p_k, p_v, p_logf, jnp.stack(p_conv), jnp.stack(p_lru),
            s_k, s_v, s_logf, jnp.stack(s_conv), jnp.stack(s_lru))
```

```python
import functools

import jax
import jax.numpy as jnp
from jax import lax
from jax.experimental import pallas as pl
from jax.experimental.pallas import tpu as pltpu

F32 = jnp.float32
BF16 = jnp.bfloat16
I32 = jnp.int32
U32 = jnp.uint32

D_MODEL = 2048
N_HEADS = 8
HEAD_DIM = 128
ATT_W = N_HEADS * HEAD_DIM
LRU_W = D_MODEL - ATT_W
LRU_BLOCKS = 8
LRU_BW = LRU_W // LRU_BLOCKS
CONV_W = 4
LRU_C = 8.0
N_EXPERTS = 64
TOP_K = 8
TOP_K_SHIFT = TOP_K.bit_length() - 1
assert 1 << TOP_K_SHIFT == TOP_K
D_EXPERT = 512
D_SHARED = 512
ROUTE_SCALE = 2.5
ATT_SCALE = HEAD_DIM ** -0.5
LN_EPS = 1e-5
RMS_EPS = 1e-6

LANES = 128
SUBLANES = 8
SLAB_ROWS = D_MODEL // (2 * LANES)
VMEM_LIMIT = 56 * 1024 * 1024

TM = 256
MOE_BLK = 256
TM_COMBINE = 128
ROUTE_CHUNK = 2048
ATT_TQ = 512
ATT_TK = ATT_TQ
ATT_HEAD_GROUP = 2
LRU_TS = 256
CUM_CHUNK = 128

C_Q, C_K, C_V, C_XR, C_XG, C_F, C_END = 0, 1024, 2048, 3072, 4096, 5120, 5248


def _params(n_axes=1):
    return pltpu.CompilerParams(dimension_semantics=("arbitrary",) * n_axes,
                                vmem_limit_bytes=VMEM_LIMIT)


def _resident(shape):
    nd = len(shape)
    return pl.BlockSpec(shape, lambda *_: (0,) * nd, pipeline_mode=pl.Buffered(1))


def _layer_norm(x, g, b):
    mu = jnp.mean(x, axis=-1, keepdims=True)
    xc = x - mu
    var = jnp.mean(xc * xc, axis=-1, keepdims=True)
    return xc * lax.rsqrt(var + LN_EPS) * g + b


def _rms_norm(x, g):
    return x * lax.rsqrt(jnp.mean(x * x, axis=-1, keepdims=True) + RMS_EPS) * g


def _silu(x):
    return x * jax.nn.sigmoid(x)


def _gelu_tanh(x):
    return 0.5 * x * (1.0 + jnp.tanh(0.7978845608028654 * (x + 0.044715 * (x * x * x))))


def _split3(x):
    hi = x.astype(BF16)
    r1 = x - hi.astype(F32)
    mid = r1.astype(BF16)
    lo = (r1 - mid.astype(F32)).astype(BF16)
    return hi, mid, lo


def _inproj_kernel(apply_ln, x_ref, g_ref, b_ref, w_ref, bf_ref, *outs):
    if apply_ln:
        x0_ref, outs = outs[0], outs[1:]
    q_ref, k_ref, v_ref, kb_ref, vb_ref, xr_ref, xg_ref, lf_ref = outs
    x = x_ref[...]
    if apply_ln:
        x = _layer_norm(x, g_ref[...], b_ref[...])
        x0_ref[...] = x
    h = x.astype(BF16)

    def mm(c0, c1):
        return jnp.dot(h, w_ref[:, c0:c1], preferred_element_type=F32)

    q_ref[...] = mm(C_Q, C_K).astype(BF16)
    k = mm(C_K, C_V)
    k_ref[...] = k
    kb_ref[...] = k.astype(BF16)
    v = mm(C_V, C_XR)
    v_ref[...] = v
    vb_ref[...] = v.astype(BF16)
    xr_ref[...] = mm(C_XR, C_XG)
    xg_ref[...] = mm(C_XG, C_F)
    f = mm(C_F, C_END) + bf_ref[...]
    lf_ref[...] = jnp.minimum(f, 0.0) - jnp.log1p(jnp.exp(-jnp.abs(f)))


def _inproj(x, ln_g, ln_b, w_pad, bf_pad, apply_ln):
    T = x.shape[0]
    row = lambda w: pl.BlockSpec((TM, w), lambda i: (i, 0))
    out_shape = [
        jax.ShapeDtypeStruct((T, ATT_W), BF16),
        jax.ShapeDtypeStruct((T, ATT_W), F32),
        jax.ShapeDtypeStruct((T, ATT_W), F32),
        jax.ShapeDtypeStruct((T, ATT_W), BF16),
        jax.ShapeDtypeStruct((T, ATT_W), BF16),
        jax.ShapeDtypeStruct((T, LRU_W), F32),
        jax.ShapeDtypeStruct((T, LRU_W), F32),
        jax.ShapeDtypeStruct((T, LANES), F32),
    ]
    out_specs = [row(ATT_W)] * 5 + [row(LRU_W)] * 2 + [row(LANES)]
    if apply_ln:
        out_shape = [jax.ShapeDtypeStruct((T, D_MODEL), F32)] + out_shape
        out_specs = [row(D_MODEL)] + out_specs
    return pl.pallas_call(
        functools.partial(_inproj_kernel, apply_ln),
        grid=(T // TM,),
        in_specs=[row(D_MODEL), _resident((1, D_MODEL)), _resident((1, D_MODEL)),
                  _resident((D_MODEL, C_END)), _resident((1, LANES))],
        out_specs=out_specs,
        out_shape=out_shape,
        compiler_params=_params(),
        name="inproj",
    )(x, ln_g, ln_b, w_pad, bf_pad)


def _cumsum_kernel(lf_ref, c_ref, *, S):
    carry = jnp.zeros((1, LANES), F32)
    for c0 in range(0, S, CUM_CHUNK):
        n = min(CUM_CHUNK, S - c0)
        r = lax.broadcasted_iota(I32, (n, n), 0)
        c = lax.broadcasted_iota(I32, (n, n), 1)
        tril = (r >= c).astype(BF16)
        hi, mid, lo = _split3(lf_ref[c0:c0 + n, :])
        acc = jnp.dot(tril, hi, preferred_element_type=F32)
        acc = acc + jnp.dot(tril, mid, preferred_element_type=F32)
        acc = acc + jnp.dot(tril, lo, preferred_element_type=F32)
        acc = acc + carry
        c_ref[c0:c0 + n, :] = acc
        carry = acc[n - 1:n, :]


def _cumsum(lf):
    B, S, _ = lf.shape
    spec = pl.BlockSpec((None, S, LANES), lambda b: (b, 0, 0))
    return pl.pallas_call(
        functools.partial(_cumsum_kernel, S=S),
        grid=(B,),
        in_specs=[spec],
        out_specs=spec,
        out_shape=jax.ShapeDtypeStruct((B, S, LANES), F32),
        compiler_params=_params(),
        name="cumsum_logf",
    )(lf)


def _attn_prompt_kernel(q_ref, k_ref, v_ref, c_ref, ct_ref, o_ref, *, S):
    nq = S // ATT_TQ
    for h0 in range(0, N_HEADS, ATT_HEAD_GROUP):
        heads = range(h0, h0 + ATT_HEAD_GROUP)

        def q_body(qi, carry, heads=heads):
            q0 = pl.multiple_of(qi * ATT_TQ, ATT_TQ)
            hsl = [slice(h * HEAD_DIM, (h + 1) * HEAD_DIM) for h in heads]
            qs = [q_ref[pl.ds(q0, ATT_TQ), hs] for hs in hsl]
            cqs = [c_ref[pl.ds(q0, ATT_TQ), h:h + 1] for h in heads]

            def kv_step(kj, sts, diagonal):
                k0 = pl.multiple_of(kj * ATT_TK, ATT_TK)
                out = []
                for h, hs, q, cq, (m, l, acc) in zip(heads, hsl, qs, cqs, sts):
                    k = k_ref[pl.ds(k0, ATT_TK), hs]
                    v = v_ref[pl.ds(k0, ATT_TK), hs]
                    ck = ct_ref[h, pl.ds(kj, 1), :]
                    s = lax.dot_general(q, k, (((1,), (1,)), ((), ())),
                                        preferred_element_type=F32) * ATT_SCALE
                    s = s + cq - ck
                    if diagonal:
                        row = lax.broadcasted_iota(I32, (ATT_TQ, ATT_TK), 0)
                        col = lax.broadcasted_iota(I32, (ATT_TQ, ATT_TK), 1)
                        s = jnp.where(row >= col, s, -jnp.inf)
                    m_new = jnp.maximum(m, jnp.max(s, axis=1, keepdims=True))
                    alpha = jnp.exp(m - m_new)
                    p = jnp.exp(s - m_new)
                    l = alpha * l + jnp.sum(p, axis=1, keepdims=True)
                    acc = alpha * acc + jnp.dot(p.astype(BF16), v, preferred_element_type=F32)
                    out.append((m_new, l, acc))
                return tuple(out)

            init = tuple((jnp.full((ATT_TQ, 1), -jnp.inf, F32), jnp.zeros((ATT_TQ, 1), F32),
                          jnp.zeros((ATT_TQ, HEAD_DIM), F32)) for _ in heads)
            sts = lax.fori_loop(0, qi, lambda kj, sts: kv_step(kj, sts, False), init)
            sts = kv_step(qi, sts, True)
            for hs, (m, l, acc) in zip(hsl, sts):
                o_ref[pl.ds(q0, ATT_TQ), hs] = acc / l
            return carry

        lax.fori_loop(0, nq, q_body, 0)


def _attn_prompt(qb, kb, vb, c, ct, B, S):
    T = B * S
    seq = pl.BlockSpec((S, ATT_W), lambda b: (b, 0))
    return pl.pallas_call(
        functools.partial(_attn_prompt_kernel, S=S),
        grid=(B,),
        in_specs=[seq, seq, seq,
                  pl.BlockSpec((None, S, LANES), lambda b: (b, 0, 0)),
                  pl.BlockSpec((None, N_HEADS, S // ATT_TK, ATT_TK), lambda b: (b, 0, 0, 0))],
        out_specs=seq,
        out_shape=jax.ShapeDtypeStruct((T, ATT_W), F32),
        compiler_params=_params(),
        name="fox_prompt",
    )(qb, kb, vb, c, ct)


def _attn_sample_kernel(q_ref, kn_ref, vn_ref, ck_ref, cv_ref, c_ref, ct_ref, o_ref, *, P, Tn):
    r = lax.broadcasted_iota(I32, (Tn, Tn), 0)
    c = lax.broadcasted_iota(I32, (Tn, Tn), 1)
    nt = (((1,), (1,)), ((), ()))
    for h in range(N_HEADS):
        hs = slice(h * HEAD_DIM, (h + 1) * HEAD_DIM)
        q = q_ref[:, hs]
        kc = ck_ref[pl.ds(h, P, stride=N_HEADS), :].astype(BF16)
        vc = cv_ref[pl.ds(h, P, stride=N_HEADS), :].astype(BF16)
        cq = c_ref[P:P + Tn, h:h + 1]
        ckr = ct_ref[h:h + 1, :]
        s_c = lax.dot_general(q, kc, nt, preferred_element_type=F32) * ATT_SCALE
        s_c = s_c + cq - ckr[:, :P]
        s_n = lax.dot_general(q, kn_ref[:, hs], nt, preferred_element_type=F32) * ATT_SCALE
        s_n = jnp.where(r >= c, s_n + cq - ckr[:, P:P + Tn], -jnp.inf)
        m = jnp.maximum(jnp.max(s_c, axis=1, keepdims=True), jnp.max(s_n, axis=1, keepdims=True))
        p_c = jnp.exp(s_c - m)
        p_n = jnp.exp(s_n - m)
        l = jnp.sum(p_c, axis=1, keepdims=True) + jnp.sum(p_n, axis=1, keepdims=True)
        o = jnp.dot(p_c.astype(BF16), vc, preferred_element_type=F32)
        o = o + jnp.dot(p_n.astype(BF16), vn_ref[:, hs], preferred_element_type=F32)
        o_ref[:, hs] = o / l


def _attn_sample(qb, kb, vb, cache_k, cache_v, layer, c, ct, row0, Bd, P, Tn):
    blk0 = row0 // Tn
    new = pl.BlockSpec((Tn, ATT_W), lambda b: (blk0 + b, 0))
    cache = pl.BlockSpec((None, None, P * N_HEADS, HEAD_DIM), lambda b: (layer, b, 0, 0))
    return pl.pallas_call(
        functools.partial(_attn_sample_kernel, P=P, Tn=Tn),
        grid=(Bd,),
        in_specs=[new, new, new, cache, cache,
                  pl.BlockSpec((None, P + Tn, LANES), lambda b: (b, 0, 0)),
                  pl.BlockSpec((None, N_HEADS, P + Tn), lambda b: (b, 0, 0))],
        out_specs=pl.BlockSpec((Tn, ATT_W), lambda b: (b, 0)),
        out_shape=jax.ShapeDtypeStruct((Bd * Tn, ATT_W), F32),
        compiler_params=_params(),
        name="fox_sample",
    )(qb, kb, vb, cache_k, cache_v, c, ct)


def _lru_kernel(xr_ref, xg_ref, cs_ref, h0_ref, cw_ref, cb_ref, wr_ref, br_ref, wi_ref, bi_ref,
                lam_ref, o_ref, cso_ref, hl_ref, xp_ref, a_ref, u_ref, hs_ref, h_ref, *, ts):
    s = pl.program_id(1)
    pad = SUBLANES
    nprev = CONV_W - 1

    @pl.when(s == 0)
    def _():
        xp_ref[pad - nprev:pad, :] = cs_ref[...]
        h_ref[...] = h0_ref[...]

    x = xr_ref[...]
    xp_ref[pad:pad + ts, :] = x
    xc = cb_ref[...] + cw_ref[nprev:nprev + 1, :] * x
    for i in range(nprev):
        xc = xc + cw_ref[i:i + 1, :] * xp_ref[pad - nprev + i:pad - nprev + i + ts, :]
    tail = xp_ref[pad + ts - nprev:pad + ts, :]
    xp_ref[pad - nprev:pad, :] = tail
    cso_ref[...] = tail

    xcb = xc.astype(BF16)

    def block_diag(w_ref, b_ref):
        parts = [jnp.dot(xcb[:, n * LRU_BW:(n + 1) * LRU_BW], w_ref[n], preferred_element_type=F32)
                 for n in range(LRU_BLOCKS)]
        return jnp.concatenate(parts, axis=1) + b_ref[...]

    r = jax.nn.sigmoid(block_diag(wr_ref, br_ref))
    g = jax.nn.sigmoid(block_diag(wi_ref, bi_ref))
    nl = -lam_ref[...]
    softplus = jnp.maximum(nl, 0.0) + jnp.log1p(jnp.exp(-jnp.abs(nl)))
    log_a = (-LRU_C) * r * softplus
    a = jnp.exp(log_a)
    u = jnp.sqrt(1.0 - a * a) * g * xc

    sub = lax.broadcasted_iota(I32, (ts, LRU_W), 0) % SUBLANES
    d = 1
    while d < SUBLANES:
        keep = sub >= d
        u = jnp.where(keep, u + a * pltpu.roll(u, d, axis=0), u)
        a = jnp.where(keep, a * pltpu.roll(a, d, axis=0), a)
        d *= 2
    a_ref[...] = a
    u_ref[...] = u

    def group(gi, hprev):
        r0 = pl.multiple_of(gi * SUBLANES, SUBLANES)
        hh = u_ref[pl.ds(r0, SUBLANES), :] + a_ref[pl.ds(r0, SUBLANES), :] * hprev
        hs_ref[pl.ds(r0, SUBLANES), :] = hh
        return hh[SUBLANES - 1:SUBLANES, :]

    hlast = lax.fori_loop(0, ts // SUBLANES, group, h_ref[...])
    h_ref[...] = hlast
    hl_ref[...] = hlast
    o_ref[...] = hs_ref[...] * _gelu_tanh(xg_ref[...])


def _lru(xr, xg, conv_state, h0, lw, row0, nb, S, ts):
    ns = S // ts
    blk0 = row0 // ts
    rows = pl.BlockSpec((ts, LRU_W), lambda b, s: (blk0 + b * ns + s, 0))
    vec = lambda n: _resident((n, LRU_W))
    per_seq = lambda n: pl.BlockSpec((None, n, LRU_W), lambda b, s: (b, 0, 0))
    in_specs = [rows, rows, per_seq(CONV_W - 1), per_seq(1), vec(CONV_W), vec(1),
                _resident((LRU_BLOCKS, LRU_BW, LRU_BW)), vec(1),
                _resident((LRU_BLOCKS, LRU_BW, LRU_BW)), vec(1), vec(1)]
    args = [xr, xg, conv_state, h0, lw["conv_w"], lw["conv_b"], lw["w_r"], lw["b_r"],
            lw["w_i"], lw["b_i"], lw["lam"]]
    return pl.pallas_call(
        functools.partial(_lru_kernel, ts=ts),
        grid=(nb, ns),
        in_specs=in_specs,
        out_specs=[pl.BlockSpec((ts, LRU_W), lambda b, s: (b * ns + s, 0)),
                   per_seq(CONV_W - 1), per_seq(1)],
        out_shape=[jax.ShapeDtypeStruct((nb * S, LRU_W), F32),
                   jax.ShapeDtypeStruct((nb, CONV_W - 1, LRU_W), F32),
                   jax.ShapeDtypeStruct((nb, 1, LRU_W), F32)],
        scratch_shapes=[pltpu.VMEM((SUBLANES + ts, LRU_W), F32),
                        pltpu.VMEM((ts, LRU_W), F32), pltpu.VMEM((ts, LRU_W), F32),
                        pltpu.VMEM((ts, LRU_W), F32), pltpu.VMEM((1, LRU_W), F32)],
        compiler_params=_params(2),
        name="rg_lru",
    )(*args)


def _outproj_kernel(oap_ref, oas_ref, olp_ref, ols_ref, x_ref, w_ref, ga_ref, gl_ref, lg_ref,
                    lb_ref, wrh_ref, wrl_ref, rb_ref, x1_ref, x1s_ref, idx_ref, gate_ref,
                    *, alpha, n_prompt_blocks):
    is_prompt = pl.program_id(0) < n_prompt_blocks
    oa = jnp.where(is_prompt, oap_ref[...], oas_ref[...])
    ol = jnp.where(is_prompt, olp_ref[...], ols_ref[...])
    na = _rms_norm(oa, ga_ref[...]).astype(BF16)
    nl = _rms_norm(ol, gl_ref[...]).astype(BF16)
    m = jnp.dot(na, w_ref[0:ATT_W, :], preferred_element_type=F32)
    m = m + jnp.dot(nl, w_ref[ATT_W:D_MODEL, :], preferred_element_type=F32)
    x1 = _layer_norm(alpha * x_ref[...] + m, lg_ref[...], lb_ref[...])
    x1_ref[...] = x1
    _matrix_to_rows(x1s_ref, x1, x1.shape[0])

    xh = x1.astype(BF16)
    xl = (x1 - xh.astype(F32)).astype(BF16)
    logits = jnp.dot(xh, wrh_ref[...], preferred_element_type=F32)
    logits = logits + jnp.dot(xl, wrh_ref[...], preferred_element_type=F32)
    logits = logits + jnp.dot(xh, wrl_ref[...], preferred_element_type=F32)
    scores = jax.nn.sigmoid(logits)
    tm = scores.shape[0]
    lane = lax.broadcasted_iota(I32, (tm, LANES), 1)
    lane_f = lane.astype(F32)
    sel = jnp.where(lane < N_EXPERTS, scores + rb_ref[...], -jnp.inf)
    idx_out = jnp.zeros((tm, LANES), F32)
    gate_out = jnp.zeros((tm, LANES), F32)
    gsum = jnp.zeros((tm, 1), F32)
    for k in range(TOP_K):
        mx = jnp.max(sel, axis=1, keepdims=True)
        ik = jnp.min(jnp.where(sel == mx, lane_f, float(LANES)), axis=1, keepdims=True)
        hit = lane_f == ik
        gk = jnp.sum(jnp.where(hit, scores, 0.0), axis=1, keepdims=True)
        sel = jnp.where(hit, -jnp.inf, sel)
        idx_out = jnp.where(lane == k, ik, idx_out)
        gate_out = jnp.where(lane == k, gk, gate_out)
        gsum = gsum + gk
    idx_ref[...] = idx_out.astype(I32)
    gate_ref[...] = gate_out / gsum * ROUTE_SCALE


def _outproj(oa_p, oa_s, ol_p, ol_s, x, lw, alpha):
    T = x.shape[0]
    n_p = oa_p.shape[0] // TM
    row = lambda w: pl.BlockSpec((TM, w), lambda i: (i, 0))
    prompt = lambda w: pl.BlockSpec((TM, w), lambda i: (jnp.minimum(i, n_p - 1), 0))
    sample = lambda w: pl.BlockSpec((TM, w), lambda i: (jnp.maximum(i - n_p, 0), 0))
    vec = lambda n: _resident((1, n))
    return pl.pallas_call(
        functools.partial(_outproj_kernel, alpha=alpha, n_prompt_blocks=n_p),
        grid=(T // TM,),
        in_specs=[prompt(ATT_W), sample(ATT_W), prompt(LRU_W), sample(LRU_W), row(D_MODEL),
                  _resident((D_MODEL, D_MODEL)),
                  vec(ATT_W), vec(LRU_W), vec(D_MODEL), vec(D_MODEL),
                  _resident((D_MODEL, LANES)), _resident((D_MODEL, LANES)), vec(LANES)],
        out_specs=[row(D_MODEL), pl.BlockSpec((TM * SLAB_ROWS, LANES), lambda i: (i, 0)),
                   row(LANES), row(LANES)],
        out_shape=[jax.ShapeDtypeStruct((T, D_MODEL), F32),
                   jax.ShapeDtypeStruct((T * SLAB_ROWS, LANES), U32),
                   jax.ShapeDtypeStruct((T, LANES), I32),
                   jax.ShapeDtypeStruct((T, LANES), F32)],
        compiler_params=_params(),
        name="outproj_ln_router",
    )(oa_p, oa_s, ol_p, ol_s, x, lw["w_out"], lw["g_att"], lw["g_lru"], lw["ln1_g"], lw["ln1_b"],
      lw["wr_hi"], lw["wr_lo"], lw["b_router"])


def _rank_kernel(idx_ref, dest_ref, cnt_ref, carry_ref, start_ref):
    phase = pl.program_id(0)
    i = pl.program_id(1)

    @pl.when((phase == 0) & (i == 0))
    def _():
        carry_ref[...] = jnp.zeros_like(carry_ref)

    tm = idx_ref.shape[0]
    idx = idx_ref[...]
    lane = lax.broadcasted_iota(I32, (tm, LANES), 1)
    onehot = jnp.zeros((tm, LANES), F32)
    for k in range(TOP_K):
        onehot = onehot + (lane == idx[:, k:k + 1]).astype(F32)
    colsum = jnp.sum(onehot, axis=0, keepdims=True)

    @pl.when(phase == 0)
    def _():
        carry_ref[...] = carry_ref[...] + colsum
        dest_ref[...] = jnp.zeros_like(dest_ref)

    @pl.when((phase == 1) & (i == 0))
    def _():
        counts = carry_ref[...]
        cnt_ref[...] = counts.astype(I32)
        padded = jnp.floor((counts + (MOE_BLK - 1)) * (1.0 / MOE_BLK)) * MOE_BLK
        r = lax.broadcasted_iota(I32, (LANES, LANES), 0)
        c = lax.broadcasted_iota(I32, (LANES, LANES), 1)
        before = (r < c).astype(BF16)
        start = jnp.zeros((SUBLANES, LANES), F32)
        for term in _split3(jnp.broadcast_to(padded, (SUBLANES, LANES))):
            start = start + jnp.dot(term, before, preferred_element_type=F32)
        start_ref[...] = start[0:1, :]
        carry_ref[...] = jnp.zeros_like(carry_ref)

    @pl.when(phase == 1)
    def _():
        r = lax.broadcasted_iota(I32, (tm, tm), 0)
        c = lax.broadcasted_iota(I32, (tm, tm), 1)
        below = (r > c).astype(BF16)
        entry = jnp.dot(below, onehot.astype(BF16), preferred_element_type=F32)
        entry = entry + carry_ref[...] + start_ref[...]
        out = jnp.zeros((tm, LANES), F32)
        for k in range(TOP_K):
            ek = jnp.sum(jnp.where(lane == idx[:, k:k + 1], entry, 0.0), axis=1, keepdims=True)
            out = jnp.where(lane == k, ek, out)
        dest_ref[...] = out.astype(I32)
        carry_ref[...] = carry_ref[...] + colsum


def _rank(idx):
    T = idx.shape[0]
    row = pl.BlockSpec((TM, LANES), lambda p, i: (i, 0))
    return pl.pallas_call(
        _rank_kernel,
        grid=(2, T // TM),
        in_specs=[row],
        out_specs=[pl.BlockSpec((TM, LANES), lambda p, i: (i * p, 0)),
                   pl.BlockSpec((1, LANES), lambda p, i: (0, 0))],
        out_shape=[jax.ShapeDtypeStruct((T, LANES), I32), jax.ShapeDtypeStruct((1, LANES), I32)],
        scratch_shapes=[pltpu.VMEM((1, LANES), F32), pltpu.VMEM((1, LANES), F32)],
        compiler_params=_params(2),
        name="route_rank",
    )(idx)


def _slot_table_kernel(dest_hbm, init_hbm, tab_hbm, tab, dchunk, sem):
    i = pl.program_id(0)
    n = ROUTE_CHUNK

    @pl.when(i == 0)
    def _():
        cp = pltpu.make_async_copy(init_hbm, tab, sem.at[0])
        cp.start()
        cp.wait()

    off = pl.multiple_of(i * n, n)
    cp = pltpu.make_async_copy(dest_hbm.at[pl.ds(off, n)], dchunk, sem.at[1])
    cp.start()
    cp.wait()

    def group(j8, carry):
        for u in range(SUBLANES):
            j = j8 * SUBLANES + u
            tab[dchunk[j]] = off + j
        return carry

    lax.fori_loop(0, n // SUBLANES, group, 0)

    @pl.when(i == pl.num_programs(0) - 1)
    def _():
        cp = pltpu.make_async_copy(tab, tab_hbm, sem.at[0])
        cp.start()
        cp.wait()


def _slot_table(dest_flat, n_slots):
    n = dest_flat.shape[0]
    n_tab = n_slots
    sentinel = jnp.zeros((n_tab,), I32)
    any_spec = pl.BlockSpec(memory_space=pl.ANY)
    return pl.pallas_call(
        _slot_table_kernel,
        grid=(n // ROUTE_CHUNK,),
        in_specs=[any_spec, any_spec],
        out_specs=any_spec,
        out_shape=jax.ShapeDtypeStruct((n_tab,), I32),
        scratch_shapes=[pltpu.SMEM((n_tab,), I32), pltpu.SMEM((ROUTE_CHUNK,), I32),
                        pltpu.SemaphoreType.DMA((2,))],
        compiler_params=_params(),
        name="moe_slot_table",
    )(dest_flat, sentinel)


def _rows_to_matrix(ref, n):
    cols = []
    for r in range(SLAB_ROWS):
        w = ref[pl.ds(r, n, stride=SLAB_ROWS), :]
        cols.append(pltpu.bitcast(w << 16, F32))
        cols.append(pltpu.bitcast(w & jnp.uint32(0xFFFF0000), F32))
    return jnp.concatenate(cols, axis=1)


def _matrix_to_rows(ref, y, n):
    for r in range(SLAB_ROWS):
        lo = y[:, (2 * r) * LANES:(2 * r + 1) * LANES].astype(BF16).astype(F32)
        hi = y[:, (2 * r + 1) * LANES:(2 * r + 2) * LANES].astype(BF16).astype(F32)
        ref[pl.ds(r, n, stride=SLAB_ROWS), :] = (
            pltpu.bitcast(hi, U32) | (pltpu.bitcast(lo, U32) >> 16))


def _experts_kernel(first_ref, nu_ref, tab_ref, x1s_hbm, wg_ref, wu_ref, wd_ref, ys_hbm,
                    wg_s, wu_s, wd_s, xbuf0, xbuf1, ybuf0, ybuf1, gsem, ssem, *, n_blocks):
    e = pl.program_id(0)
    nu = nu_ref[0]
    blk_rows = MOE_BLK * SLAB_ROWS
    xbufs = (xbuf0, xbuf1)
    ybufs = (ybuf0, ybuf1)

    def gather_block(g, xb, sem):
        base = g * MOE_BLK
        for r in range(MOE_BLK):
            tok = tab_ref[base + r] >> TOP_K_SHIFT
            src = x1s_hbm.at[pl.ds(pl.multiple_of(tok * SLAB_ROWS, SLAB_ROWS), SLAB_ROWS)]
            pltpu.make_async_copy(src, xb.at[pl.ds(r * SLAB_ROWS, SLAB_ROWS)], sem).start()

    def wait_gather(xb, sem):
        pltpu.make_async_copy(x1s_hbm.at[pl.ds(0, blk_rows)], xb, sem).wait()

    def result_copy(g, yb, sem):
        dst = ys_hbm.at[pl.ds(pl.multiple_of(g * blk_rows, blk_rows), blk_rows)]
        return pltpu.make_async_copy(yb, dst, sem)

    @pl.when(e == 0)
    def _():
        gather_block(0, xbuf0, gsem.at[0])

    g0 = first_ref[e]
    g1 = first_ref[e + 1]

    @pl.when(g1 > g0)
    def _():
        wg_s[...] = wg_ref[...].astype(BF16)
        wu_s[...] = wu_ref[...].astype(BF16)
        wd_s[...] = wd_ref[...].astype(BF16)

    def step(g, p):
        xb_cur, xb_nxt = xbufs[p], xbufs[1 - p]
        yb_cur = ybufs[p]
        wait_gather(xb_cur, gsem.at[p])

        @pl.when(g >= 2)
        def _():
            result_copy(g - 2, yb_cur, ssem.at[p]).wait()

        @pl.when(g + 1 < nu)
        def _():
            gather_block(g + 1, xb_nxt, gsem.at[1 - p])

        x = _rows_to_matrix(xb_cur, MOE_BLK).astype(BF16)
        gt = jnp.dot(x, wg_s[...], preferred_element_type=F32)
        up = jnp.dot(x, wu_s[...], preferred_element_type=F32)
        h = (_silu(gt) * up).astype(BF16)
        y = jnp.dot(h, wd_s[...], preferred_element_type=F32)
        _matrix_to_rows(yb_cur, y, MOE_BLK)
        result_copy(g, yb_cur, ssem.at[p]).start()

        @pl.when(g == nu - 1)
        def _():
            @pl.when(g >= 1)
            def _():
                result_copy(g - 1, ybufs[1 - p], ssem.at[1 - p]).wait()

            result_copy(g, yb_cur, ssem.at[p]).wait()
            yb_cur[...] = jnp.zeros_like(yb_cur)

            def fill_start(gb, carry):
                result_copy(gb, yb_cur, ssem.at[p]).start()
                return carry

            def fill_wait(gb, carry):
                result_copy(gb, yb_cur, ssem.at[p]).wait()
                return carry

            lax.fori_loop(nu, n_blocks, fill_start, 0)
            lax.fori_loop(nu, n_blocks, fill_wait, 0)

    def block(g, carry):
        @pl.when(g % 2 == 0)
        def _():
            step(g, 0)

        @pl.when(g % 2 == 1)
        def _():
            step(g, 1)

        return carry

    lax.fori_loop(g0, g1, block, 0)


def _experts(first_blk, n_used, slot_tab, x1s, we_gate, we_up, we_down, layer):
    n_blocks = slot_tab.shape[0] // MOE_BLK
    blk_rows = MOE_BLK * SLAB_ROWS
    any_spec = pl.BlockSpec(memory_space=pl.ANY)
    w_in_spec = pl.BlockSpec((None, None, D_MODEL, D_EXPERT), lambda e, *_: (layer, e, 0, 0))
    grid_spec = pltpu.PrefetchScalarGridSpec(
        num_scalar_prefetch=3,
        grid=(N_EXPERTS,),
        in_specs=[any_spec, w_in_spec, w_in_spec,
                  pl.BlockSpec((None, None, D_EXPERT, D_MODEL), lambda e, *_: (layer, e, 0, 0))],
        out_specs=any_spec,
        scratch_shapes=[pltpu.VMEM((D_MODEL, D_EXPERT), BF16), pltpu.VMEM((D_MODEL, D_EXPERT), BF16),
                        pltpu.VMEM((D_EXPERT, D_MODEL), BF16),
                        pltpu.VMEM((blk_rows, LANES), U32), pltpu.VMEM((blk_rows, LANES), U32),
                        pltpu.VMEM((blk_rows, LANES), U32), pltpu.VMEM((blk_rows, LANES), U32),
                        pltpu.SemaphoreType.DMA((2,)), pltpu.SemaphoreType.DMA((2,))],
    )
    return pl.pallas_call(
        functools.partial(_experts_kernel, n_blocks=n_blocks),
        grid_spec=grid_spec,
        out_shape=jax.ShapeDtypeStruct((n_blocks * blk_rows, LANES), U32),
        compiler_params=_params(),
        name="moe_experts",
    )(first_blk, n_used, slot_tab, x1s, we_gate, we_up, we_down)


def _combine_kernel(dest_ref, ys_hbm, gate_ref, x_ref, wsg_ref, wsu_ref, wsd_ref, lg_ref, lb_ref,
                    o_ref, buf0, buf1, gsem, *, alpha):
    i = pl.program_id(0)
    tm = x_ref.shape[0]
    bufs = (buf0, buf1)

    def gather_rows(step, buf, sem):
        base = step * (tm * TOP_K)

        def per_token(t, carry):
            d0 = pl.multiple_of(t * SLAB_ROWS, SLAB_ROWS)
            for k in range(TOP_K):
                slot = dest_ref[base + t * TOP_K + k]
                src = ys_hbm.at[pl.ds(pl.multiple_of(slot * SLAB_ROWS, SLAB_ROWS), SLAB_ROWS)]
                pltpu.make_async_copy(src, buf.at[k, pl.ds(d0, SLAB_ROWS)], sem).start()
            return carry

        lax.fori_loop(0, tm, per_token, 0)

    @pl.when(i == 0)
    def _():
        gather_rows(0, buf0, gsem.at[0])

    def step(p):
        @pl.when(i + 1 < pl.num_programs(0))
        def _():
            gather_rows(i + 1, bufs[1 - p], gsem.at[1 - p])

        x1 = x_ref[...]
        xb = x1.astype(BF16)
        hs = _silu(jnp.dot(xb, wsg_ref[...], preferred_element_type=F32))
        hs = (hs * jnp.dot(xb, wsu_ref[...], preferred_element_type=F32)).astype(BF16)
        acc = jnp.dot(hs, wsd_ref[...], preferred_element_type=F32)
        for k in range(TOP_K):
            pltpu.make_async_copy(ys_hbm.at[pl.ds(0, tm * SLAB_ROWS)], bufs[p].at[k],
                                  gsem.at[p]).wait()
        gates = gate_ref[...]
        for k in range(TOP_K):
            acc = acc + gates[:, k:k + 1] * _rows_to_matrix(bufs[p].at[k], tm)
        o_ref[...] = _layer_norm(alpha * x1 + acc, lg_ref[...], lb_ref[...])

    @pl.when(i % 2 == 0)
    def _():
        step(0)

    @pl.when(i % 2 == 1)
    def _():
        step(1)


def _combine(dest_flat, ys, gates, x1, lw, alpha):
    T = x1.shape[0]
    tm = TM_COMBINE
    row = lambda w: pl.BlockSpec((tm, w), lambda i, *_: (i, 0))
    res = _resident
    grid_spec = pltpu.PrefetchScalarGridSpec(
        num_scalar_prefetch=1,
        grid=(T // tm,),
        in_specs=[pl.BlockSpec(memory_space=pl.ANY), row(LANES), row(D_MODEL),
                  res((D_MODEL, D_SHARED)), res((D_MODEL, D_SHARED)), res((D_SHARED, D_MODEL)),
                  res((1, D_MODEL)), res((1, D_MODEL))],
        out_specs=row(D_MODEL),
        scratch_shapes=[pltpu.VMEM((TOP_K, tm * SLAB_ROWS, LANES), U32),
                        pltpu.VMEM((TOP_K, tm * SLAB_ROWS, LANES), U32),
                        pltpu.SemaphoreType.DMA((2,))],
    )
    return pl.pallas_call(
        functools.partial(_combine_kernel, alpha=alpha),
        grid_spec=grid_spec,
        out_shape=jax.ShapeDtypeStruct((T, D_MODEL), F32),
        compiler_params=_params(),
        name="moe_combine",
    )(dest_flat, ys, gates, x1, lw["ws_gate"], lw["ws_up"], lw["ws_down"], lw["ln2_g"],
      lw["ln2_b"])


def _block_tables(counts):
    padded = (counts + MOE_BLK - 1) // MOE_BLK * MOE_BLK
    pad_end = jnp.cumsum(padded)
    pad_start = pad_end - padded
    first_blk = jnp.concatenate([pad_start, pad_end[-1:]]) // MOE_BLK
    n_used = pad_end[-1:] // MOE_BLK
    return first_blk.astype(I32), n_used.astype(I32)


def _moe(x1, x1s, idx, gates, lw, we_gate, we_up, we_down, layer, alpha):
    T = x1.shape[0]
    n_blocks = T * TOP_K // MOE_BLK + N_EXPERTS
    dest, cnt = _rank(idx)
    first_blk, n_used = _block_tables(cnt[0, :N_EXPERTS])
    dest_flat = dest[:, :TOP_K].reshape(-1)
    slot_tab = _slot_table(dest_flat, n_blocks * MOE_BLK)
    ys = _experts(first_blk, n_used, slot_tab, x1s, we_gate, we_up, we_down, layer)
    return _combine(dest_flat, ys, gates, x1, lw, alpha)


def _prep_layer(l, w_in, b_f, w_out, g_att, g_lru, conv_w, conv_b, w_r, b_r, w_i, b_i, lru_lambda,
                ln1_g, ln1_b, w_router, b_router, ws_gate, ws_up, ws_down, ln2_g, ln2_b):
    qkv_end = 3 * ATT_W
    wl = w_in[l]
    w_pad = jnp.concatenate(
        [wl[:, :qkv_end], wl[:, qkv_end + N_HEADS:],
         jnp.pad(wl[:, qkv_end:qkv_end + N_HEADS], ((0, 0), (0, LANES - N_HEADS)))],
        axis=1).astype(BF16)
    wr = jnp.pad(w_router[l], ((0, 0), (0, LANES - N_EXPERTS)))
    wr_hi = wr.astype(BF16)
    return dict(
        w_pad=w_pad,
        bf_pad=jnp.pad(b_f[l], (0, LANES - N_HEADS)).reshape(1, LANES),
        w_out=w_out[l].astype(BF16),
        g_att=g_att[l].reshape(1, ATT_W), g_lru=g_lru[l].reshape(1, LRU_W),
        conv_w=conv_w[l], conv_b=conv_b[l].reshape(1, LRU_W),
        w_r=w_r[l].astype(BF16), b_r=b_r[l].reshape(1, LRU_W),
        w_i=w_i[l].astype(BF16), b_i=b_i[l].reshape(1, LRU_W),
        lam=lru_lambda[l].reshape(1, LRU_W),
        ln1_g=ln1_g[l].reshape(1, D_MODEL), ln1_b=ln1_b[l].reshape(1, D_MODEL),
        wr_hi=wr_hi, wr_lo=(wr - wr_hi.astype(F32)).astype(BF16),
        b_router=jnp.pad(b_router[l], (0, LANES - N_EXPERTS)).reshape(1, LANES),
        ws_gate=ws_gate[l].astype(BF16), ws_up=ws_up[l].astype(BF16),
        ws_down=ws_down[l].astype(BF16),
        ln2_g=ln2_g[l].reshape(1, D_MODEL), ln2_b=ln2_b[l].reshape(1, D_MODEL),
    )


def kernel(x_prompt, x_sample, cache_k, cache_v, cache_logf, state_conv, state_lru, ln_in_g, ln_in_b, w_in, b_f, w_out, g_att, g_lru, conv_w, conv_b, w_r, b_r, w_i, b_i, lru_lambda, ln1_g, ln1_b, w_router, b_router, we_gate, we_up, we_down, ws_gate, ws_up, ws_down, ln2_g, ln2_b):
    B, S, _ = x_prompt.shape
    Bd, Tn, _ = x_sample.shape
    depth = w_in.shape[0]
    P = cache_k.shape[2]
    Tp = B * S
    Ts = Bd * Tn
    alpha = (2 * depth) ** 0.25

    x = jnp.concatenate([x_prompt.reshape(Tp, D_MODEL), x_sample.reshape(Ts, D_MODEL)], axis=0)
    ln_g = ln_in_g.reshape(1, D_MODEL)
    ln_b = ln_in_b.reshape(1, D_MODEL)
    zero_conv = jnp.zeros((B, CONV_W - 1, LRU_W), F32)
    zero_h = jnp.zeros((B, 1, LRU_W), F32)

    ks, vs, lfs, p_conv, p_lru, s_conv, s_lru = [], [], [], [], [], [], []
    for l in range(depth):
        lw = _prep_layer(l, w_in, b_f, w_out, g_att, g_lru, conv_w, conv_b, w_r, b_r, w_i, b_i,
                         lru_lambda, ln1_g, ln1_b, w_router, b_router, ws_gate, ws_up, ws_down,
                         ln2_g, ln2_b)
        outs = _inproj(x, ln_g, ln_b, lw["w_pad"], lw["bf_pad"], apply_ln=(l == 0))
        if l == 0:
            x, outs = outs[0], outs[1:]
        qb, k, v, kb, vb, xr, xg, lf = outs

        c_p = _cumsum(lf[:Tp].reshape(B, S, LANES))
        ct_p = c_p[:, :, :N_HEADS].transpose(0, 2, 1).reshape(B, N_HEADS, S // ATT_TK, ATT_TK)
        lf_cache = jnp.pad(cache_logf[l], ((0, 0), (0, 0), (0, LANES - N_HEADS)))
        c_s = _cumsum(jnp.concatenate([lf_cache, lf[Tp:].reshape(Bd, Tn, LANES)], axis=1))
        ct_s = c_s[:, :, :N_HEADS].transpose(0, 2, 1)

        oa_p = _attn_prompt(qb, kb, vb, c_p, ct_p, B, S)
        oa_s = _attn_sample(qb, kb, vb, cache_k.reshape(depth, Bd, P * N_HEADS, HEAD_DIM),
                            cache_v.reshape(depth, Bd, P * N_HEADS, HEAD_DIM), l, c_s, ct_s,
                            Tp, Bd, P, Tn)

        ol_p, pc, ph = _lru(xr, xg, zero_conv, zero_h, lw, 0, B, S, LRU_TS)
        ol_s, sc, sh = _lru(xr, xg, state_conv[l], state_lru[l].reshape(Bd, 1, LRU_W), lw,
                            Tp, Bd, Tn, Tn)

        x1, x1s, idx, gates = _outproj(oa_p, oa_s, ol_p, ol_s, x, lw, alpha)
        x = _moe(x1, x1s, idx, gates, lw, we_gate, we_up, we_down, l, alpha)

        ks.append(k)
        vs.append(v)
        lfs.append(lf[:, :N_HEADS])
        p_conv.append(pc)
        p_lru.append(ph.reshape(B, LRU_W))
        s_conv.append(sc)
        s_lru.append(sh.reshape(Bd, LRU_W))

    def split(ts, tail):
        p = jnp.stack([t[:Tp].reshape((B, S) + tail) for t in ts])
        s = jnp.stack([t[Tp:].reshape((Bd, Tn) + tail) for t in ts])
        return p, s

    p_k, s_k = split(ks, (N_HEADS, HEAD_DIM))
    p_v, s_v = split(vs, (N_HEADS, HEAD_DIM))
    p_logf, s_logf = split(lfs, (N_HEADS,))
    y_prompt = x[:Tp].reshape(B, S, D_MODEL)
    y_sample = x[Tp:].reshape(Bd, Tn, D_MODEL)
    return (y_prompt, y_sample, p_k, p_v, p_logf, jnp.stack(p_conv), jnp.stack(p_lru),
            s_k, s_v, s_logf, jnp.stack(s_conv), jnp.stack(s_lru))
```

```python
import functools

import jax
import jax.numpy as jnp
from jax import lax
from jax.experimental import pallas as pl
from jax.experimental.pallas import tpu as pltpu

F32 = jnp.float32
BF16 = jnp.bfloat16
I32 = jnp.int32
U32 = jnp.uint32

D_MODEL = 2048
N_HEADS = 8
HEAD_DIM = 128
ATT_W = N_HEADS * HEAD_DIM
LRU_W = D_MODEL - ATT_W
LRU_BLOCKS = 8
LRU_BW = LRU_W // LRU_BLOCKS
CONV_W = 4
LRU_C = 8.0
N_EXPERTS = 64
TOP_K = 8
D_EXPERT = 512
D_SHARED = 512
ROUTE_SCALE = 2.5
ATT_SCALE = HEAD_DIM ** -0.5
LN_EPS = 1e-5
RMS_EPS = 1e-6

LANES = 128
SUBLANES = 8
SLAB_ROWS = D_MODEL // (2 * LANES)
VMEM_LIMIT = 56 * 1024 * 1024

TM = 256
MOE_BLK = 256
TM_COMBINE = 128
ATT_TQ = 512
ATT_TK = ATT_TQ
ATT_HEAD_GROUP = 2
LRU_TS = 256
CUM_CHUNK = 128

C_Q, C_K, C_V, C_XR, C_XG, C_F, C_END = 0, 1024, 2048, 3072, 4096, 5120, 5248


def _params(n_axes=1):
    return pltpu.CompilerParams(dimension_semantics=("arbitrary",) * n_axes,
                                vmem_limit_bytes=VMEM_LIMIT)


def _resident(shape):
    nd = len(shape)
    return pl.BlockSpec(shape, lambda *_: (0,) * nd, pipeline_mode=pl.Buffered(1))


def _layer_norm(x, g, b):
    mu = jnp.mean(x, axis=-1, keepdims=True)
    xc = x - mu
    var = jnp.mean(xc * xc, axis=-1, keepdims=True)
    return xc * lax.rsqrt(var + LN_EPS) * g + b


def _rms_norm(x, g):
    return x * lax.rsqrt(jnp.mean(x * x, axis=-1, keepdims=True) + RMS_EPS) * g


def _silu(x):
    return x * jax.nn.sigmoid(x)


def _gelu_tanh(x):
    return 0.5 * x * (1.0 + jnp.tanh(0.7978845608028654 * (x + 0.044715 * (x * x * x))))


def _split3(x):
    hi = x.astype(BF16)
    r1 = x - hi.astype(F32)
    mid = r1.astype(BF16)
    lo = (r1 - mid.astype(F32)).astype(BF16)
    return hi, mid, lo


def _inproj_kernel(apply_ln, x_ref, g_ref, b_ref, w_ref, bf_ref, *outs):
    if apply_ln:
        x0_ref, outs = outs[0], outs[1:]
    q_ref, k_ref, v_ref, kb_ref, vb_ref, xr_ref, xg_ref, lf_ref = outs
    x = x_ref[...]
    if apply_ln:
        x = _layer_norm(x, g_ref[...], b_ref[...])
        x0_ref[...] = x
    h = x.astype(BF16)

    def mm(c0, c1):
        return jnp.dot(h, w_ref[:, c0:c1], preferred_element_type=F32)

    q_ref[...] = mm(C_Q, C_K).astype(BF16)
    k = mm(C_K, C_V)
    k_ref[...] = k
    kb_ref[...] = k.astype(BF16)
    v = mm(C_V, C_XR)
    v_ref[...] = v
    vb_ref[...] = v.astype(BF16)
    xr_ref[...] = mm(C_XR, C_XG)
    xg_ref[...] = mm(C_XG, C_F)
    f = mm(C_F, C_END) + bf_ref[...]
    lf_ref[...] = jnp.minimum(f, 0.0) - jnp.log1p(jnp.exp(-jnp.abs(f)))


def _inproj(x, ln_g, ln_b, w_pad, bf_pad, apply_ln):
    T = x.shape[0]
    row = lambda w: pl.BlockSpec((TM, w), lambda i: (i, 0))
    out_shape = [
        jax.ShapeDtypeStruct((T, ATT_W), BF16),
        jax.ShapeDtypeStruct((T, ATT_W), F32),
        jax.ShapeDtypeStruct((T, ATT_W), F32),
        jax.ShapeDtypeStruct((T, ATT_W), BF16),
        jax.ShapeDtypeStruct((T, ATT_W), BF16),
        jax.ShapeDtypeStruct((T, LRU_W), F32),
        jax.ShapeDtypeStruct((T, LRU_W), F32),
        jax.ShapeDtypeStruct((T, LANES), F32),
    ]
    out_specs = [row(ATT_W)] * 5 + [row(LRU_W)] * 2 + [row(LANES)]
    if apply_ln:
        out_shape = [jax.ShapeDtypeStruct((T, D_MODEL), F32)] + out_shape
        out_specs = [row(D_MODEL)] + out_specs
    return pl.pallas_call(
        functools.partial(_inproj_kernel, apply_ln),
        grid=(T // TM,),
        in_specs=[row(D_MODEL), _resident((1, D_MODEL)), _resident((1, D_MODEL)),
                  _resident((D_MODEL, C_END)), _resident((1, LANES))],
        out_specs=out_specs,
        out_shape=out_shape,
        compiler_params=_params(),
        name="inproj",
    )(x, ln_g, ln_b, w_pad, bf_pad)


def _cumsum_kernel(lf_ref, c_ref, *, S):
    carry = jnp.zeros((1, LANES), F32)
    for c0 in range(0, S, CUM_CHUNK):
        n = min(CUM_CHUNK, S - c0)
        r = lax.broadcasted_iota(I32, (n, n), 0)
        c = lax.broadcasted_iota(I32, (n, n), 1)
        tril = (r >= c).astype(BF16)
        hi, mid, lo = _split3(lf_ref[c0:c0 + n, :])
        acc = jnp.dot(tril, hi, preferred_element_type=F32)
        acc = acc + jnp.dot(tril, mid, preferred_element_type=F32)
        acc = acc + jnp.dot(tril, lo, preferred_element_type=F32)
        acc = acc + carry
        c_ref[c0:c0 + n, :] = acc
        carry = acc[n - 1:n, :]


def _cumsum(lf):
    B, S, _ = lf.shape
    spec = pl.BlockSpec((None, S, LANES), lambda b: (b, 0, 0))
    return pl.pallas_call(
        functools.partial(_cumsum_kernel, S=S),
        grid=(B,),
        in_specs=[spec],
        out_specs=spec,
        out_shape=jax.ShapeDtypeStruct((B, S, LANES), F32),
        compiler_params=_params(),
        name="cumsum_logf",
    )(lf)


def _attn_prompt_kernel(q_ref, k_ref, v_ref, c_ref, ct_ref, o_ref, *, S):
    nq = S // ATT_TQ
    for h0 in range(0, N_HEADS, ATT_HEAD_GROUP):
        heads = range(h0, h0 + ATT_HEAD_GROUP)

        def q_body(qi, carry, heads=heads):
            q0 = pl.multiple_of(qi * ATT_TQ, ATT_TQ)
            hsl = [slice(h * HEAD_DIM, (h + 1) * HEAD_DIM) for h in heads]
            qs = [q_ref[pl.ds(q0, ATT_TQ), hs] for hs in hsl]
            cqs = [c_ref[pl.ds(q0, ATT_TQ), h:h + 1] for h in heads]

            def kv_step(kj, sts, diagonal):
                k0 = pl.multiple_of(kj * ATT_TK, ATT_TK)
                out = []
                for h, hs, q, cq, (m, l, acc) in zip(heads, hsl, qs, cqs, sts):
                    k = k_ref[pl.ds(k0, ATT_TK), hs]
                    v = v_ref[pl.ds(k0, ATT_TK), hs]
                    ck = ct_ref[h, pl.ds(kj, 1), :]
                    s = lax.dot_general(q, k, (((1,), (1,)), ((), ())),
                                        preferred_element_type=F32) * ATT_SCALE
                    s = s + cq - ck
                    if diagonal:
                        row = lax.broadcasted_iota(I32, (ATT_TQ, ATT_TK), 0)
                        col = lax.broadcasted_iota(I32, (ATT_TQ, ATT_TK), 1)
                        s = jnp.where(row >= col, s, -jnp.inf)
                    m_new = jnp.maximum(m, jnp.max(s, axis=1, keepdims=True))
                    alpha = jnp.exp(m - m_new)
                    p = jnp.exp(s - m_new)
                    l = alpha * l + jnp.sum(p, axis=1, keepdims=True)
                    acc = alpha * acc + jnp.dot(p.astype(BF16), v, preferred_element_type=F32)
                    out.append((m_new, l, acc))
                return tuple(out)

            init = tuple((jnp.full((ATT_TQ, 1), -jnp.inf, F32), jnp.zeros((ATT_TQ, 1), F32),
                          jnp.zeros((ATT_TQ, HEAD_DIM), F32)) for _ in heads)
            sts = lax.fori_loop(0, qi, lambda kj, sts: kv_step(kj, sts, False), init)
            sts = kv_step(qi, sts, True)
            for hs, (m, l, acc) in zip(hsl, sts):
                o_ref[pl.ds(q0, ATT_TQ), hs] = acc / l
            return carry

        lax.fori_loop(0, nq, q_body, 0)


def _attn_prompt(qb, kb, vb, c, ct, B, S):
    T = B * S
    seq = pl.BlockSpec((S, ATT_W), lambda b: (b, 0))
    return pl.pallas_call(
        functools.partial(_attn_prompt_kernel, S=S),
        grid=(B,),
        in_specs=[seq, seq, seq,
                  pl.BlockSpec((None, S, LANES), lambda b: (b, 0, 0)),
                  pl.BlockSpec((None, N_HEADS, S // ATT_TK, ATT_TK), lambda b: (b, 0, 0, 0))],
        out_specs=seq,
        out_shape=jax.ShapeDtypeStruct((T, ATT_W), F32),
        compiler_params=_params(),
        name="fox_prompt",
    )(qb, kb, vb, c, ct)


def _attn_sample_kernel(q_ref, kn_ref, vn_ref, ck_ref, cv_ref, c_ref, ct_ref, o_ref, *, P, Tn):
    r = lax.broadcasted_iota(I32, (Tn, Tn), 0)
    c = lax.broadcasted_iota(I32, (Tn, Tn), 1)
    nt = (((1,), (1,)), ((), ()))
    for h in range(N_HEADS):
        hs = slice(h * HEAD_DIM, (h + 1) * HEAD_DIM)
        q = q_ref[:, hs]
        kc = ck_ref[pl.ds(h, P, stride=N_HEADS), :].astype(BF16)
        vc = cv_ref[pl.ds(h, P, stride=N_HEADS), :].astype(BF16)
        cq = c_ref[P:P + Tn, h:h + 1]
        ckr = ct_ref[h:h + 1, :]
        s_c = lax.dot_general(q, kc, nt, preferred_element_type=F32) * ATT_SCALE
        s_c = s_c + cq - ckr[:, :P]
        s_n = lax.dot_general(q, kn_ref[:, hs], nt, preferred_element_type=F32) * ATT_SCALE
        s_n = jnp.where(r >= c, s_n + cq - ckr[:, P:P + Tn], -jnp.inf)
        m = jnp.maximum(jnp.max(s_c, axis=1, keepdims=True), jnp.max(s_n, axis=1, keepdims=True))
        p_c = jnp.exp(s_c - m)
        p_n = jnp.exp(s_n - m)
        l = jnp.sum(p_c, axis=1, keepdims=True) + jnp.sum(p_n, axis=1, keepdims=True)
        o = jnp.dot(p_c.astype(BF16), vc, preferred_element_type=F32)
        o = o + jnp.dot(p_n.astype(BF16), vn_ref[:, hs], preferred_element_type=F32)
        o_ref[:, hs] = o / l


def _attn_sample(qb, kb, vb, cache_k, cache_v, layer, c, ct, row0, Bd, P, Tn):
    blk0 = row0 // Tn
    new = pl.BlockSpec((Tn, ATT_W), lambda b: (blk0 + b, 0))
    cache = pl.BlockSpec((None, None, P * N_HEADS, HEAD_DIM), lambda b: (layer, b, 0, 0))
    return pl.pallas_call(
        functools.partial(_attn_sample_kernel, P=P, Tn=Tn),
        grid=(Bd,),
        in_specs=[new, new, new, cache, cache,
                  pl.BlockSpec((None, P + Tn, LANES), lambda b: (b, 0, 0)),
                  pl.BlockSpec((None, N_HEADS, P + Tn), lambda b: (b, 0, 0))],
        out_specs=pl.BlockSpec((Tn, ATT_W), lambda b: (b, 0)),
        out_shape=jax.ShapeDtypeStruct((Bd * Tn, ATT_W), F32),
        compiler_params=_params(),
        name="fox_sample",
    )(qb, kb, vb, cache_k, cache_v, c, ct)


def _lru_kernel(xr_ref, xg_ref, cs_ref, h0_ref, cw_ref, cb_ref, wr_ref, br_ref, wi_ref, bi_ref,
                lam_ref, o_ref, cso_ref, hl_ref, xp_ref, a_ref, u_ref, hs_ref, h_ref, *, ts):
    s = pl.program_id(1)
    pad = SUBLANES
    nprev = CONV_W - 1

    @pl.when(s == 0)
    def _():
        xp_ref[pad - nprev:pad, :] = cs_ref[...]
        h_ref[...] = h0_ref[...]

    x = xr_ref[...]
    xp_ref[pad:pad + ts, :] = x
    xc = cb_ref[...] + cw_ref[nprev:nprev + 1, :] * x
    for i in range(nprev):
        xc = xc + cw_ref[i:i + 1, :] * xp_ref[pad - nprev + i:pad - nprev + i + ts, :]
    tail = xp_ref[pad + ts - nprev:pad + ts, :]
    xp_ref[pad - nprev:pad, :] = tail
    cso_ref[...] = tail

    xcb = xc.astype(BF16)

    def block_diag(w_ref, b_ref):
        parts = [jnp.dot(xcb[:, n * LRU_BW:(n + 1) * LRU_BW], w_ref[n], preferred_element_type=F32)
                 for n in range(LRU_BLOCKS)]
        return jnp.concatenate(parts, axis=1) + b_ref[...]

    r = jax.nn.sigmoid(block_diag(wr_ref, br_ref))
    g = jax.nn.sigmoid(block_diag(wi_ref, bi_ref))
    nl = -lam_ref[...]
    softplus = jnp.maximum(nl, 0.0) + jnp.log1p(jnp.exp(-jnp.abs(nl)))
    log_a = (-LRU_C) * r * softplus
    a = jnp.exp(log_a)
    u = jnp.sqrt(1.0 - a * a) * g * xc

    sub = lax.broadcasted_iota(I32, (ts, LRU_W), 0) % SUBLANES
    d = 1
    while d < SUBLANES:
        keep = sub >= d
        u = jnp.where(keep, u + a * pltpu.roll(u, d, axis=0), u)
        a = jnp.where(keep, a * pltpu.roll(a, d, axis=0), a)
        d *= 2
    a_ref[...] = a
    u_ref[...] = u

    def group(gi, hprev):
        r0 = pl.multiple_of(gi * SUBLANES, SUBLANES)
        hh = u_ref[pl.ds(r0, SUBLANES), :] + a_ref[pl.ds(r0, SUBLANES), :] * hprev
        hs_ref[pl.ds(r0, SUBLANES), :] = hh
        return hh[SUBLANES - 1:SUBLANES, :]

    hlast = lax.fori_loop(0, ts // SUBLANES, group, h_ref[...])
    h_ref[...] = hlast
    hl_ref[...] = hlast
    o_ref[...] = hs_ref[...] * _gelu_tanh(xg_ref[...])


def _lru(xr, xg, conv_state, h0, lw, row0, nb, S, ts):
    ns = S // ts
    blk0 = row0 // ts
    rows = pl.BlockSpec((ts, LRU_W), lambda b, s: (blk0 + b * ns + s, 0))
    vec = lambda n: _resident((n, LRU_W))
    per_seq = lambda n: pl.BlockSpec((None, n, LRU_W), lambda b, s: (b, 0, 0))
    in_specs = [rows, rows, per_seq(CONV_W - 1), per_seq(1), vec(CONV_W), vec(1),
                _resident((LRU_BLOCKS, LRU_BW, LRU_BW)), vec(1),
                _resident((LRU_BLOCKS, LRU_BW, LRU_BW)), vec(1), vec(1)]
    args = [xr, xg, conv_state, h0, lw["conv_w"], lw["conv_b"], lw["w_r"], lw["b_r"],
            lw["w_i"], lw["b_i"], lw["lam"]]
    return pl.pallas_call(
        functools.partial(_lru_kernel, ts=ts),
        grid=(nb, ns),
        in_specs=in_specs,
        out_specs=[pl.BlockSpec((ts, LRU_W), lambda b, s: (b * ns + s, 0)),
                   per_seq(CONV_W - 1), per_seq(1)],
        out_shape=[jax.ShapeDtypeStruct((nb * S, LRU_W), F32),
                   jax.ShapeDtypeStruct((nb, CONV_W - 1, LRU_W), F32),
                   jax.ShapeDtypeStruct((nb, 1, LRU_W), F32)],
        scratch_shapes=[pltpu.VMEM((SUBLANES + ts, LRU_W), F32),
                        pltpu.VMEM((ts, LRU_W), F32), pltpu.VMEM((ts, LRU_W), F32),
                        pltpu.VMEM((ts, LRU_W), F32), pltpu.VMEM((1, LRU_W), F32)],
        compiler_params=_params(2),
        name="rg_lru",
    )(*args)


def _outproj_kernel(oap_ref, oas_ref, olp_ref, ols_ref, x_ref, w_ref, ga_ref, gl_ref, lg_ref,
                    lb_ref, wrh_ref, wrl_ref, rb_ref, x1_ref, x1s_ref, idx_ref, gate_ref,
                    *, alpha, n_prompt_blocks):
    is_prompt = pl.program_id(0) < n_prompt_blocks
    oa = jnp.where(is_prompt, oap_ref[...], oas_ref[...])
    ol = jnp.where(is_prompt, olp_ref[...], ols_ref[...])
    na = _rms_norm(oa, ga_ref[...]).astype(BF16)
    nl = _rms_norm(ol, gl_ref[...]).astype(BF16)
    m = jnp.dot(na, w_ref[0:ATT_W, :], preferred_element_type=F32)
    m = m + jnp.dot(nl, w_ref[ATT_W:D_MODEL, :], preferred_element_type=F32)
    x1 = _layer_norm(alpha * x_ref[...] + m, lg_ref[...], lb_ref[...])
    x1_ref[...] = x1
    _matrix_to_rows(x1s_ref, x1, x1.shape[0])

    xh = x1.astype(BF16)
    xl = (x1 - xh.astype(F32)).astype(BF16)
    logits = jnp.dot(xh, wrh_ref[...], preferred_element_type=F32)
    logits = logits + jnp.dot(xl, wrh_ref[...], preferred_element_type=F32)
    logits = logits + jnp.dot(xh, wrl_ref[...], preferred_element_type=F32)
    scores = jax.nn.sigmoid(logits)
    tm = scores.shape[0]
    lane = lax.broadcasted_iota(I32, (tm, LANES), 1)
    lane_f = lane.astype(F32)
    sel = jnp.where(lane < N_EXPERTS, scores + rb_ref[...], -jnp.inf)
    idx_out = jnp.zeros((tm, LANES), F32)
    gate_out = jnp.zeros((tm, LANES), F32)
    gsum = jnp.zeros((tm, 1), F32)
    for k in range(TOP_K):
        mx = jnp.max(sel, axis=1, keepdims=True)
        ik = jnp.min(jnp.where(sel == mx, lane_f, float(LANES)), axis=1, keepdims=True)
        hit = lane_f == ik
        gk = jnp.sum(jnp.where(hit, scores, 0.0), axis=1, keepdims=True)
        sel = jnp.where(hit, -jnp.inf, sel)
        idx_out = jnp.where(lane == k, ik, idx_out)
        gate_out = jnp.where(lane == k, gk, gate_out)
        gsum = gsum + gk
    idx_ref[...] = idx_out.astype(I32)
    gate_ref[...] = gate_out / gsum * ROUTE_SCALE


def _outproj(oa_p, oa_s, ol_p, ol_s, x, lw, alpha):
    T = x.shape[0]
    n_p = oa_p.shape[0] // TM
    row = lambda w: pl.BlockSpec((TM, w), lambda i: (i, 0))
    prompt = lambda w: pl.BlockSpec((TM, w), lambda i: (jnp.minimum(i, n_p - 1), 0))
    sample = lambda w: pl.BlockSpec((TM, w), lambda i: (jnp.maximum(i - n_p, 0), 0))
    vec = lambda n: _resident((1, n))
    return pl.pallas_call(
        functools.partial(_outproj_kernel, alpha=alpha, n_prompt_blocks=n_p),
        grid=(T // TM,),
        in_specs=[prompt(ATT_W), sample(ATT_W), prompt(LRU_W), sample(LRU_W), row(D_MODEL),
                  _resident((D_MODEL, D_MODEL)),
                  vec(ATT_W), vec(LRU_W), vec(D_MODEL), vec(D_MODEL),
                  _resident((D_MODEL, LANES)), _resident((D_MODEL, LANES)), vec(LANES)],
        out_specs=[row(D_MODEL), pl.BlockSpec((TM * SLAB_ROWS, LANES), lambda i: (i, 0)),
                   row(LANES), row(LANES)],
        out_shape=[jax.ShapeDtypeStruct((T, D_MODEL), F32),
                   jax.ShapeDtypeStruct((T * SLAB_ROWS, LANES), U32),
                   jax.ShapeDtypeStruct((T, LANES), I32),
                   jax.ShapeDtypeStruct((T, LANES), F32)],
        compiler_params=_params(),
        name="outproj_ln_router",
    )(oa_p, oa_s, ol_p, ol_s, x, lw["w_out"], lw["g_att"], lw["g_lru"], lw["ln1_g"], lw["ln1_b"],
      lw["wr_hi"], lw["wr_lo"], lw["b_router"])


def _rank_kernel(idx_ref, dest_ref, cnt_ref, carry_ref, start_ref):
    phase = pl.program_id(0)
    i = pl.program_id(1)

    @pl.when((phase == 0) & (i == 0))
    def _():
        carry_ref[...] = jnp.zeros_like(carry_ref)

    tm = idx_ref.shape[0]
    idx = idx_ref[...]
    lane = lax.broadcasted_iota(I32, (tm, LANES), 1)
    onehot = jnp.zeros((tm, LANES), F32)
    for k in range(TOP_K):
        onehot = onehot + (lane == idx[:, k:k + 1]).astype(F32)
    colsum = jnp.sum(onehot, axis=0, keepdims=True)

    @pl.when(phase == 0)
    def _():
        carry_ref[...] = carry_ref[...] + colsum
        dest_ref[...] = jnp.zeros_like(dest_ref)

    @pl.when((phase == 1) & (i == 0))
    def _():
        counts = carry_ref[...]
        cnt_ref[...] = counts.astype(I32)
        padded = jnp.floor((counts + (MOE_BLK - 1)) * (1.0 / MOE_BLK)) * MOE_BLK
        r = lax.broadcasted_iota(I32, (LANES, LANES), 0)
        c = lax.broadcasted_iota(I32, (LANES, LANES), 1)
        before = (r < c).astype(BF16)
        start = jnp.zeros((SUBLANES, LANES), F32)
        for term in _split3(jnp.broadcast_to(padded, (SUBLANES, LANES))):
            start = start + jnp.dot(term, before, preferred_element_type=F32)
        start_ref[...] = start[0:1, :]
        carry_ref[...] = jnp.zeros_like(carry_ref)

    @pl.when(phase == 1)
    def _():
        r = lax.broadcasted_iota(I32, (tm, tm), 0)
        c = lax.broadcasted_iota(I32, (tm, tm), 1)
        below = (r > c).astype(BF16)
        entry = jnp.dot(below, onehot.astype(BF16), preferred_element_type=F32)
        entry = entry + carry_ref[...] + start_ref[...]
        out = jnp.zeros((tm, LANES), F32)
        for k in range(TOP_K):
            ek = jnp.sum(jnp.where(lane == idx[:, k:k + 1], entry, 0.0), axis=1, keepdims=True)
            out = jnp.where(lane == k, ek, out)
        dest_ref[...] = out.astype(I32)
        carry_ref[...] = carry_ref[...] + colsum


def _rank(idx):
    T = idx.shape[0]
    row = pl.BlockSpec((TM, LANES), lambda p, i: (i, 0))
    return pl.pallas_call(
        _rank_kernel,
        grid=(2, T // TM),
        in_specs=[row],
        out_specs=[pl.BlockSpec((TM, LANES), lambda p, i: (i * p, 0)),
                   pl.BlockSpec((1, LANES), lambda p, i: (0, 0))],
        out_shape=[jax.ShapeDtypeStruct((T, LANES), I32), jax.ShapeDtypeStruct((1, LANES), I32)],
        scratch_shapes=[pltpu.VMEM((1, LANES), F32), pltpu.VMEM((1, LANES), F32)],
        compiler_params=_params(2),
        name="route_rank",
    )(idx)


def _dispatch_kernel(dest_ref, zst_ref, zcnt_ref, nu_ref, x_ref, xs_hbm, zblk, sem_o, sem_z,
                     *, n_blocks):
    i = pl.program_id(0)
    tm = x_ref.shape[0] // SLAB_ROWS
    blk_rows = MOE_BLK * SLAB_ROWS

    @pl.when(i == 0)
    def _():
        zblk[...] = jnp.zeros_like(zblk)
        zrow = zblk.at[pl.ds(0, SLAB_ROWS)]

        def pad_copy(slot):
            dst = xs_hbm.at[pl.ds(pl.multiple_of(slot * SLAB_ROWS, SLAB_ROWS), SLAB_ROWS)]
            return pltpu.make_async_copy(zrow, dst, sem_z)

        def block_copy(g):
            dst = xs_hbm.at[pl.ds(pl.multiple_of(g * blk_rows, blk_rows), blk_rows)]
            return pltpu.make_async_copy(zblk, dst, sem_z)

        def start_expert(e, carry):
            z0 = zst_ref[e]
            return lax.fori_loop(0, zcnt_ref[e], lambda j, c: (pad_copy(z0 + j).start(), c)[1], carry)

        def wait_expert(e, carry):
            return lax.fori_loop(0, zcnt_ref[e], lambda j, c: (pad_copy(0).wait(), c)[1], carry)

        lax.fori_loop(0, N_EXPERTS, start_expert, 0)
        lax.fori_loop(nu_ref[0], n_blocks, lambda g, c: (block_copy(g).start(), c)[1], 0)
        lax.fori_loop(0, N_EXPERTS, wait_expert, 0)
        lax.fori_loop(nu_ref[0], n_blocks, lambda g, c: (block_copy(g).wait(), c)[1], 0)

    base = i * (tm * TOP_K)

    def per_token(t, carry):
        src = x_ref.at[pl.ds(pl.multiple_of(t * SLAB_ROWS, SLAB_ROWS), SLAB_ROWS)]
        for k in range(TOP_K):
            slot = dest_ref[base + t * TOP_K + k]
            dst = xs_hbm.at[pl.ds(pl.multiple_of(slot * SLAB_ROWS, SLAB_ROWS), SLAB_ROWS)]
            pltpu.make_async_copy(src, dst, sem_o).start()
        return carry

    lax.fori_loop(0, tm, per_token, 0)
    for _ in range(TOP_K):
        pltpu.make_async_copy(x_ref, xs_hbm.at[pl.ds(0, tm * SLAB_ROWS)], sem_o).wait()


def _dispatch(dest_flat, zst, zcnt, n_used, x1s, n_blocks):
    T = x1s.shape[0] // SLAB_ROWS
    grid_spec = pltpu.PrefetchScalarGridSpec(
        num_scalar_prefetch=4,
        grid=(T // TM,),
        in_specs=[pl.BlockSpec((TM * SLAB_ROWS, LANES), lambda i, *_: (i, 0))],
        out_specs=pl.BlockSpec(memory_space=pl.ANY),
        scratch_shapes=[pltpu.VMEM((MOE_BLK * SLAB_ROWS, LANES), U32),
                        pltpu.SemaphoreType.DMA(()), pltpu.SemaphoreType.DMA(())],
    )
    return pl.pallas_call(
        functools.partial(_dispatch_kernel, n_blocks=n_blocks),
        grid_spec=grid_spec,
        out_shape=jax.ShapeDtypeStruct((n_blocks * MOE_BLK * SLAB_ROWS, LANES), U32),
        compiler_params=_params(),
        name="moe_dispatch",
    )(dest_flat, zst, zcnt, n_used, x1s)


def _rows_to_matrix(ref, n):
    cols = []
    for r in range(SLAB_ROWS):
        w = ref[pl.ds(r, n, stride=SLAB_ROWS), :]
        cols.append(pltpu.bitcast(w << 16, F32))
        cols.append(pltpu.bitcast(w & jnp.uint32(0xFFFF0000), F32))
    return jnp.concatenate(cols, axis=1)


def _matrix_to_rows(ref, y, n):
    for r in range(SLAB_ROWS):
        lo = y[:, (2 * r) * LANES:(2 * r + 1) * LANES].astype(BF16).astype(F32)
        hi = y[:, (2 * r + 1) * LANES:(2 * r + 2) * LANES].astype(BF16).astype(F32)
        ref[pl.ds(r, n, stride=SLAB_ROWS), :] = (
            pltpu.bitcast(hi, U32) | (pltpu.bitcast(lo, U32) >> 16))


def _experts_kernel(first_ref, nu_ref, xs_hbm, wg_ref, wu_ref, wd_ref, ys_hbm,
                    wg_s, wu_s, wd_s, xbuf0, xbuf1, ybuf0, ybuf1, gsem, ssem, *, n_blocks):
    e = pl.program_id(0)
    nu = nu_ref[0]
    blk_rows = MOE_BLK * SLAB_ROWS
    xbufs = (xbuf0, xbuf1)
    ybufs = (ybuf0, ybuf1)

    def rows_copy(g, xb, sem):
        src = xs_hbm.at[pl.ds(pl.multiple_of(g * blk_rows, blk_rows), blk_rows)]
        return pltpu.make_async_copy(src, xb, sem)

    def result_copy(g, yb, sem):
        dst = ys_hbm.at[pl.ds(pl.multiple_of(g * blk_rows, blk_rows), blk_rows)]
        return pltpu.make_async_copy(yb, dst, sem)

    @pl.when(e == 0)
    def _():
        rows_copy(0, xbuf0, gsem.at[0]).start()

    g0 = first_ref[e]
    g1 = first_ref[e + 1]

    @pl.when(g1 > g0)
    def _():
        wg_s[...] = wg_ref[...].astype(BF16)
        wu_s[...] = wu_ref[...].astype(BF16)
        wd_s[...] = wd_ref[...].astype(BF16)

    def step(g, p):
        xb_cur, xb_nxt = xbufs[p], xbufs[1 - p]
        yb_cur = ybufs[p]
        rows_copy(g, xb_cur, gsem.at[p]).wait()

        @pl.when(g >= 2)
        def _():
            result_copy(g - 2, yb_cur, ssem.at[p]).wait()

        @pl.when(g + 1 < nu)
        def _():
            rows_copy(g + 1, xb_nxt, gsem.at[1 - p]).start()

        x = _rows_to_matrix(xb_cur, MOE_BLK).astype(BF16)
        gt = jnp.dot(x, wg_s[...], preferred_element_type=F32)
        up = jnp.dot(x, wu_s[...], preferred_element_type=F32)
        h = (_silu(gt) * up).astype(BF16)
        y = jnp.dot(h, wd_s[...], preferred_element_type=F32)
        _matrix_to_rows(yb_cur, y, MOE_BLK)
        result_copy(g, yb_cur, ssem.at[p]).start()

        @pl.when(g == nu - 1)
        def _():
            @pl.when(g >= 1)
            def _():
                result_copy(g - 1, ybufs[1 - p], ssem.at[1 - p]).wait()

            result_copy(g, yb_cur, ssem.at[p]).wait()
            yb_cur[...] = jnp.zeros_like(yb_cur)

            def fill_start(gb, carry):
                result_copy(gb, yb_cur, ssem.at[p]).start()
                return carry

            def fill_wait(gb, carry):
                result_copy(gb, yb_cur, ssem.at[p]).wait()
                return carry

            lax.fori_loop(nu, n_blocks, fill_start, 0)
            lax.fori_loop(nu, n_blocks, fill_wait, 0)

    def block(g, carry):
        @pl.when(g % 2 == 0)
        def _():
            step(g, 0)

        @pl.when(g % 2 == 1)
        def _():
            step(g, 1)

        return carry

    lax.fori_loop(g0, g1, block, 0)


def _experts(first_blk, n_used, xs, we_gate, we_up, we_down, layer):
    blk_rows = MOE_BLK * SLAB_ROWS
    n_blocks = xs.shape[0] // blk_rows
    any_spec = pl.BlockSpec(memory_space=pl.ANY)
    w_in_spec = pl.BlockSpec((None, None, D_MODEL, D_EXPERT), lambda e, *_: (layer, e, 0, 0))
    grid_spec = pltpu.PrefetchScalarGridSpec(
        num_scalar_prefetch=2,
        grid=(N_EXPERTS,),
        in_specs=[any_spec, w_in_spec, w_in_spec,
                  pl.BlockSpec((None, None, D_EXPERT, D_MODEL), lambda e, *_: (layer, e, 0, 0))],
        out_specs=any_spec,
        scratch_shapes=[pltpu.VMEM((D_MODEL, D_EXPERT), BF16), pltpu.VMEM((D_MODEL, D_EXPERT), BF16),
                        pltpu.VMEM((D_EXPERT, D_MODEL), BF16),
                        pltpu.VMEM((blk_rows, LANES), U32), pltpu.VMEM((blk_rows, LANES), U32),
                        pltpu.VMEM((blk_rows, LANES), U32), pltpu.VMEM((blk_rows, LANES), U32),
                        pltpu.SemaphoreType.DMA((2,)), pltpu.SemaphoreType.DMA((2,))],
    )
    return pl.pallas_call(
        functools.partial(_experts_kernel, n_blocks=n_blocks),
        grid_spec=grid_spec,
        out_shape=jax.ShapeDtypeStruct((n_blocks * blk_rows, LANES), U32),
        compiler_params=_params(),
        name="moe_experts",
    )(first_blk, n_used, xs, we_gate, we_up, we_down)


def _combine_kernel(dest_ref, ys_hbm, gate_ref, x_ref, wsg_ref, wsu_ref, wsd_ref, lg_ref, lb_ref,
                    o_ref, buf0, buf1, gsem, *, alpha):
    i = pl.program_id(0)
    tm = x_ref.shape[0]
    bufs = (buf0, buf1)

    def gather_rows(step, buf, sem):
        base = step * (tm * TOP_K)

        def per_token(t, carry):
            d0 = pl.multiple_of(t * SLAB_ROWS, SLAB_ROWS)
            for k in range(TOP_K):
                slot = dest_ref[base + t * TOP_K + k]
                src = ys_hbm.at[pl.ds(pl.multiple_of(slot * SLAB_ROWS, SLAB_ROWS), SLAB_ROWS)]
                pltpu.make_async_copy(src, buf.at[k, pl.ds(d0, SLAB_ROWS)], sem).start()
            return carry

        lax.fori_loop(0, tm, per_token, 0)

    @pl.when(i == 0)
    def _():
        gather_rows(0, buf0, gsem.at[0])

    def step(p):
        @pl.when(i + 1 < pl.num_programs(0))
        def _():
            gather_rows(i + 1, bufs[1 - p], gsem.at[1 - p])

        x1 = x_ref[...]
        xb = x1.astype(BF16)
        hs = _silu(jnp.dot(xb, wsg_ref[...], preferred_element_type=F32))
        hs = (hs * jnp.dot(xb, wsu_ref[...], preferred_element_type=F32)).astype(BF16)
        acc = jnp.dot(hs, wsd_ref[...], preferred_element_type=F32)
        for k in range(TOP_K):
            pltpu.make_async_copy(ys_hbm.at[pl.ds(0, tm * SLAB_ROWS)], bufs[p].at[k],
                                  gsem.at[p]).wait()
        gates = gate_ref[...]
        for k in range(TOP_K):
            acc = acc + gates[:, k:k + 1] * _rows_to_matrix(bufs[p].at[k], tm)
        o_ref[...] = _layer_norm(alpha * x1 + acc, lg_ref[...], lb_ref[...])

    @pl.when(i % 2 == 0)
    def _():
        step(0)

    @pl.when(i % 2 == 1)
    def _():
        step(1)


def _combine(dest_flat, ys, gates, x1, lw, alpha):
    T = x1.shape[0]
    tm = TM_COMBINE
    row = lambda w: pl.BlockSpec((tm, w), lambda i, *_: (i, 0))
    res = _resident
    grid_spec = pltpu.PrefetchScalarGridSpec(
        num_scalar_prefetch=1,
        grid=(T // tm,),
        in_specs=[pl.BlockSpec(memory_space=pl.ANY), row(LANES), row(D_MODEL),
                  res((D_MODEL, D_SHARED)), res((D_MODEL, D_SHARED)), res((D_SHARED, D_MODEL)),
                  res((1, D_MODEL)), res((1, D_MODEL))],
        out_specs=row(D_MODEL),
        scratch_shapes=[pltpu.VMEM((TOP_K, tm * SLAB_ROWS, LANES), U32),
                        pltpu.VMEM((TOP_K, tm * SLAB_ROWS, LANES), U32),
                        pltpu.SemaphoreType.DMA((2,))],
    )
    return pl.pallas_call(
        functools.partial(_combine_kernel, alpha=alpha),
        grid_spec=grid_spec,
        out_shape=jax.ShapeDtypeStruct((T, D_MODEL), F32),
        compiler_params=_params(),
        name="moe_combine",
    )(dest_flat, ys, gates, x1, lw["ws_gate"], lw["ws_up"], lw["ws_down"], lw["ln2_g"],
      lw["ln2_b"])


def _block_tables(counts):
    padded = (counts + MOE_BLK - 1) // MOE_BLK * MOE_BLK
    pad_end = jnp.cumsum(padded)
    pad_start = pad_end - padded
    first_blk = jnp.concatenate([pad_start, pad_end[-1:]]) // MOE_BLK
    n_used = pad_end[-1:] // MOE_BLK
    pad_first = pad_start + counts
    pad_count = padded - counts
    return (first_blk.astype(I32), n_used.astype(I32), pad_first.astype(I32),
            pad_count.astype(I32))


def _moe(x1, x1s, idx, gates, lw, we_gate, we_up, we_down, layer, alpha):
    T = x1.shape[0]
    n_blocks = T * TOP_K // MOE_BLK + N_EXPERTS
    dest, cnt = _rank(idx)
    first_blk, n_used, pad_first, pad_count = _block_tables(cnt[0, :N_EXPERTS])
    dest_flat = dest[:, :TOP_K].reshape(-1)
    xs = _dispatch(dest_flat, pad_first, pad_count, n_used, x1s, n_blocks)
    ys = _experts(first_blk, n_used, xs, we_gate, we_up, we_down, layer)
    return _combine(dest_flat, ys, gates, x1, lw, alpha)


def _prep_layer(l, w_in, b_f, w_out, g_att, g_lru, conv_w, conv_b, w_r, b_r, w_i, b_i, lru_lambda,
                ln1_g, ln1_b, w_router, b_router, ws_gate, ws_up, ws_down, ln2_g, ln2_b):
    qkv_end = 3 * ATT_W
    wl = w_in[l]
    w_pad = jnp.concatenate(
        [wl[:, :qkv_end], wl[:, qkv_end + N_HEADS:],
         jnp.pad(wl[:, qkv_end:qkv_end + N_HEADS], ((0, 0), (0, LANES - N_HEADS)))],
        axis=1).astype(BF16)
    wr = jnp.pad(w_router[l], ((0, 0), (0, LANES - N_EXPERTS)))
    wr_hi = wr.astype(BF16)
    return dict(
        w_pad=w_pad,
        bf_pad=jnp.pad(b_f[l], (0, LANES - N_HEADS)).reshape(1, LANES),
        w_out=w_out[l].astype(BF16),
        g_att=g_att[l].reshape(1, ATT_W), g_lru=g_lru[l].reshape(1, LRU_W),
        conv_w=conv_w[l], conv_b=conv_b[l].reshape(1, LRU_W),
        w_r=w_r[l].astype(BF16), b_r=b_r[l].reshape(1, LRU_W),
        w_i=w_i[l].astype(BF16), b_i=b_i[l].reshape(1, LRU_W),
        lam=lru_lambda[l].reshape(1, LRU_W),
        ln1_g=ln1_g[l].reshape(1, D_MODEL), ln1_b=ln1_b[l].reshape(1, D_MODEL),
        wr_hi=wr_hi, wr_lo=(wr - wr_hi.astype(F32)).astype(BF16),
        b_router=jnp.pad(b_router[l], (0, LANES - N_EXPERTS)).reshape(1, LANES),
        ws_gate=ws_gate[l].astype(BF16), ws_up=ws_up[l].astype(BF16),
        ws_down=ws_down[l].astype(BF16),
        ln2_g=ln2_g[l].reshape(1, D_MODEL), ln2_b=ln2_b[l].reshape(1, D_MODEL),
    )


def kernel(x_prompt, x_sample, cache_k, cache_v, cache_logf, state_conv, state_lru, ln_in_g, ln_in_b, w_in, b_f, w_out, g_att, g_lru, conv_w, conv_b, w_r, b_r, w_i, b_i, lru_lambda, ln1_g, ln1_b, w_router, b_router, we_gate, we_up, we_down, ws_gate, ws_up, ws_down, ln2_g, ln2_b):
    B, S, _ = x_prompt.shape
    Bd, Tn, _ = x_sample.shape
    depth = w_in.shape[0]
    P = cache_k.shape[2]
    Tp = B * S
    Ts = Bd * Tn
    alpha = (2 * depth) ** 0.25

    x = jnp.concatenate([x_prompt.reshape(Tp, D_MODEL), x_sample.reshape(Ts, D_MODEL)], axis=0)
    ln_g = ln_in_g.reshape(1, D_MODEL)
    ln_b = ln_in_b.reshape(1, D_MODEL)
    zero_conv = jnp.zeros((B, CONV_W - 1, LRU_W), F32)
    zero_h = jnp.zeros((B, 1, LRU_W), F32)

    ks, vs, lfs, p_conv, p_lru, s_conv, s_lru = [], [], [], [], [], [], []
    for l in range(depth):
        lw = _prep_layer(l, w_in, b_f, w_out, g_att, g_lru, conv_w, conv_b, w_r, b_r, w_i, b_i,
                         lru_lambda, ln1_g, ln1_b, w_router, b_router, ws_gate, ws_up, ws_down,
                         ln2_g, ln2_b)
        outs = _inproj(x, ln_g, ln_b, lw["w_pad"], lw["bf_pad"], apply_ln=(l == 0))
        if l == 0:
            x, outs = outs[0], outs[1:]
        qb, k, v, kb, vb, xr, xg, lf = outs

        c_p = _cumsum(lf[:Tp].reshape(B, S, LANES))
        ct_p = c_p[:, :, :N_HEADS].transpose(0, 2, 1).reshape(B, N_HEADS, S // ATT_TK, ATT_TK)
        lf_cache = jnp.pad(cache_logf[l], ((0, 0), (0, 0), (0, LANES - N_HEADS)))
        c_s = _cumsum(jnp.concatenate([lf_cache, lf[Tp:].reshape(Bd, Tn, LANES)], axis=1))
        ct_s = c_s[:, :, :N_HEADS].transpose(0, 2, 1)

        oa_p = _attn_prompt(qb, kb, vb, c_p, ct_p, B, S)
        oa_s = _attn_sample(qb, kb, vb, cache_k.reshape(depth, Bd, P * N_HEADS, HEAD_DIM),
                            cache_v.reshape(depth, Bd, P * N_HEADS, HEAD_DIM), l, c_s, ct_s,
                            Tp, Bd, P, Tn)

        ol_p, pc, ph = _lru(xr, xg, zero_conv, zero_h, lw, 0, B, S, LRU_TS)
        ol_s, sc, sh = _lru(xr, xg, state_conv[l], state_lru[l].reshape(Bd, 1, LRU_W), lw,
                            Tp, Bd, Tn, Tn)

        x1, x1s, idx, gates = _outproj(oa_p, oa_s, ol_p, ol_s, x, lw, alpha)
        x = _moe(x1, x1s, idx, gates, lw, we_gate, we_up, we_down, l, alpha)

        ks.append(k)
        vs.append(v)
        lfs.append(lf[:, :N_HEADS])
        p_conv.append(pc)
        p_lru.append(ph.reshape(B, LRU_W))
        s_conv.append(sc)
        s_lru.append(sh.reshape(Bd, LRU_W))

    def split(ts, tail):
        p = jnp.stack([t[:Tp].reshape((B, S) + tail) for t in ts])
        s = jnp.stack([t[Tp:].reshape((Bd, Tn) + tail) for t in ts])
        return p, s

    p_k, s_k = split(ks, (N_HEADS, HEAD_DIM))
    p_v, s_v = split(vs, (N_HEADS, HEAD_DIM))
    p_logf, s_logf = split(lfs, (N_HEADS,))
    y_prompt = x[:Tp].reshape(B, S, D_MODEL)
    y_sample = x[Tp:].reshape(Bd, Tn, D_MODEL)
    return (y_prompt, y_sample, p_k, p_v, p_logf, jnp.stack(p_conv), jnp.stack(p_lru),
            s_k, s_v, s_logf, jnp.stack(s_conv), jnp.stack(s_lru))
```

```python
import functools

import jax
import jax.numpy as jnp
from jax import lax
from jax.experimental import pallas as pl
from jax.experimental.pallas import tpu as pltpu

F32 = jnp.float32
BF16 = jnp.bfloat16
I32 = jnp.int32
U32 = jnp.uint32

D_MODEL = 2048
N_HEADS = 8
HEAD_DIM = 128
ATT_W = N_HEADS * HEAD_DIM
LRU_W = D_MODEL - ATT_W
LRU_BLOCKS = 8
LRU_BW = LRU_W // LRU_BLOCKS
CONV_W = 4
LRU_C = 8.0
N_EXPERTS = 64
TOP_K = 8
D_EXPERT = 512
D_SHARED = 512
ROUTE_SCALE = 2.5
ATT_SCALE = HEAD_DIM ** -0.5
LN_EPS = 1e-5
RMS_EPS = 1e-6

LANES = 128
SUBLANES = 8
SLAB_ROWS = D_MODEL // (2 * LANES)
VMEM_LIMIT = 56 * 1024 * 1024

TM = 256
MOE_BLK = 256
TM_COMBINE = 128
DMA_QUEUES = 2
ATT_TQ = 512
ATT_TK = ATT_TQ
ATT_HEAD_GROUP = 2
LRU_TS = 256
CUM_CHUNK = 128

C_Q, C_K, C_V, C_XR, C_XG, C_F, C_END = 0, 1024, 2048, 3072, 4096, 5120, 5248


def _params(n_axes=1):
    return pltpu.CompilerParams(dimension_semantics=("arbitrary",) * n_axes,
                                vmem_limit_bytes=VMEM_LIMIT)


def _resident(shape):
    nd = len(shape)
    return pl.BlockSpec(shape, lambda *_: (0,) * nd, pipeline_mode=pl.Buffered(1))


def _layer_norm(x, g, b):
    mu = jnp.mean(x, axis=-1, keepdims=True)
    xc = x - mu
    var = jnp.mean(xc * xc, axis=-1, keepdims=True)
    return xc * lax.rsqrt(var + LN_EPS) * g + b


def _rms_norm(x, g):
    return x * lax.rsqrt(jnp.mean(x * x, axis=-1, keepdims=True) + RMS_EPS) * g


def _silu(x):
    return x * jax.nn.sigmoid(x)


def _gelu_tanh(x):
    return 0.5 * x * (1.0 + jnp.tanh(0.7978845608028654 * (x + 0.044715 * (x * x * x))))


def _split3(x):
    hi = x.astype(BF16)
    r1 = x - hi.astype(F32)
    mid = r1.astype(BF16)
    lo = (r1 - mid.astype(F32)).astype(BF16)
    return hi, mid, lo


def _inproj_kernel(apply_ln, x_ref, g_ref, b_ref, w_ref, bf_ref, *outs):
    if apply_ln:
        x0_ref, outs = outs[0], outs[1:]
    q_ref, k_ref, v_ref, kb_ref, vb_ref, xr_ref, xg_ref, lf_ref = outs
    x = x_ref[...]
    if apply_ln:
        x = _layer_norm(x, g_ref[...], b_ref[...])
        x0_ref[...] = x
    h = x.astype(BF16)

    def mm(c0, c1):
        return jnp.dot(h, w_ref[:, c0:c1], preferred_element_type=F32)

    q_ref[...] = mm(C_Q, C_K).astype(BF16)
    k = mm(C_K, C_V)
    k_ref[...] = k
    kb_ref[...] = k.astype(BF16)
    v = mm(C_V, C_XR)
    v_ref[...] = v
    vb_ref[...] = v.astype(BF16)
    xr_ref[...] = mm(C_XR, C_XG)
    xg_ref[...] = mm(C_XG, C_F)
    f = mm(C_F, C_END) + bf_ref[...]
    lf_ref[...] = jnp.minimum(f, 0.0) - jnp.log1p(jnp.exp(-jnp.abs(f)))


def _inproj(x, ln_g, ln_b, w_pad, bf_pad, apply_ln):
    T = x.shape[0]
    row = lambda w: pl.BlockSpec((TM, w), lambda i: (i, 0))
    out_shape = [
        jax.ShapeDtypeStruct((T, ATT_W), BF16),
        jax.ShapeDtypeStruct((T, ATT_W), F32),
        jax.ShapeDtypeStruct((T, ATT_W), F32),
        jax.ShapeDtypeStruct((T, ATT_W), BF16),
        jax.ShapeDtypeStruct((T, ATT_W), BF16),
        jax.ShapeDtypeStruct((T, LRU_W), F32),
        jax.ShapeDtypeStruct((T, LRU_W), F32),
        jax.ShapeDtypeStruct((T, LANES), F32),
    ]
    out_specs = [row(ATT_W)] * 5 + [row(LRU_W)] * 2 + [row(LANES)]
    if apply_ln:
        out_shape = [jax.ShapeDtypeStruct((T, D_MODEL), F32)] + out_shape
        out_specs = [row(D_MODEL)] + out_specs
    return pl.pallas_call(
        functools.partial(_inproj_kernel, apply_ln),
        grid=(T // TM,),
        in_specs=[row(D_MODEL), _resident((1, D_MODEL)), _resident((1, D_MODEL)),
                  _resident((D_MODEL, C_END)), _resident((1, LANES))],
        out_specs=out_specs,
        out_shape=out_shape,
        compiler_params=_params(),
        name="inproj",
    )(x, ln_g, ln_b, w_pad, bf_pad)


def _cumsum_kernel(lf_ref, c_ref, *, S):
    carry = jnp.zeros((1, LANES), F32)
    for c0 in range(0, S, CUM_CHUNK):
        n = min(CUM_CHUNK, S - c0)
        r = lax.broadcasted_iota(I32, (n, n), 0)
        c = lax.broadcasted_iota(I32, (n, n), 1)
        tril = (r >= c).astype(BF16)
        hi, mid, lo = _split3(lf_ref[c0:c0 + n, :])
        acc = jnp.dot(tril, hi, preferred_element_type=F32)
        acc = acc + jnp.dot(tril, mid, preferred_element_type=F32)
        acc = acc + jnp.dot(tril, lo, preferred_element_type=F32)
        acc = acc + carry
        c_ref[c0:c0 + n, :] = acc
        carry = acc[n - 1:n, :]


def _cumsum(lf):
    B, S, _ = lf.shape
    spec = pl.BlockSpec((None, S, LANES), lambda b: (b, 0, 0))
    return pl.pallas_call(
        functools.partial(_cumsum_kernel, S=S),
        grid=(B,),
        in_specs=[spec],
        out_specs=spec,
        out_shape=jax.ShapeDtypeStruct((B, S, LANES), F32),
        compiler_params=_params(),
        name="cumsum_logf",
    )(lf)


def _attn_prompt_kernel(q_ref, k_ref, v_ref, c_ref, ct_ref, o_ref, *, S):
    nq = S // ATT_TQ
    for h0 in range(0, N_HEADS, ATT_HEAD_GROUP):
        heads = range(h0, h0 + ATT_HEAD_GROUP)

        def q_body(qi, carry, heads=heads):
            q0 = pl.multiple_of(qi * ATT_TQ, ATT_TQ)
            hsl = [slice(h * HEAD_DIM, (h + 1) * HEAD_DIM) for h in heads]
            qs = [q_ref[pl.ds(q0, ATT_TQ), hs] for hs in hsl]
            cqs = [c_ref[pl.ds(q0, ATT_TQ), h:h + 1] for h in heads]

            def kv_step(kj, sts, diagonal):
                k0 = pl.multiple_of(kj * ATT_TK, ATT_TK)
                out = []
                for h, hs, q, cq, (m, l, acc) in zip(heads, hsl, qs, cqs, sts):
                    k = k_ref[pl.ds(k0, ATT_TK), hs]
                    v = v_ref[pl.ds(k0, ATT_TK), hs]
                    ck = ct_ref[h, pl.ds(kj, 1), :]
                    s = lax.dot_general(q, k, (((1,), (1,)), ((), ())),
                                        preferred_element_type=F32) * ATT_SCALE
                    s = s + cq - ck
                    if diagonal:
                        row = lax.broadcasted_iota(I32, (ATT_TQ, ATT_TK), 0)
                        col = lax.broadcasted_iota(I32, (ATT_TQ, ATT_TK), 1)
                        s = jnp.where(row >= col, s, -jnp.inf)
                    m_new = jnp.maximum(m, jnp.max(s, axis=1, keepdims=True))
                    alpha = jnp.exp(m - m_new)
                    p = jnp.exp(s - m_new)
                    l = alpha * l + jnp.sum(p, axis=1, keepdims=True)
                    acc = alpha * acc + jnp.dot(p.astype(BF16), v, preferred_element_type=F32)
                    out.append((m_new, l, acc))
                return tuple(out)

            init = tuple((jnp.full((ATT_TQ, 1), -jnp.inf, F32), jnp.zeros((ATT_TQ, 1), F32),
                          jnp.zeros((ATT_TQ, HEAD_DIM), F32)) for _ in heads)
            sts = lax.fori_loop(0, qi, lambda kj, sts: kv_step(kj, sts, False), init)
            sts = kv_step(qi, sts, True)
            for hs, (m, l, acc) in zip(hsl, sts):
                o_ref[pl.ds(q0, ATT_TQ), hs] = acc / l
            return carry

        lax.fori_loop(0, nq, q_body, 0)


def _attn_prompt(qb, kb, vb, c, ct, B, S):
    T = B * S
    seq = pl.BlockSpec((S, ATT_W), lambda b: (b, 0))
    return pl.pallas_call(
        functools.partial(_attn_prompt_kernel, S=S),
        grid=(B,),
        in_specs=[seq, seq, seq,
                  pl.BlockSpec((None, S, LANES), lambda b: (b, 0, 0)),
                  pl.BlockSpec((None, N_HEADS, S // ATT_TK, ATT_TK), lambda b: (b, 0, 0, 0))],
        out_specs=seq,
        out_shape=jax.ShapeDtypeStruct((T, ATT_W), F32),
        compiler_params=_params(),
        name="fox_prompt",
    )(qb, kb, vb, c, ct)


def _attn_sample_kernel(q_ref, kn_ref, vn_ref, ck_ref, cv_ref, c_ref, ct_ref, o_ref, *, P, Tn):
    r = lax.broadcasted_iota(I32, (Tn, Tn), 0)
    c = lax.broadcasted_iota(I32, (Tn, Tn), 1)
    nt = (((1,), (1,)), ((), ()))
    for h in range(N_HEADS):
        hs = slice(h * HEAD_DIM, (h + 1) * HEAD_DIM)
        q = q_ref[:, hs]
        kc = ck_ref[pl.ds(h, P, stride=N_HEADS), :].astype(BF16)
        vc = cv_ref[pl.ds(h, P, stride=N_HEADS), :].astype(BF16)
        cq = c_ref[P:P + Tn, h:h + 1]
        ckr = ct_ref[h:h + 1, :]
        s_c = lax.dot_general(q, kc, nt, preferred_element_type=F32) * ATT_SCALE
        s_c = s_c + cq - ckr[:, :P]
        s_n = lax.dot_general(q, kn_ref[:, hs], nt, preferred_element_type=F32) * ATT_SCALE
        s_n = jnp.where(r >= c, s_n + cq - ckr[:, P:P + Tn], -jnp.inf)
        m = jnp.maximum(jnp.max(s_c, axis=1, keepdims=True), jnp.max(s_n, axis=1, keepdims=True))
        p_c = jnp.exp(s_c - m)
        p_n = jnp.exp(s_n - m)
        l = jnp.sum(p_c, axis=1, keepdims=True) + jnp.sum(p_n, axis=1, keepdims=True)
        o = jnp.dot(p_c.astype(BF16), vc, preferred_element_type=F32)
        o = o + jnp.dot(p_n.astype(BF16), vn_ref[:, hs], preferred_element_type=F32)
        o_ref[:, hs] = o / l


def _attn_sample(qb, kb, vb, cache_k, cache_v, layer, c, ct, row0, Bd, P, Tn):
    blk0 = row0 // Tn
    new = pl.BlockSpec((Tn, ATT_W), lambda b: (blk0 + b, 0))
    cache = pl.BlockSpec((None, None, P * N_HEADS, HEAD_DIM), lambda b: (layer, b, 0, 0))
    return pl.pallas_call(
        functools.partial(_attn_sample_kernel, P=P, Tn=Tn),
        grid=(Bd,),
        in_specs=[new, new, new, cache, cache,
                  pl.BlockSpec((None, P + Tn, LANES), lambda b: (b, 0, 0)),
                  pl.BlockSpec((None, N_HEADS, P + Tn), lambda b: (b, 0, 0))],
        out_specs=pl.BlockSpec((Tn, ATT_W), lambda b: (b, 0)),
        out_shape=jax.ShapeDtypeStruct((Bd * Tn, ATT_W), F32),
        compiler_params=_params(),
        name="fox_sample",
    )(qb, kb, vb, cache_k, cache_v, c, ct)


def _lru_kernel(xr_ref, xg_ref, cs_ref, h0_ref, cw_ref, cb_ref, wr_ref, br_ref, wi_ref, bi_ref,
                lam_ref, o_ref, cso_ref, hl_ref, xp_ref, a_ref, u_ref, hs_ref, h_ref, *, ts):
    s = pl.program_id(1)
    pad = SUBLANES
    nprev = CONV_W - 1

    @pl.when(s == 0)
    def _():
        xp_ref[pad - nprev:pad, :] = cs_ref[...]
        h_ref[...] = h0_ref[...]

    x = xr_ref[...]
    xp_ref[pad:pad + ts, :] = x
    xc = cb_ref[...] + cw_ref[nprev:nprev + 1, :] * x
    for i in range(nprev):
        xc = xc + cw_ref[i:i + 1, :] * xp_ref[pad - nprev + i:pad - nprev + i + ts, :]
    tail = xp_ref[pad + ts - nprev:pad + ts, :]
    xp_ref[pad - nprev:pad, :] = tail
    cso_ref[...] = tail

    xcb = xc.astype(BF16)

    def block_diag(w_ref, b_ref):
        parts = [jnp.dot(xcb[:, n * LRU_BW:(n + 1) * LRU_BW], w_ref[n], preferred_element_type=F32)
                 for n in range(LRU_BLOCKS)]
        return jnp.concatenate(parts, axis=1) + b_ref[...]

    r = jax.nn.sigmoid(block_diag(wr_ref, br_ref))
    g = jax.nn.sigmoid(block_diag(wi_ref, bi_ref))
    nl = -lam_ref[...]
    softplus = jnp.maximum(nl, 0.0) + jnp.log1p(jnp.exp(-jnp.abs(nl)))
    log_a = (-LRU_C) * r * softplus
    a = jnp.exp(log_a)
    u = jnp.sqrt(1.0 - a * a) * g * xc

    sub = lax.broadcasted_iota(I32, (ts, LRU_W), 0) % SUBLANES
    d = 1
    while d < SUBLANES:
        keep = sub >= d
        u = jnp.where(keep, u + a * pltpu.roll(u, d, axis=0), u)
        a = jnp.where(keep, a * pltpu.roll(a, d, axis=0), a)
        d *= 2
    a_ref[...] = a
    u_ref[...] = u

    def group(gi, hprev):
        r0 = pl.multiple_of(gi * SUBLANES, SUBLANES)
        hh = u_ref[pl.ds(r0, SUBLANES), :] + a_ref[pl.ds(r0, SUBLANES), :] * hprev
        hs_ref[pl.ds(r0, SUBLANES), :] = hh
        return hh[SUBLANES - 1:SUBLANES, :]

    hlast = lax.fori_loop(0, ts // SUBLANES, group, h_ref[...])
    h_ref[...] = hlast
    hl_ref[...] = hlast
    o_ref[...] = hs_ref[...] * _gelu_tanh(xg_ref[...])


def _lru(xr, xg, conv_state, h0, lw, row0, nb, S, ts):
    ns = S // ts
    blk0 = row0 // ts
    rows = pl.BlockSpec((ts, LRU_W), lambda b, s: (blk0 + b * ns + s, 0))
    vec = lambda n: _resident((n, LRU_W))
    per_seq = lambda n: pl.BlockSpec((None, n, LRU_W), lambda b, s: (b, 0, 0))
    in_specs = [rows, rows, per_seq(CONV_W - 1), per_seq(1), vec(CONV_W), vec(1),
                _resident((LRU_BLOCKS, LRU_BW, LRU_BW)), vec(1),
                _resident((LRU_BLOCKS, LRU_BW, LRU_BW)), vec(1), vec(1)]
    args = [xr, xg, conv_state, h0, lw["conv_w"], lw["conv_b"], lw["w_r"], lw["b_r"],
            lw["w_i"], lw["b_i"], lw["lam"]]
    return pl.pallas_call(
        functools.partial(_lru_kernel, ts=ts),
        grid=(nb, ns),
        in_specs=in_specs,
        out_specs=[pl.BlockSpec((ts, LRU_W), lambda b, s: (b * ns + s, 0)),
                   per_seq(CONV_W - 1), per_seq(1)],
        out_shape=[jax.ShapeDtypeStruct((nb * S, LRU_W), F32),
                   jax.ShapeDtypeStruct((nb, CONV_W - 1, LRU_W), F32),
                   jax.ShapeDtypeStruct((nb, 1, LRU_W), F32)],
        scratch_shapes=[pltpu.VMEM((SUBLANES + ts, LRU_W), F32),
                        pltpu.VMEM((ts, LRU_W), F32), pltpu.VMEM((ts, LRU_W), F32),
                        pltpu.VMEM((ts, LRU_W), F32), pltpu.VMEM((1, LRU_W), F32)],
        compiler_params=_params(2),
        name="rg_lru",
    )(*args)


def _outproj_kernel(oap_ref, oas_ref, olp_ref, ols_ref, x_ref, w_ref, ga_ref, gl_ref, lg_ref,
                    lb_ref, wrh_ref, wrl_ref, rb_ref, x1_ref, x1s_ref, idx_ref, gate_ref,
                    *, alpha, n_prompt_blocks):
    is_prompt = pl.program_id(0) < n_prompt_blocks
    oa = jnp.where(is_prompt, oap_ref[...], oas_ref[...])
    ol = jnp.where(is_prompt, olp_ref[...], ols_ref[...])
    na = _rms_norm(oa, ga_ref[...]).astype(BF16)
    nl = _rms_norm(ol, gl_ref[...]).astype(BF16)
    m = jnp.dot(na, w_ref[0:ATT_W, :], preferred_element_type=F32)
    m = m + jnp.dot(nl, w_ref[ATT_W:D_MODEL, :], preferred_element_type=F32)
    x1 = _layer_norm(alpha * x_ref[...] + m, lg_ref[...], lb_ref[...])
    x1_ref[...] = x1
    _matrix_to_rows(x1s_ref, x1, x1.shape[0])

    xh = x1.astype(BF16)
    xl = (x1 - xh.astype(F32)).astype(BF16)
    logits = jnp.dot(xh, wrh_ref[...], preferred_element_type=F32)
    logits = logits + jnp.dot(xl, wrh_ref[...], preferred_element_type=F32)
    logits = logits + jnp.dot(xh, wrl_ref[...], preferred_element_type=F32)
    scores = jax.nn.sigmoid(logits)
    tm = scores.shape[0]
    lane = lax.broadcasted_iota(I32, (tm, LANES), 1)
    lane_f = lane.astype(F32)
    sel = jnp.where(lane < N_EXPERTS, scores + rb_ref[...], -jnp.inf)
    idx_out = jnp.zeros((tm, LANES), F32)
    gate_out = jnp.zeros((tm, LANES), F32)
    gsum = jnp.zeros((tm, 1), F32)
    for k in range(TOP_K):
        mx = jnp.max(sel, axis=1, keepdims=True)
        ik = jnp.min(jnp.where(sel == mx, lane_f, float(LANES)), axis=1, keepdims=True)
        hit = lane_f == ik
        gk = jnp.sum(jnp.where(hit, scores, 0.0), axis=1, keepdims=True)
        sel = jnp.where(hit, -jnp.inf, sel)
        idx_out = jnp.where(lane == k, ik, idx_out)
        gate_out = jnp.where(lane == k, gk, gate_out)
        gsum = gsum + gk
    idx_ref[...] = idx_out.astype(I32)
    gate_ref[...] = gate_out / gsum * ROUTE_SCALE


def _outproj(oa_p, oa_s, ol_p, ol_s, x, lw, alpha):
    T = x.shape[0]
    n_p = oa_p.shape[0] // TM
    row = lambda w: pl.BlockSpec((TM, w), lambda i: (i, 0))
    prompt = lambda w: pl.BlockSpec((TM, w), lambda i: (jnp.minimum(i, n_p - 1), 0))
    sample = lambda w: pl.BlockSpec((TM, w), lambda i: (jnp.maximum(i - n_p, 0), 0))
    vec = lambda n: _resident((1, n))
    return pl.pallas_call(
        functools.partial(_outproj_kernel, alpha=alpha, n_prompt_blocks=n_p),
        grid=(T // TM,),
        in_specs=[prompt(ATT_W), sample(ATT_W), prompt(LRU_W), sample(LRU_W), row(D_MODEL),
                  _resident((D_MODEL, D_MODEL)),
                  vec(ATT_W), vec(LRU_W), vec(D_MODEL), vec(D_MODEL),
                  _resident((D_MODEL, LANES)), _resident((D_MODEL, LANES)), vec(LANES)],
        out_specs=[row(D_MODEL), pl.BlockSpec((TM * SLAB_ROWS, LANES), lambda i: (i, 0)),
                   row(LANES), row(LANES)],
        out_shape=[jax.ShapeDtypeStruct((T, D_MODEL), F32),
                   jax.ShapeDtypeStruct((T * SLAB_ROWS, LANES), U32),
                   jax.ShapeDtypeStruct((T, LANES), I32),
                   jax.ShapeDtypeStruct((T, LANES), F32)],
        compiler_params=_params(),
        name="outproj_ln_router",
    )(oa_p, oa_s, ol_p, ol_s, x, lw["w_out"], lw["g_att"], lw["g_lru"], lw["ln1_g"], lw["ln1_b"],
      lw["wr_hi"], lw["wr_lo"], lw["b_router"])


def _rank_kernel(idx_ref, dest_ref, cnt_ref, carry_ref, start_ref):
    phase = pl.program_id(0)
    i = pl.program_id(1)

    @pl.when((phase == 0) & (i == 0))
    def _():
        carry_ref[...] = jnp.zeros_like(carry_ref)

    tm = idx_ref.shape[0]
    idx = idx_ref[...]
    lane = lax.broadcasted_iota(I32, (tm, LANES), 1)
    onehot = jnp.zeros((tm, LANES), F32)
    for k in range(TOP_K):
        onehot = onehot + (lane == idx[:, k:k + 1]).astype(F32)
    colsum = jnp.sum(onehot, axis=0, keepdims=True)

    @pl.when(phase == 0)
    def _():
        carry_ref[...] = carry_ref[...] + colsum
        dest_ref[...] = jnp.zeros_like(dest_ref)

    @pl.when((phase == 1) & (i == 0))
    def _():
        counts = carry_ref[...]
        cnt_ref[...] = counts.astype(I32)
        padded = jnp.floor((counts + (MOE_BLK - 1)) * (1.0 / MOE_BLK)) * MOE_BLK
        r = lax.broadcasted_iota(I32, (LANES, LANES), 0)
        c = lax.broadcasted_iota(I32, (LANES, LANES), 1)
        before = (r < c).astype(BF16)
        start = jnp.zeros((SUBLANES, LANES), F32)
        for term in _split3(jnp.broadcast_to(padded, (SUBLANES, LANES))):
            start = start + jnp.dot(term, before, preferred_element_type=F32)
        start_ref[...] = start[0:1, :]
        carry_ref[...] = jnp.zeros_like(carry_ref)

    @pl.when(phase == 1)
    def _():
        r = lax.broadcasted_iota(I32, (tm, tm), 0)
        c = lax.broadcasted_iota(I32, (tm, tm), 1)
        below = (r > c).astype(BF16)
        entry = jnp.dot(below, onehot.astype(BF16), preferred_element_type=F32)
        entry = entry + carry_ref[...] + start_ref[...]
        out = jnp.zeros((tm, LANES), F32)
        for k in range(TOP_K):
            ek = jnp.sum(jnp.where(lane == idx[:, k:k + 1], entry, 0.0), axis=1, keepdims=True)
            out = jnp.where(lane == k, ek, out)
        dest_ref[...] = out.astype(I32)
        carry_ref[...] = carry_ref[...] + colsum


def _rank(idx):
    T = idx.shape[0]
    row = pl.BlockSpec((TM, LANES), lambda p, i: (i, 0))
    return pl.pallas_call(
        _rank_kernel,
        grid=(2, T // TM),
        in_specs=[row],
        out_specs=[pl.BlockSpec((TM, LANES), lambda p, i: (i * p, 0)),
                   pl.BlockSpec((1, LANES), lambda p, i: (0, 0))],
        out_shape=[jax.ShapeDtypeStruct((T, LANES), I32), jax.ShapeDtypeStruct((1, LANES), I32)],
        scratch_shapes=[pltpu.VMEM((1, LANES), F32), pltpu.VMEM((1, LANES), F32)],
        compiler_params=_params(2),
        name="route_rank",
    )(idx)


def _dispatch_kernel(dest_ref, zst_ref, zcnt_ref, nu_ref, x_ref, xs_hbm, zblk, sem_o, sem_z,
                     *, n_blocks):
    i = pl.program_id(0)
    tm = x_ref.shape[0] // SLAB_ROWS
    blk_rows = MOE_BLK * SLAB_ROWS

    @pl.when(i == 0)
    def _():
        zblk[...] = jnp.zeros_like(zblk)
        zrow = zblk.at[pl.ds(0, SLAB_ROWS)]

        def pad_copy(slot):
            dst = xs_hbm.at[pl.ds(pl.multiple_of(slot * SLAB_ROWS, SLAB_ROWS), SLAB_ROWS)]
            return pltpu.make_async_copy(zrow, dst, sem_z)

        def block_copy(g):
            dst = xs_hbm.at[pl.ds(pl.multiple_of(g * blk_rows, blk_rows), blk_rows)]
            return pltpu.make_async_copy(zblk, dst, sem_z)

        def start_expert(e, carry):
            z0 = zst_ref[e]
            return lax.fori_loop(0, zcnt_ref[e], lambda j, c: (pad_copy(z0 + j).start(), c)[1], carry)

        def wait_expert(e, carry):
            return lax.fori_loop(0, zcnt_ref[e], lambda j, c: (pad_copy(0).wait(), c)[1], carry)

        lax.fori_loop(0, N_EXPERTS, start_expert, 0)
        lax.fori_loop(nu_ref[0], n_blocks, lambda g, c: (block_copy(g).start(), c)[1], 0)
        lax.fori_loop(0, N_EXPERTS, wait_expert, 0)
        lax.fori_loop(nu_ref[0], n_blocks, lambda g, c: (block_copy(g).wait(), c)[1], 0)

    base = i * (tm * TOP_K)

    def per_token(t, carry):
        src = x_ref.at[pl.ds(pl.multiple_of(t * SLAB_ROWS, SLAB_ROWS), SLAB_ROWS)]
        for k in range(TOP_K):
            slot = dest_ref[base + t * TOP_K + k]
            dst = xs_hbm.at[pl.ds(pl.multiple_of(slot * SLAB_ROWS, SLAB_ROWS), SLAB_ROWS)]
            pltpu.make_async_copy(src, dst, sem_o).start(priority=k % DMA_QUEUES)
        return carry

    lax.fori_loop(0, tm, per_token, 0)
    for _ in range(TOP_K):
        pltpu.make_async_copy(x_ref, xs_hbm.at[pl.ds(0, tm * SLAB_ROWS)], sem_o).wait()


def _dispatch(dest_flat, zst, zcnt, n_used, x1s, n_blocks):
    T = x1s.shape[0] // SLAB_ROWS
    grid_spec = pltpu.PrefetchScalarGridSpec(
        num_scalar_prefetch=4,
        grid=(T // TM,),
        in_specs=[pl.BlockSpec((TM * SLAB_ROWS, LANES), lambda i, *_: (i, 0))],
        out_specs=pl.BlockSpec(memory_space=pl.ANY),
        scratch_shapes=[pltpu.VMEM((MOE_BLK * SLAB_ROWS, LANES), U32),
                        pltpu.SemaphoreType.DMA(()), pltpu.SemaphoreType.DMA(())],
    )
    return pl.pallas_call(
        functools.partial(_dispatch_kernel, n_blocks=n_blocks),
        grid_spec=grid_spec,
        out_shape=jax.ShapeDtypeStruct((n_blocks * MOE_BLK * SLAB_ROWS, LANES), U32),
        compiler_params=_params(),
        name="moe_dispatch",
    )(dest_flat, zst, zcnt, n_used, x1s)


def _rows_to_matrix(ref, n):
    cols = []
    for r in range(SLAB_ROWS):
        w = ref[pl.ds(r, n, stride=SLAB_ROWS), :]
        cols.append(pltpu.bitcast(w << 16, F32))
        cols.append(pltpu.bitcast(w & jnp.uint32(0xFFFF0000), F32))
    return jnp.concatenate(cols, axis=1)


def _matrix_to_rows(ref, y, n):
    for r in range(SLAB_ROWS):
        lo = y[:, (2 * r) * LANES:(2 * r + 1) * LANES].astype(BF16).astype(F32)
        hi = y[:, (2 * r + 1) * LANES:(2 * r + 2) * LANES].astype(BF16).astype(F32)
        ref[pl.ds(r, n, stride=SLAB_ROWS), :] = (
            pltpu.bitcast(hi, U32) | (pltpu.bitcast(lo, U32) >> 16))


def _experts_kernel(first_ref, nu_ref, xs_hbm, wg_ref, wu_ref, wd_ref, ys_hbm,
                    wg_s, wu_s, wd_s, xbuf0, xbuf1, ybuf0, ybuf1, gsem, ssem, *, n_blocks):
    e = pl.program_id(0)
    nu = nu_ref[0]
    blk_rows = MOE_BLK * SLAB_ROWS
    xbufs = (xbuf0, xbuf1)
    ybufs = (ybuf0, ybuf1)

    def rows_copy(g, xb, sem):
        src = xs_hbm.at[pl.ds(pl.multiple_of(g * blk_rows, blk_rows), blk_rows)]
        return pltpu.make_async_copy(src, xb, sem)

    def result_copy(g, yb, sem):
        dst = ys_hbm.at[pl.ds(pl.multiple_of(g * blk_rows, blk_rows), blk_rows)]
        return pltpu.make_async_copy(yb, dst, sem)

    @pl.when(e == 0)
    def _():
        rows_copy(0, xbuf0, gsem.at[0]).start()

    g0 = first_ref[e]
    g1 = first_ref[e + 1]

    @pl.when(g1 > g0)
    def _():
        wg_s[...] = wg_ref[...].astype(BF16)
        wu_s[...] = wu_ref[...].astype(BF16)
        wd_s[...] = wd_ref[...].astype(BF16)

    def step(g, p):
        xb_cur, xb_nxt = xbufs[p], xbufs[1 - p]
        yb_cur = ybufs[p]
        rows_copy(g, xb_cur, gsem.at[p]).wait()

        @pl.when(g >= 2)
        def _():
            result_copy(g - 2, yb_cur, ssem.at[p]).wait()

        @pl.when(g + 1 < nu)
        def _():
            rows_copy(g + 1, xb_nxt, gsem.at[1 - p]).start(priority=1)

        x = _rows_to_matrix(xb_cur, MOE_BLK).astype(BF16)
        gt = jnp.dot(x, wg_s[...], preferred_element_type=F32)
        up = jnp.dot(x, wu_s[...], preferred_element_type=F32)
        h = (_silu(gt) * up).astype(BF16)
        y = jnp.dot(h, wd_s[...], preferred_element_type=F32)
        _matrix_to_rows(yb_cur, y, MOE_BLK)
        result_copy(g, yb_cur, ssem.at[p]).start()

        @pl.when(g == nu - 1)
        def _():
            @pl.when(g >= 1)
            def _():
                result_copy(g - 1, ybufs[1 - p], ssem.at[1 - p]).wait()

            result_copy(g, yb_cur, ssem.at[p]).wait()
            yb_cur[...] = jnp.zeros_like(yb_cur)

            def fill_start(gb, carry):
                result_copy(gb, yb_cur, ssem.at[p]).start()
                return carry

            def fill_wait(gb, carry):
                result_copy(gb, yb_cur, ssem.at[p]).wait()
                return carry

            lax.fori_loop(nu, n_blocks, fill_start, 0)
            lax.fori_loop(nu, n_blocks, fill_wait, 0)

    def block(g, carry):
        @pl.when(g % 2 == 0)
        def _():
            step(g, 0)

        @pl.when(g % 2 == 1)
        def _():
            step(g, 1)

        return carry

    lax.fori_loop(g0, g1, block, 0)


def _experts(first_blk, n_used, xs, we_gate, we_up, we_down, layer):
    blk_rows = MOE_BLK * SLAB_ROWS
    n_blocks = xs.shape[0] // blk_rows
    any_spec = pl.BlockSpec(memory_space=pl.ANY)
    w_in_spec = pl.BlockSpec((None, None, D_MODEL, D_EXPERT), lambda e, *_: (layer, e, 0, 0))
    grid_spec = pltpu.PrefetchScalarGridSpec(
        num_scalar_prefetch=2,
        grid=(N_EXPERTS,),
        in_specs=[any_spec, w_in_spec, w_in_spec,
                  pl.BlockSpec((None, None, D_EXPERT, D_MODEL), lambda e, *_: (layer, e, 0, 0))],
        out_specs=any_spec,
        scratch_shapes=[pltpu.VMEM((D_MODEL, D_EXPERT), BF16), pltpu.VMEM((D_MODEL, D_EXPERT), BF16),
                        pltpu.VMEM((D_EXPERT, D_MODEL), BF16),
                        pltpu.VMEM((blk_rows, LANES), U32), pltpu.VMEM((blk_rows, LANES), U32),
                        pltpu.VMEM((blk_rows, LANES), U32), pltpu.VMEM((blk_rows, LANES), U32),
                        pltpu.SemaphoreType.DMA((2,)), pltpu.SemaphoreType.DMA((2,))],
    )
    return pl.pallas_call(
        functools.partial(_experts_kernel, n_blocks=n_blocks),
        grid_spec=grid_spec,
        out_shape=jax.ShapeDtypeStruct((n_blocks * blk_rows, LANES), U32),
        compiler_params=_params(),
        name="moe_experts",
    )(first_blk, n_used, xs, we_gate, we_up, we_down)


def _combine_kernel(dest_ref, ys_hbm, gate_ref, x_ref, wsg_ref, wsu_ref, wsd_ref, lg_ref, lb_ref,
                    o_ref, buf0, buf1, gsem, *, alpha):
    i = pl.program_id(0)
    tm = x_ref.shape[0]
    bufs = (buf0, buf1)

    def gather_rows(step, buf, sem):
        base = step * (tm * TOP_K)

        def per_token(t, carry):
            d0 = pl.multiple_of(t * SLAB_ROWS, SLAB_ROWS)
            for k in range(TOP_K):
                slot = dest_ref[base + t * TOP_K + k]
                src = ys_hbm.at[pl.ds(pl.multiple_of(slot * SLAB_ROWS, SLAB_ROWS), SLAB_ROWS)]
                pltpu.make_async_copy(src, buf.at[k, pl.ds(d0, SLAB_ROWS)], sem).start(
                    priority=k % DMA_QUEUES)
            return carry

        lax.fori_loop(0, tm, per_token, 0)

    @pl.when(i == 0)
    def _():
        gather_rows(0, buf0, gsem.at[0])

    def step(p):
        @pl.when(i + 1 < pl.num_programs(0))
        def _():
            gather_rows(i + 1, bufs[1 - p], gsem.at[1 - p])

        x1 = x_ref[...]
        xb = x1.astype(BF16)
        hs = _silu(jnp.dot(xb, wsg_ref[...], preferred_element_type=F32))
        hs = (hs * jnp.dot(xb, wsu_ref[...], preferred_element_type=F32)).astype(BF16)
        acc = jnp.dot(hs, wsd_ref[...], preferred_element_type=F32)
        for k in range(TOP_K):
            pltpu.make_async_copy(ys_hbm.at[pl.ds(0, tm * SLAB_ROWS)], bufs[p].at[k],
                                  gsem.at[p]).wait()
        gates = gate_ref[...]
        for k in range(TOP_K):
            acc = acc + gates[:, k:k + 1] * _rows_to_matrix(bufs[p].at[k], tm)
        o_ref[...] = _layer_norm(alpha * x1 + acc, lg_ref[...], lb_ref[...])

    @pl.when(i % 2 == 0)
    def _():
        step(0)

    @pl.when(i % 2 == 1)
    def _():
        step(1)


def _combine(dest_flat, ys, gates, x1, lw, alpha):
    T = x1.shape[0]
    tm = TM_COMBINE
    row = lambda w: pl.BlockSpec((tm, w), lambda i, *_: (i, 0))
    res = _resident
    grid_spec = pltpu.PrefetchScalarGridSpec(
        num_scalar_prefetch=1,
        grid=(T // tm,),
        in_specs=[pl.BlockSpec(memory_space=pl.ANY), row(LANES), row(D_MODEL),
                  res((D_MODEL, D_SHARED)), res((D_MODEL, D_SHARED)), res((D_SHARED, D_MODEL)),
                  res((1, D_MODEL)), res((1, D_MODEL))],
        out_specs=row(D_MODEL),
        scratch_shapes=[pltpu.VMEM((TOP_K, tm * SLAB_ROWS, LANES), U32),
                        pltpu.VMEM((TOP_K, tm * SLAB_ROWS, LANES), U32),
                        pltpu.SemaphoreType.DMA((2,))],
    )
    return pl.pallas_call(
        functools.partial(_combine_kernel, alpha=alpha),
        grid_spec=grid_spec,
        out_shape=jax.ShapeDtypeStruct((T, D_MODEL), F32),
        compiler_params=_params(),
        name="moe_combine",
    )(dest_flat, ys, gates, x1, lw["ws_gate"], lw["ws_up"], lw["ws_down"], lw["ln2_g"],
      lw["ln2_b"])


def _block_tables(counts):
    padded = (counts + MOE_BLK - 1) // MOE_BLK * MOE_BLK
    pad_end = jnp.cumsum(padded)
    pad_start = pad_end - padded
    first_blk = jnp.concatenate([pad_start, pad_end[-1:]]) // MOE_BLK
    n_used = pad_end[-1:] // MOE_BLK
    pad_first = pad_start + counts
    pad_count = padded - counts
    return (first_blk.astype(I32), n_used.astype(I32), pad_first.astype(I32),
            pad_count.astype(I32))


def _moe(x1, x1s, idx, gates, lw, we_gate, we_up, we_down, layer, alpha):
    T = x1.shape[0]
    n_blocks = T * TOP_K // MOE_BLK + N_EXPERTS
    dest, cnt = _rank(idx)
    first_blk, n_used, pad_first, pad_count = _block_tables(cnt[0, :N_EXPERTS])
    dest_flat = dest[:, :TOP_K].reshape(-1)
    xs = _dispatch(dest_flat, pad_first, pad_count, n_used, x1s, n_blocks)
    ys = _experts(first_blk, n_used, xs, we_gate, we_up, we_down, layer)
    return _combine(dest_flat, ys, gates, x1, lw, alpha)


def _prep_layer(l, w_in, b_f, w_out, g_att, g_lru, conv_w, conv_b, w_r, b_r, w_i, b_i, lru_lambda,
                ln1_g, ln1_b, w_router, b_router, ws_gate, ws_up, ws_down, ln2_g, ln2_b):
    qkv_end = 3 * ATT_W
    wl = w_in[l]
    w_pad = jnp.concatenate(
        [wl[:, :qkv_end], wl[:, qkv_end + N_HEADS:],
         jnp.pad(wl[:, qkv_end:qkv_end + N_HEADS], ((0, 0), (0, LANES - N_HEADS)))],
        axis=1).astype(BF16)
    wr = jnp.pad(w_router[l], ((0, 0), (0, LANES - N_EXPERTS)))
    wr_hi = wr.astype(BF16)
    return dict(
        w_pad=w_pad,
        bf_pad=jnp.pad(b_f[l], (0, LANES - N_HEADS)).reshape(1, LANES),
        w_out=w_out[l].astype(BF16),
        g_att=g_att[l].reshape(1, ATT_W), g_lru=g_lru[l].reshape(1, LRU_W),
        conv_w=conv_w[l], conv_b=conv_b[l].reshape(1, LRU_W),
        w_r=w_r[l].astype(BF16), b_r=b_r[l].reshape(1, LRU_W),
        w_i=w_i[l].astype(BF16), b_i=b_i[l].reshape(1, LRU_W),
        lam=lru_lambda[l].reshape(1, LRU_W),
        ln1_g=ln1_g[l].reshape(1, D_MODEL), ln1_b=ln1_b[l].reshape(1, D_MODEL),
        wr_hi=wr_hi, wr_lo=(wr - wr_hi.astype(F32)).astype(BF16),
        b_router=jnp.pad(b_router[l], (0, LANES - N_EXPERTS)).reshape(1, LANES),
        ws_gate=ws_gate[l].astype(BF16), ws_up=ws_up[l].astype(BF16),
        ws_down=ws_down[l].astype(BF16),
        ln2_g=ln2_g[l].reshape(1, D_MODEL), ln2_b=ln2_b[l].reshape(1, D_MODEL),
    )


def kernel(x_prompt, x_sample, cache_k, cache_v, cache_logf, state_conv, state_lru, ln_in_g, ln_in_b, w_in, b_f, w_out, g_att, g_lru, conv_w, conv_b, w_r, b_r, w_i, b_i, lru_lambda, ln1_g, ln1_b, w_router, b_router, we_gate, we_up, we_down, ws_gate, ws_up, ws_down, ln2_g, ln2_b):
    B, S, _ = x_prompt.shape
    Bd, Tn, _ = x_sample.shape
    depth = w_in.shape[0]
    P = cache_k.shape[2]
    Tp = B * S
    Ts = Bd * Tn
    alpha = (2 * depth) ** 0.25

    x = jnp.concatenate([x_prompt.reshape(Tp, D_MODEL), x_sample.reshape(Ts, D_MODEL)], axis=0)
    ln_g = ln_in_g.reshape(1, D_MODEL)
    ln_b = ln_in_b.reshape(1, D_MODEL)
    zero_conv = jnp.zeros((B, CONV_W - 1, LRU_W), F32)
    zero_h = jnp.zeros((B, 1, LRU_W), F32)

    ks, vs, lfs, p_conv, p_lru, s_conv, s_lru = [], [], [], [], [], [], []
    for l in range(depth):
        lw = _prep_layer(l, w_in, b_f, w_out, g_att, g_lru, conv_w, conv_b, w_r, b_r, w_i, b_i,
                         lru_lambda, ln1_g, ln1_b, w_router, b_router, ws_gate, ws_up, ws_down,
                         ln2_g, ln2_b)
        outs = _inproj(x, ln_g, ln_b, lw["w_pad"], lw["bf_pad"], apply_ln=(l == 0))
        if l == 0:
            x, outs = outs[0], outs[1:]
        qb, k, v, kb, vb, xr, xg, lf = outs

        c_p = _cumsum(lf[:Tp].reshape(B, S, LANES))
        ct_p = c_p[:, :, :N_HEADS].transpose(0, 2, 1).reshape(B, N_HEADS, S // ATT_TK, ATT_TK)
        lf_cache = jnp.pad(cache_logf[l], ((0, 0), (0, 0), (0, LANES - N_HEADS)))
        c_s = _cumsum(jnp.concatenate([lf_cache, lf[Tp:].reshape(Bd, Tn, LANES)], axis=1))
        ct_s = c_s[:, :, :N_HEADS].transpose(0, 2, 1)

        oa_p = _attn_prompt(qb, kb, vb, c_p, ct_p, B, S)
        oa_s = _attn_sample(qb, kb, vb, cache_k.reshape(depth, Bd, P * N_HEADS, HEAD_DIM),
                            cache_v.reshape(depth, Bd, P * N_HEADS, HEAD_DIM), l, c_s, ct_s,
                            Tp, Bd, P, Tn)

        ol_p, pc, ph = _lru(xr, xg, zero_conv, zero_h, lw, 0, B, S, LRU_TS)
        ol_s, sc, sh = _lru(xr, xg, state_conv[l], state_lru[l].reshape(Bd, 1, LRU_W), lw,
                            Tp, Bd, Tn, Tn)

        x1, x1s, idx, gates = _outproj(oa_p, oa_s, ol_p, ol_s, x, lw, alpha)
        x = _moe(x1, x1s, idx, gates, lw, we_gate, we_up, we_down, l, alpha)

        ks.append(k)
        vs.append(v)
        lfs.append(lf[:, :N_HEADS])
        p_conv.append(pc)
        p_lru.append(ph.reshape(B, LRU_W))
        s_conv.append(sc)
        s_lru.append(sh.reshape(Bd, LRU_W))

    def split(ts, tail):
        p = jnp.stack([t[:Tp].reshape((B, S) + tail) for t in ts])
        s = jnp.stack([t[Tp:].reshape((Bd, Tn) + tail) for t in ts])
        return p, s

    p_k, s_k = split(ks, (N_HEADS, HEAD_DIM))
    p_v, s_v = split(vs, (N_HEADS, HEAD_DIM))
    p_logf, s_logf = split(lfs, (N_HEADS,))
    y_prompt = x[:Tp].reshape(B, S, D_MODEL)
    y_sample = x[Tp:].reshape(Bd, Tn, D_MODEL)
    return (y_prompt, y_sample, p_k, p_v, p_logf, jnp.stack(p_conv), jnp.stack(p_lru),
            s_k, s_v, s_logf, jnp.stack(s_conv), jnp.stack(s_lru))
```

```python
import functools

import jax
import jax.numpy as jnp
from jax import lax
from jax.experimental import pallas as pl
from jax.experimental.pallas import tpu as pltpu

F32 = jnp.float32
BF16 = jnp.bfloat16
I32 = jnp.int32
U32 = jnp.uint32

D_MODEL = 2048
N_HEADS = 8
HEAD_DIM = 128
ATT_W = N_HEADS * HEAD_DIM
LRU_W = D_MODEL - ATT_W
LRU_BLOCKS = 8
LRU_BW = LRU_W // LRU_BLOCKS
CONV_W = 4
LRU_C = 8.0
N_EXPERTS = 64
TOP_K = 8
D_EXPERT = 512
D_SHARED = 512
ROUTE_SCALE = 2.5
ATT_SCALE = HEAD_DIM ** -0.5
LN_EPS = 1e-5
RMS_EPS = 1e-6

LANES = 128
SUBLANES = 8
SLAB_ROWS = D_MODEL // (2 * LANES)
VMEM_LIMIT = 56 * 1024 * 1024

TM = 256
MOE_BLK = 256
TM_COMBINE = 128
DMA_QUEUES = 2
ATT_TQ = 512
ATT_TK = ATT_TQ
ATT_HEAD_GROUP = 2
LRU_TS = 256
CUM_CHUNK = 128

C_Q, C_K, C_V, C_XR, C_XG, C_F, C_END = 0, 1024, 2048, 3072, 4096, 5120, 5248


def _params(n_axes=1):
    return pltpu.CompilerParams(dimension_semantics=("arbitrary",) * n_axes,
                                vmem_limit_bytes=VMEM_LIMIT)


def _resident(shape):
    nd = len(shape)
    return pl.BlockSpec(shape, lambda *_: (0,) * nd, pipeline_mode=pl.Buffered(1))


def _layer_norm(x, g, b):
    mu = jnp.mean(x, axis=-1, keepdims=True)
    xc = x - mu
    var = jnp.mean(xc * xc, axis=-1, keepdims=True)
    return xc * lax.rsqrt(var + LN_EPS) * g + b


def _rms_norm(x, g):
    return x * lax.rsqrt(jnp.mean(x * x, axis=-1, keepdims=True) + RMS_EPS) * g


def _silu(x):
    return x * jax.nn.sigmoid(x)


def _gelu_tanh(x):
    return 0.5 * x * (1.0 + jnp.tanh(0.7978845608028654 * (x + 0.044715 * (x * x * x))))


def _split3(x):
    hi = x.astype(BF16)
    r1 = x - hi.astype(F32)
    mid = r1.astype(BF16)
    lo = (r1 - mid.astype(F32)).astype(BF16)
    return hi, mid, lo


def _inproj_kernel(first_layer, n_prompt_blocks, *refs):
    if first_layer:
        xp_ref, xs_ref, g_ref, b_ref, w_ref, bf_ref, x0_ref = refs[:7]
        outs = refs[7:]
        x = jnp.where(pl.program_id(0) < n_prompt_blocks, xp_ref[...], xs_ref[...])
        x = _layer_norm(x, g_ref[...], b_ref[...])
        x0_ref[...] = x
    else:
        x_ref, g_ref, b_ref, w_ref, bf_ref = refs[:5]
        outs = refs[5:]
        x = x_ref[...]
    q_ref, k_ref, v_ref, kb_ref, vb_ref, xr_ref, xg_ref, lf_ref = outs
    h = x.astype(BF16)

    def mm(c0, c1):
        return jnp.dot(h, w_ref[:, c0:c1], preferred_element_type=F32)

    q_ref[...] = mm(C_Q, C_K).astype(BF16)
    k = mm(C_K, C_V)
    k_ref[...] = k
    kb_ref[...] = k.astype(BF16)
    v = mm(C_V, C_XR)
    v_ref[...] = v
    vb_ref[...] = v.astype(BF16)
    xr_ref[...] = mm(C_XR, C_XG)
    xg_ref[...] = mm(C_XG, C_F)
    f = mm(C_F, C_END) + bf_ref[...]
    lf_ref[...] = jnp.minimum(f, 0.0) - jnp.log1p(jnp.exp(-jnp.abs(f)))


def _inproj(xs, ln_g, ln_b, w_pad, bf_pad):
    first_layer = len(xs) == 2
    T = sum(x.shape[0] for x in xs)
    n_p = xs[0].shape[0] // TM
    row = lambda w: pl.BlockSpec((TM, w), lambda i: (i, 0))
    out_shape = [
        jax.ShapeDtypeStruct((T, ATT_W), BF16),
        jax.ShapeDtypeStruct((T, ATT_W), F32),
        jax.ShapeDtypeStruct((T, ATT_W), F32),
        jax.ShapeDtypeStruct((T, ATT_W), BF16),
        jax.ShapeDtypeStruct((T, ATT_W), BF16),
        jax.ShapeDtypeStruct((T, LRU_W), F32),
        jax.ShapeDtypeStruct((T, LRU_W), F32),
        jax.ShapeDtypeStruct((T, LANES), F32),
    ]
    out_specs = [row(ATT_W)] * 5 + [row(LRU_W)] * 2 + [row(LANES)]
    if first_layer:
        out_shape = [jax.ShapeDtypeStruct((T, D_MODEL), F32)] + out_shape
        out_specs = [row(D_MODEL)] + out_specs
        x_specs = [pl.BlockSpec((TM, D_MODEL), lambda i: (jnp.minimum(i, n_p - 1), 0)),
                   pl.BlockSpec((TM, D_MODEL), lambda i: (jnp.maximum(i - n_p, 0), 0))]
    else:
        x_specs = [row(D_MODEL)]
    return pl.pallas_call(
        functools.partial(_inproj_kernel, first_layer, n_p),
        grid=(T // TM,),
        in_specs=x_specs + [_resident((1, D_MODEL)), _resident((1, D_MODEL)),
                            _resident((D_MODEL, C_END)), _resident((1, LANES))],
        out_specs=out_specs,
        out_shape=out_shape,
        compiler_params=_params(),
        name="inproj",
    )(*xs, ln_g, ln_b, w_pad, bf_pad)


def _cumsum_kernel(lf_ref, c_ref, *, S):
    carry = jnp.zeros((1, LANES), F32)
    for c0 in range(0, S, CUM_CHUNK):
        n = min(CUM_CHUNK, S - c0)
        r = lax.broadcasted_iota(I32, (n, n), 0)
        c = lax.broadcasted_iota(I32, (n, n), 1)
        tril = (r >= c).astype(BF16)
        hi, mid, lo = _split3(lf_ref[c0:c0 + n, :])
        acc = jnp.dot(tril, hi, preferred_element_type=F32)
        acc = acc + jnp.dot(tril, mid, preferred_element_type=F32)
        acc = acc + jnp.dot(tril, lo, preferred_element_type=F32)
        acc = acc + carry
        c_ref[c0:c0 + n, :] = acc
        carry = acc[n - 1:n, :]


def _cumsum(lf):
    B, S, _ = lf.shape
    spec = pl.BlockSpec((None, S, LANES), lambda b: (b, 0, 0))
    return pl.pallas_call(
        functools.partial(_cumsum_kernel, S=S),
        grid=(B,),
        in_specs=[spec],
        out_specs=spec,
        out_shape=jax.ShapeDtypeStruct((B, S, LANES), F32),
        compiler_params=_params(),
        name="cumsum_logf",
    )(lf)


def _attn_prompt_kernel(q_ref, k_ref, v_ref, c_ref, ct_ref, o_ref, *, S):
    nq = S // ATT_TQ
    for h0 in range(0, N_HEADS, ATT_HEAD_GROUP):
        heads = range(h0, h0 + ATT_HEAD_GROUP)

        def q_body(qi, carry, heads=heads):
            q0 = pl.multiple_of(qi * ATT_TQ, ATT_TQ)
            hsl = [slice(h * HEAD_DIM, (h + 1) * HEAD_DIM) for h in heads]
            qs = [q_ref[pl.ds(q0, ATT_TQ), hs] for hs in hsl]
            cqs = [c_ref[pl.ds(q0, ATT_TQ), h:h + 1] for h in heads]

            def kv_step(kj, sts, diagonal):
                k0 = pl.multiple_of(kj * ATT_TK, ATT_TK)
                out = []
                for h, hs, q, cq, (m, l, acc) in zip(heads, hsl, qs, cqs, sts):
                    k = k_ref[pl.ds(k0, ATT_TK), hs]
                    v = v_ref[pl.ds(k0, ATT_TK), hs]
                    ck = ct_ref[h, pl.ds(kj, 1), :]
                    s = lax.dot_general(q, k, (((1,), (1,)), ((), ())),
                                        preferred_element_type=F32) * ATT_SCALE
                    s = s + cq - ck
                    if diagonal:
                        row = lax.broadcasted_iota(I32, (ATT_TQ, ATT_TK), 0)
                        col = lax.broadcasted_iota(I32, (ATT_TQ, ATT_TK), 1)
                        s = jnp.where(row >= col, s, -jnp.inf)
                    m_new = jnp.maximum(m, jnp.max(s, axis=1, keepdims=True))
                    alpha = jnp.exp(m - m_new)
                    p = jnp.exp(s - m_new)
                    l = alpha * l + jnp.sum(p, axis=1, keepdims=True)
                    acc = alpha * acc + jnp.dot(p.astype(BF16), v, preferred_element_type=F32)
                    out.append((m_new, l, acc))
                return tuple(out)

            init = tuple((jnp.full((ATT_TQ, 1), -jnp.inf, F32), jnp.zeros((ATT_TQ, 1), F32),
                          jnp.zeros((ATT_TQ, HEAD_DIM), F32)) for _ in heads)
            sts = lax.fori_loop(0, qi, lambda kj, sts: kv_step(kj, sts, False), init)
            sts = kv_step(qi, sts, True)
            for hs, (m, l, acc) in zip(hsl, sts):
                o_ref[pl.ds(q0, ATT_TQ), hs] = acc / l
            return carry

        lax.fori_loop(0, nq, q_body, 0)


def _attn_prompt(qb, kb, vb, c, ct, B, S):
    T = B * S
    seq = pl.BlockSpec((S, ATT_W), lambda b: (b, 0))
    return pl.pallas_call(
        functools.partial(_attn_prompt_kernel, S=S),
        grid=(B,),
        in_specs=[seq, seq, seq,
                  pl.BlockSpec((None, S, LANES), lambda b: (b, 0, 0)),
                  pl.BlockSpec((None, N_HEADS, S // ATT_TK, ATT_TK), lambda b: (b, 0, 0, 0))],
        out_specs=seq,
        out_shape=jax.ShapeDtypeStruct((T, ATT_W), F32),
        compiler_params=_params(),
        name="fox_prompt",
    )(qb, kb, vb, c, ct)


def _attn_sample_kernel(q_ref, kn_ref, vn_ref, ck_ref, cv_ref, c_ref, ct_ref, o_ref, *, P, Tn):
    r = lax.broadcasted_iota(I32, (Tn, Tn), 0)
    c = lax.broadcasted_iota(I32, (Tn, Tn), 1)
    nt = (((1,), (1,)), ((), ()))
    for h in range(N_HEADS):
        hs = slice(h * HEAD_DIM, (h + 1) * HEAD_DIM)
        q = q_ref[:, hs]
        kc = ck_ref[pl.ds(h, P, stride=N_HEADS), :].astype(BF16)
        vc = cv_ref[pl.ds(h, P, stride=N_HEADS), :].astype(BF16)
        cq = c_ref[P:P + Tn, h:h + 1]
        ckr = ct_ref[h:h + 1, :]
        s_c = lax.dot_general(q, kc, nt, preferred_element_type=F32) * ATT_SCALE
        s_c = s_c + cq - ckr[:, :P]
        s_n = lax.dot_general(q, kn_ref[:, hs], nt, preferred_element_type=F32) * ATT_SCALE
        s_n = jnp.where(r >= c, s_n + cq - ckr[:, P:P + Tn], -jnp.inf)
        m = jnp.maximum(jnp.max(s_c, axis=1, keepdims=True), jnp.max(s_n, axis=1, keepdims=True))
        p_c = jnp.exp(s_c - m)
        p_n = jnp.exp(s_n - m)
        l = jnp.sum(p_c, axis=1, keepdims=True) + jnp.sum(p_n, axis=1, keepdims=True)
        o = jnp.dot(p_c.astype(BF16), vc, preferred_element_type=F32)
        o = o + jnp.dot(p_n.astype(BF16), vn_ref[:, hs], preferred_element_type=F32)
        o_ref[:, hs] = o / l


def _attn_sample(qb, kb, vb, cache_k, cache_v, layer, c, ct, row0, Bd, P, Tn):
    blk0 = row0 // Tn
    new = pl.BlockSpec((Tn, ATT_W), lambda b: (blk0 + b, 0))
    cache = pl.BlockSpec((None, None, P * N_HEADS, HEAD_DIM), lambda b: (layer, b, 0, 0))
    return pl.pallas_call(
        functools.partial(_attn_sample_kernel, P=P, Tn=Tn),
        grid=(Bd,),
        in_specs=[new, new, new, cache, cache,
                  pl.BlockSpec((None, P + Tn, LANES), lambda b: (b, 0, 0)),
                  pl.BlockSpec((None, N_HEADS, P + Tn), lambda b: (b, 0, 0))],
        out_specs=pl.BlockSpec((Tn, ATT_W), lambda b: (b, 0)),
        out_shape=jax.ShapeDtypeStruct((Bd * Tn, ATT_W), F32),
        compiler_params=_params(),
        name="fox_sample",
    )(qb, kb, vb, cache_k, cache_v, c, ct)


def _lru_kernel(xr_ref, xg_ref, cs_ref, h0_ref, cw_ref, cb_ref, wr_ref, br_ref, wi_ref, bi_ref,
                lam_ref, o_ref, cso_ref, hl_ref, xp_ref, a_ref, u_ref, hs_ref, h_ref, *, ts):
    s = pl.program_id(1)
    pad = SUBLANES
    nprev = CONV_W - 1

    @pl.when(s == 0)
    def _():
        xp_ref[pad - nprev:pad, :] = cs_ref[...]
        h_ref[...] = h0_ref[...]

    x = xr_ref[...]
    xp_ref[pad:pad + ts, :] = x
    xc = cb_ref[...] + cw_ref[nprev:nprev + 1, :] * x
    for i in range(nprev):
        xc = xc + cw_ref[i:i + 1, :] * xp_ref[pad - nprev + i:pad - nprev + i + ts, :]
    tail = xp_ref[pad + ts - nprev:pad + ts, :]
    xp_ref[pad - nprev:pad, :] = tail
    cso_ref[...] = tail

    xcb = xc.astype(BF16)

    def block_diag(w_ref, b_ref):
        parts = [jnp.dot(xcb[:, n * LRU_BW:(n + 1) * LRU_BW], w_ref[n], preferred_element_type=F32)
                 for n in range(LRU_BLOCKS)]
        return jnp.concatenate(parts, axis=1) + b_ref[...]

    r = jax.nn.sigmoid(block_diag(wr_ref, br_ref))
    g = jax.nn.sigmoid(block_diag(wi_ref, bi_ref))
    nl = -lam_ref[...]
    softplus = jnp.maximum(nl, 0.0) + jnp.log1p(jnp.exp(-jnp.abs(nl)))
    log_a = (-LRU_C) * r * softplus
    a = jnp.exp(log_a)
    u = jnp.sqrt(1.0 - a * a) * g * xc

    sub = lax.broadcasted_iota(I32, (ts, LRU_W), 0) % SUBLANES
    d = 1
    while d < SUBLANES:
        keep = sub >= d
        u = jnp.where(keep, u + a * pltpu.roll(u, d, axis=0), u)
        a = jnp.where(keep, a * pltpu.roll(a, d, axis=0), a)
        d *= 2
    a_ref[...] = a
    u_ref[...] = u

    def group(gi, hprev):
        r0 = pl.multiple_of(gi * SUBLANES, SUBLANES)
        hh = u_ref[pl.ds(r0, SUBLANES), :] + a_ref[pl.ds(r0, SUBLANES), :] * hprev
        hs_ref[pl.ds(r0, SUBLANES), :] = hh
        return hh[SUBLANES - 1:SUBLANES, :]

    hlast = lax.fori_loop(0, ts // SUBLANES, group, h_ref[...])
    h_ref[...] = hlast
    hl_ref[...] = hlast
    o_ref[...] = hs_ref[...] * _gelu_tanh(xg_ref[...])


def _lru(xr, xg, conv_state, h0, lw, row0, nb, S, ts):
    ns = S // ts
    blk0 = row0 // ts
    rows = pl.BlockSpec((ts, LRU_W), lambda b, s: (blk0 + b * ns + s, 0))
    vec = lambda n: _resident((n, LRU_W))
    per_seq = lambda n: pl.BlockSpec((None, n, LRU_W), lambda b, s: (b, 0, 0))
    in_specs = [rows, rows, per_seq(CONV_W - 1), per_seq(1), vec(CONV_W), vec(1),
                _resident((LRU_BLOCKS, LRU_BW, LRU_BW)), vec(1),
                _resident((LRU_BLOCKS, LRU_BW, LRU_BW)), vec(1), vec(1)]
    args = [xr, xg, conv_state, h0, lw["conv_w"], lw["conv_b"], lw["w_r"], lw["b_r"],
            lw["w_i"], lw["b_i"], lw["lam"]]
    return pl.pallas_call(
        functools.partial(_lru_kernel, ts=ts),
        grid=(nb, ns),
        in_specs=in_specs,
        out_specs=[pl.BlockSpec((ts, LRU_W), lambda b, s: (b * ns + s, 0)),
                   per_seq(CONV_W - 1), per_seq(1)],
        out_shape=[jax.ShapeDtypeStruct((nb * S, LRU_W), F32),
                   jax.ShapeDtypeStruct((nb, CONV_W - 1, LRU_W), F32),
                   jax.ShapeDtypeStruct((nb, 1, LRU_W), F32)],
        scratch_shapes=[pltpu.VMEM((SUBLANES + ts, LRU_W), F32),
                        pltpu.VMEM((ts, LRU_W), F32), pltpu.VMEM((ts, LRU_W), F32),
                        pltpu.VMEM((ts, LRU_W), F32), pltpu.VMEM((1, LRU_W), F32)],
        compiler_params=_params(2),
        name="rg_lru",
    )(*args)


def _outproj_kernel(oap_ref, oas_ref, olp_ref, ols_ref, x_ref, w_ref, ga_ref, gl_ref, lg_ref,
                    lb_ref, wrh_ref, wrl_ref, rb_ref, x1_ref, x1s_ref, idx_ref, gate_ref,
                    *, alpha, n_prompt_blocks):
    is_prompt = pl.program_id(0) < n_prompt_blocks
    oa = jnp.where(is_prompt, oap_ref[...], oas_ref[...])
    ol = jnp.where(is_prompt, olp_ref[...], ols_ref[...])
    na = _rms_norm(oa, ga_ref[...]).astype(BF16)
    nl = _rms_norm(ol, gl_ref[...]).astype(BF16)
    m = jnp.dot(na, w_ref[0:ATT_W, :], preferred_element_type=F32)
    m = m + jnp.dot(nl, w_ref[ATT_W:D_MODEL, :], preferred_element_type=F32)
    x1 = _layer_norm(alpha * x_ref[...] + m, lg_ref[...], lb_ref[...])
    x1_ref[...] = x1
    _matrix_to_rows(x1s_ref, x1, x1.shape[0])

    xh = x1.astype(BF16)
    xl = (x1 - xh.astype(F32)).astype(BF16)
    logits = jnp.dot(xh, wrh_ref[...], preferred_element_type=F32)
    logits = logits + jnp.dot(xl, wrh_ref[...], preferred_element_type=F32)
    logits = logits + jnp.dot(xh, wrl_ref[...], preferred_element_type=F32)
    scores = jax.nn.sigmoid(logits)
    tm = scores.shape[0]
    lane = lax.broadcasted_iota(I32, (tm, LANES), 1)
    lane_f = lane.astype(F32)
    sel = jnp.where(lane < N_EXPERTS, scores + rb_ref[...], -jnp.inf)
    idx_out = jnp.zeros((tm, LANES), F32)
    gate_out = jnp.zeros((tm, LANES), F32)
    gsum = jnp.zeros((tm, 1), F32)
    for k in range(TOP_K):
        mx = jnp.max(sel, axis=1, keepdims=True)
        ik = jnp.min(jnp.where(sel == mx, lane_f, float(LANES)), axis=1, keepdims=True)
        hit = lane_f == ik
        gk = jnp.sum(jnp.where(hit, scores, 0.0), axis=1, keepdims=True)
        sel = jnp.where(hit, -jnp.inf, sel)
        idx_out = jnp.where(lane == k, ik, idx_out)
        gate_out = jnp.where(lane == k, gk, gate_out)
        gsum = gsum + gk
    idx_ref[...] = idx_out.astype(I32)
    gate_ref[...] = gate_out / gsum * ROUTE_SCALE


def _outproj(oa_p, oa_s, ol_p, ol_s, x, lw, alpha):
    T = x.shape[0]
    n_p = oa_p.shape[0] // TM
    row = lambda w: pl.BlockSpec((TM, w), lambda i: (i, 0))
    prompt = lambda w: pl.BlockSpec((TM, w), lambda i: (jnp.minimum(i, n_p - 1), 0))
    sample = lambda w: pl.BlockSpec((TM, w), lambda i: (jnp.maximum(i - n_p, 0), 0))
    vec = lambda n: _resident((1, n))
    return pl.pallas_call(
        functools.partial(_outproj_kernel, alpha=alpha, n_prompt_blocks=n_p),
        grid=(T // TM,),
        in_specs=[prompt(ATT_W), sample(ATT_W), prompt(LRU_W), sample(LRU_W), row(D_MODEL),
                  _resident((D_MODEL, D_MODEL)),
                  vec(ATT_W), vec(LRU_W), vec(D_MODEL), vec(D_MODEL),
                  _resident((D_MODEL, LANES)), _resident((D_MODEL, LANES)), vec(LANES)],
        out_specs=[row(D_MODEL), pl.BlockSpec((TM * SLAB_ROWS, LANES), lambda i: (i, 0)),
                   row(LANES), row(LANES)],
        out_shape=[jax.ShapeDtypeStruct((T, D_MODEL), F32),
                   jax.ShapeDtypeStruct((T * SLAB_ROWS, LANES), U32),
                   jax.ShapeDtypeStruct((T, LANES), I32),
                   jax.ShapeDtypeStruct((T, LANES), F32)],
        compiler_params=_params(),
        name="outproj_ln_router",
    )(oa_p, oa_s, ol_p, ol_s, x, lw["w_out"], lw["g_att"], lw["g_lru"], lw["ln1_g"], lw["ln1_b"],
      lw["wr_hi"], lw["wr_lo"], lw["b_router"])


def _rank_kernel(idx_ref, dest_ref, cnt_ref, carry_ref, start_ref):
    phase = pl.program_id(0)
    i = pl.program_id(1)

    @pl.when((phase == 0) & (i == 0))
    def _():
        carry_ref[...] = jnp.zeros_like(carry_ref)

    tm = idx_ref.shape[0]
    idx = idx_ref[...]
    lane = lax.broadcasted_iota(I32, (tm, LANES), 1)
    onehot = jnp.zeros((tm, LANES), F32)
    for k in range(TOP_K):
        onehot = onehot + (lane == idx[:, k:k + 1]).astype(F32)
    colsum = jnp.sum(onehot, axis=0, keepdims=True)

    @pl.when(phase == 0)
    def _():
        carry_ref[...] = carry_ref[...] + colsum
        dest_ref[...] = jnp.zeros_like(dest_ref)

    @pl.when((phase == 1) & (i == 0))
    def _():
        counts = carry_ref[...]
        cnt_ref[...] = counts.astype(I32)
        padded = jnp.floor((counts + (MOE_BLK - 1)) * (1.0 / MOE_BLK)) * MOE_BLK
        r = lax.broadcasted_iota(I32, (LANES, LANES), 0)
        c = lax.broadcasted_iota(I32, (LANES, LANES), 1)
        before = (r < c).astype(BF16)
        start = jnp.zeros((SUBLANES, LANES), F32)
        for term in _split3(jnp.broadcast_to(padded, (SUBLANES, LANES))):
            start = start + jnp.dot(term, before, preferred_element_type=F32)
        start_ref[...] = start[0:1, :]
        carry_ref[...] = jnp.zeros_like(carry_ref)

    @pl.when(phase == 1)
    def _():
        r = lax.broadcasted_iota(I32, (tm, tm), 0)
        c = lax.broadcasted_iota(I32, (tm, tm), 1)
        below = (r > c).astype(BF16)
        entry = jnp.dot(below, onehot.astype(BF16), preferred_element_type=F32)
        entry = entry + carry_ref[...] + start_ref[...]
        out = jnp.zeros((tm, LANES), F32)
        for k in range(TOP_K):
            ek = jnp.sum(jnp.where(lane == idx[:, k:k + 1], entry, 0.0), axis=1, keepdims=True)
            out = jnp.where(lane == k, ek, out)
        dest_ref[...] = out.astype(I32)
        carry_ref[...] = carry_ref[...] + colsum


def _rank(idx):
    T = idx.shape[0]
    row = pl.BlockSpec((TM, LANES), lambda p, i: (i, 0))
    return pl.pallas_call(
        _rank_kernel,
        grid=(2, T // TM),
        in_specs=[row],
        out_specs=[pl.BlockSpec((TM, LANES), lambda p, i: (i * p, 0)),
                   pl.BlockSpec((1, LANES), lambda p, i: (0, 0))],
        out_shape=[jax.ShapeDtypeStruct((T, LANES), I32), jax.ShapeDtypeStruct((1, LANES), I32)],
        scratch_shapes=[pltpu.VMEM((1, LANES), F32), pltpu.VMEM((1, LANES), F32)],
        compiler_params=_params(2),
        name="route_rank",
    )(idx)


def _dispatch_kernel(dest_ref, zst_ref, zcnt_ref, nu_ref, x_ref, xs_hbm, zblk, sem_o, sem_z,
                     *, n_blocks):
    i = pl.program_id(0)
    tm = x_ref.shape[0] // SLAB_ROWS
    blk_rows = MOE_BLK * SLAB_ROWS

    @pl.when(i == 0)
    def _():
        zblk[...] = jnp.zeros_like(zblk)
        zrow = zblk.at[pl.ds(0, SLAB_ROWS)]

        def pad_copy(slot):
            dst = xs_hbm.at[pl.ds(pl.multiple_of(slot * SLAB_ROWS, SLAB_ROWS), SLAB_ROWS)]
            return pltpu.make_async_copy(zrow, dst, sem_z)

        def block_copy(g):
            dst = xs_hbm.at[pl.ds(pl.multiple_of(g * blk_rows, blk_rows), blk_rows)]
            return pltpu.make_async_copy(zblk, dst, sem_z)

        def start_expert(e, carry):
            z0 = zst_ref[e]
            return lax.fori_loop(0, zcnt_ref[e], lambda j, c: (pad_copy(z0 + j).start(), c)[1], carry)

        def wait_expert(e, carry):
            return lax.fori_loop(0, zcnt_ref[e], lambda j, c: (pad_copy(0).wait(), c)[1], carry)

        lax.fori_loop(0, N_EXPERTS, start_expert, 0)
        lax.fori_loop(nu_ref[0], n_blocks, lambda g, c: (block_copy(g).start(), c)[1], 0)
        lax.fori_loop(0, N_EXPERTS, wait_expert, 0)
        lax.fori_loop(nu_ref[0], n_blocks, lambda g, c: (block_copy(g).wait(), c)[1], 0)

    base = i * (tm * TOP_K)

    def per_token(t, carry):
        src = x_ref.at[pl.ds(pl.multiple_of(t * SLAB_ROWS, SLAB_ROWS), SLAB_ROWS)]
        for k in range(TOP_K):
            slot = dest_ref[base + t * TOP_K + k]
            dst = xs_hbm.at[pl.ds(pl.multiple_of(slot * SLAB_ROWS, SLAB_ROWS), SLAB_ROWS)]
            pltpu.make_async_copy(src, dst, sem_o).start(priority=k % DMA_QUEUES)
        return carry

    lax.fori_loop(0, tm, per_token, 0)
    for _ in range(TOP_K):
        pltpu.make_async_copy(x_ref, xs_hbm.at[pl.ds(0, tm * SLAB_ROWS)], sem_o).wait()


def _dispatch(dest_flat, zst, zcnt, n_used, x1s, n_blocks):
    T = x1s.shape[0] // SLAB_ROWS
    grid_spec = pltpu.PrefetchScalarGridSpec(
        num_scalar_prefetch=4,
        grid=(T // TM,),
        in_specs=[pl.BlockSpec((TM * SLAB_ROWS, LANES), lambda i, *_: (i, 0))],
        out_specs=pl.BlockSpec(memory_space=pl.ANY),
        scratch_shapes=[pltpu.VMEM((MOE_BLK * SLAB_ROWS, LANES), U32),
                        pltpu.SemaphoreType.DMA(()), pltpu.SemaphoreType.DMA(())],
    )
    return pl.pallas_call(
        functools.partial(_dispatch_kernel, n_blocks=n_blocks),
        grid_spec=grid_spec,
        out_shape=jax.ShapeDtypeStruct((n_blocks * MOE_BLK * SLAB_ROWS, LANES), U32),
        compiler_params=_params(),
        name="moe_dispatch",
    )(dest_flat, zst, zcnt, n_used, x1s)


def _rows_to_matrix(ref, n):
    cols = []
    for r in range(SLAB_ROWS):
        w = ref[pl.ds(r, n, stride=SLAB_ROWS), :]
        cols.append(pltpu.bitcast(w << 16, F32))
        cols.append(pltpu.bitcast(w & jnp.uint32(0xFFFF0000), F32))
    return jnp.concatenate(cols, axis=1)


def _matrix_to_rows(ref, y, n):
    for r in range(SLAB_ROWS):
        lo = y[:, (2 * r) * LANES:(2 * r + 1) * LANES].astype(BF16).astype(F32)
        hi = y[:, (2 * r + 1) * LANES:(2 * r + 2) * LANES].astype(BF16).astype(F32)
        ref[pl.ds(r, n, stride=SLAB_ROWS), :] = (
            pltpu.bitcast(hi, U32) | (pltpu.bitcast(lo, U32) >> 16))


def _experts_kernel(first_ref, nu_ref, xs_hbm, wg_ref, wu_ref, wd_ref, ys_hbm,
                    wg_s, wu_s, wd_s, xbuf0, xbuf1, ybuf0, ybuf1, gsem, ssem, *, n_blocks):
    e = pl.program_id(0)
    nu = nu_ref[0]
    blk_rows = MOE_BLK * SLAB_ROWS
    xbufs = (xbuf0, xbuf1)
    ybufs = (ybuf0, ybuf1)

    def rows_copy(g, xb, sem):
        src = xs_hbm.at[pl.ds(pl.multiple_of(g * blk_rows, blk_rows), blk_rows)]
        return pltpu.make_async_copy(src, xb, sem)

    def result_copy(g, yb, sem):
        dst = ys_hbm.at[pl.ds(pl.multiple_of(g * blk_rows, blk_rows), blk_rows)]
        return pltpu.make_async_copy(yb, dst, sem)

    @pl.when(e == 0)
    def _():
        rows_copy(0, xbuf0, gsem.at[0]).start()

    g0 = first_ref[e]
    g1 = first_ref[e + 1]

    @pl.when(g1 > g0)
    def _():
        wg_s[...] = wg_ref[...].astype(BF16)
        wu_s[...] = wu_ref[...].astype(BF16)
        wd_s[...] = wd_ref[...].astype(BF16)

    def step(g, p):
        xb_cur, xb_nxt = xbufs[p], xbufs[1 - p]
        yb_cur = ybufs[p]
        rows_copy(g, xb_cur, gsem.at[p]).wait()

        @pl.when(g >= 2)
        def _():
            result_copy(g - 2, yb_cur, ssem.at[p]).wait()

        @pl.when(g + 1 < nu)
        def _():
            rows_copy(g + 1, xb_nxt, gsem.at[1 - p]).start()

        x = _rows_to_matrix(xb_cur, MOE_BLK).astype(BF16)
        gt = jnp.dot(x, wg_s[...], preferred_element_type=F32)
        up = jnp.dot(x, wu_s[...], preferred_element_type=F32)
        h = (_silu(gt) * up).astype(BF16)
        y = jnp.dot(h, wd_s[...], preferred_element_type=F32)
        _matrix_to_rows(yb_cur, y, MOE_BLK)
        result_copy(g, yb_cur, ssem.at[p]).start()

        @pl.when(g == nu - 1)
        def _():
            @pl.when(g >= 1)
            def _():
                result_copy(g - 1, ybufs[1 - p], ssem.at[1 - p]).wait()

            result_copy(g, yb_cur, ssem.at[p]).wait()
            yb_cur[...] = jnp.zeros_like(yb_cur)

            def fill_start(gb, carry):
                result_copy(gb, yb_cur, ssem.at[p]).start()
                return carry

            def fill_wait(gb, carry):
                result_copy(gb, yb_cur, ssem.at[p]).wait()
                return carry

            lax.fori_loop(nu, n_blocks, fill_start, 0)
            lax.fori_loop(nu, n_blocks, fill_wait, 0)

    def block(g, carry):
        @pl.when(g % 2 == 0)
        def _():
            step(g, 0)

        @pl.when(g % 2 == 1)
        def _():
            step(g, 1)

        return carry

    lax.fori_loop(g0, g1, block, 0)


def _experts(first_blk, n_used, xs, we_gate, we_up, we_down, layer):
    blk_rows = MOE_BLK * SLAB_ROWS
    n_blocks = xs.shape[0] // blk_rows
    any_spec = pl.BlockSpec(memory_space=pl.ANY)
    w_in_spec = pl.BlockSpec((None, None, D_MODEL, D_EXPERT), lambda e, *_: (layer, e, 0, 0))
    grid_spec = pltpu.PrefetchScalarGridSpec(
        num_scalar_prefetch=2,
        grid=(N_EXPERTS,),
        in_specs=[any_spec, w_in_spec, w_in_spec,
                  pl.BlockSpec((None, None, D_EXPERT, D_MODEL), lambda e, *_: (layer, e, 0, 0))],
        out_specs=any_spec,
        scratch_shapes=[pltpu.VMEM((D_MODEL, D_EXPERT), BF16), pltpu.VMEM((D_MODEL, D_EXPERT), BF16),
                        pltpu.VMEM((D_EXPERT, D_MODEL), BF16),
                        pltpu.VMEM((blk_rows, LANES), U32), pltpu.VMEM((blk_rows, LANES), U32),
                        pltpu.VMEM((blk_rows, LANES), U32), pltpu.VMEM((blk_rows, LANES), U32),
                        pltpu.SemaphoreType.DMA((2,)), pltpu.SemaphoreType.DMA((2,))],
    )
    return pl.pallas_call(
        functools.partial(_experts_kernel, n_blocks=n_blocks),
        grid_spec=grid_spec,
        out_shape=jax.ShapeDtypeStruct((n_blocks * blk_rows, LANES), U32),
        compiler_params=_params(),
        name="moe_experts",
    )(first_blk, n_used, xs, we_gate, we_up, we_down)


def _combine_kernel(dest_ref, ys_hbm, gate_ref, x_ref, wsg_ref, wsu_ref, wsd_ref, lg_ref, lb_ref,
                    o_ref, buf0, buf1, gsem, *, alpha):
    i = pl.program_id(0)
    tm = x_ref.shape[0]
    bufs = (buf0, buf1)

    def gather_rows(step, buf, sem):
        base = step * (tm * TOP_K)

        def per_token(t, carry):
            d0 = pl.multiple_of(t * SLAB_ROWS, SLAB_ROWS)
            for k in range(TOP_K):
                slot = dest_ref[base + t * TOP_K + k]
                src = ys_hbm.at[pl.ds(pl.multiple_of(slot * SLAB_ROWS, SLAB_ROWS), SLAB_ROWS)]
                pltpu.make_async_copy(src, buf.at[k, pl.ds(d0, SLAB_ROWS)], sem).start(
                    priority=k % DMA_QUEUES)
            return carry

        lax.fori_loop(0, tm, per_token, 0)

    @pl.when(i == 0)
    def _():
        gather_rows(0, buf0, gsem.at[0])

    def step(p):
        @pl.when(i + 1 < pl.num_programs(0))
        def _():
            gather_rows(i + 1, bufs[1 - p], gsem.at[1 - p])

        x1 = x_ref[...]
        xb = x1.astype(BF16)
        hs = _silu(jnp.dot(xb, wsg_ref[...], preferred_element_type=F32))
        hs = (hs * jnp.dot(xb, wsu_ref[...], preferred_element_type=F32)).astype(BF16)
        acc = jnp.dot(hs, wsd_ref[...], preferred_element_type=F32)
        for k in range(TOP_K):
            pltpu.make_async_copy(ys_hbm.at[pl.ds(0, tm * SLAB_ROWS)], bufs[p].at[k],
                                  gsem.at[p]).wait()
        gates = gate_ref[...]
        for k in range(TOP_K):
            acc = acc + gates[:, k:k + 1] * _rows_to_matrix(bufs[p].at[k], tm)
        o_ref[...] = _layer_norm(alpha * x1 + acc, lg_ref[...], lb_ref[...])

    @pl.when(i % 2 == 0)
    def _():
        step(0)

    @pl.when(i % 2 == 1)
    def _():
        step(1)


def _combine(dest_flat, ys, gates, x1, lw, alpha):
    T = x1.shape[0]
    tm = TM_COMBINE
    row = lambda w: pl.BlockSpec((tm, w), lambda i, *_: (i, 0))
    res = _resident
    grid_spec = pltpu.PrefetchScalarGridSpec(
        num_scalar_prefetch=1,
        grid=(T // tm,),
        in_specs=[pl.BlockSpec(memory_space=pl.ANY), row(LANES), row(D_MODEL),
                  res((D_MODEL, D_SHARED)), res((D_MODEL, D_SHARED)), res((D_SHARED, D_MODEL)),
                  res((1, D_MODEL)), res((1, D_MODEL))],
        out_specs=row(D_MODEL),
        scratch_shapes=[pltpu.VMEM((TOP_K, tm * SLAB_ROWS, LANES), U32),
                        pltpu.VMEM((TOP_K, tm * SLAB_ROWS, LANES), U32),
                        pltpu.SemaphoreType.DMA((2,))],
    )
    return pl.pallas_call(
        functools.partial(_combine_kernel, alpha=alpha),
        grid_spec=grid_spec,
        out_shape=jax.ShapeDtypeStruct((T, D_MODEL), F32),
        compiler_params=_params(),
        name="moe_combine",
    )(dest_flat, ys, gates, x1, lw["ws_gate"], lw["ws_up"], lw["ws_down"], lw["ln2_g"],
      lw["ln2_b"])


def _block_tables(counts):
    padded = (counts + MOE_BLK - 1) // MOE_BLK * MOE_BLK
    pad_end = jnp.cumsum(padded)
    pad_start = pad_end - padded
    first_blk = jnp.concatenate([pad_start, pad_end[-1:]]) // MOE_BLK
    n_used = pad_end[-1:] // MOE_BLK
    pad_first = pad_start + counts
    pad_count = padded - counts
    return (first_blk.astype(I32), n_used.astype(I32), pad_first.astype(I32),
            pad_count.astype(I32))


def _moe(x1, x1s, idx, gates, lw, we_gate, we_up, we_down, layer, alpha):
    T = x1.shape[0]
    n_blocks = T * TOP_K // MOE_BLK + N_EXPERTS
    dest, cnt = _rank(idx)
    first_blk, n_used, pad_first, pad_count = _block_tables(cnt[0, :N_EXPERTS])
    dest_flat = dest[:, :TOP_K].reshape(-1)
    xs = _dispatch(dest_flat, pad_first, pad_count, n_used, x1s, n_blocks)
    ys = _experts(first_blk, n_used, xs, we_gate, we_up, we_down, layer)
    return _combine(dest_flat, ys, gates, x1, lw, alpha)


def _prep_layer(l, w_in, b_f, w_out, g_att, g_lru, conv_w, conv_b, w_r, b_r, w_i, b_i, lru_lambda,
                ln1_g, ln1_b, w_router, b_router, ws_gate, ws_up, ws_down, ln2_g, ln2_b):
    qkv_end = 3 * ATT_W
    wl = w_in[l]
    w_pad = jnp.concatenate(
        [wl[:, :qkv_end], wl[:, qkv_end + N_HEADS:],
         jnp.pad(wl[:, qkv_end:qkv_end + N_HEADS], ((0, 0), (0, LANES - N_HEADS)))],
        axis=1).astype(BF16)
    wr = jnp.pad(w_router[l], ((0, 0), (0, LANES - N_EXPERTS)))
    wr_hi = wr.astype(BF16)
    return dict(
        w_pad=w_pad,
        bf_pad=jnp.pad(b_f[l], (0, LANES - N_HEADS)).reshape(1, LANES),
        w_out=w_out[l].astype(BF16),
        g_att=g_att[l].reshape(1, ATT_W), g_lru=g_lru[l].reshape(1, LRU_W),
        conv_w=conv_w[l], conv_b=conv_b[l].reshape(1, LRU_W),
        w_r=w_r[l].astype(BF16), b_r=b_r[l].reshape(1, LRU_W),
        w_i=w_i[l].astype(BF16), b_i=b_i[l].reshape(1, LRU_W),
        lam=lru_lambda[l].reshape(1, LRU_W),
        ln1_g=ln1_g[l].reshape(1, D_MODEL), ln1_b=ln1_b[l].reshape(1, D_MODEL),
        wr_hi=wr_hi, wr_lo=(wr - wr_hi.astype(F32)).astype(BF16),
        b_router=jnp.pad(b_router[l], (0, LANES - N_EXPERTS)).reshape(1, LANES),
        ws_gate=ws_gate[l].astype(BF16), ws_up=ws_up[l].astype(BF16),
        ws_down=ws_down[l].astype(BF16),
        ln2_g=ln2_g[l].reshape(1, D_MODEL), ln2_b=ln2_b[l].reshape(1, D_MODEL),
    )


def kernel(x_prompt, x_sample, cache_k, cache_v, cache_logf, state_conv, state_lru, ln_in_g, ln_in_b, w_in, b_f, w_out, g_att, g_lru, conv_w, conv_b, w_r, b_r, w_i, b_i, lru_lambda, ln1_g, ln1_b, w_router, b_router, we_gate, we_up, we_down, ws_gate, ws_up, ws_down, ln2_g, ln2_b):
    B, S, _ = x_prompt.shape
    Bd, Tn, _ = x_sample.shape
    depth = w_in.shape[0]
    P = cache_k.shape[2]
    Tp = B * S
    Ts = Bd * Tn
    alpha = (2 * depth) ** 0.25

    xs = (x_prompt.reshape(Tp, D_MODEL), x_sample.reshape(Ts, D_MODEL))
    ln_g = ln_in_g.reshape(1, D_MODEL)
    ln_b = ln_in_b.reshape(1, D_MODEL)
    zero_conv = jnp.zeros((B, CONV_W - 1, LRU_W), F32)
    zero_h = jnp.zeros((B, 1, LRU_W), F32)

    ks, vs, lfs, p_conv, p_lru, s_conv, s_lru = [], [], [], [], [], [], []
    for l in range(depth):
        lw = _prep_layer(l, w_in, b_f, w_out, g_att, g_lru, conv_w, conv_b, w_r, b_r, w_i, b_i,
                         lru_lambda, ln1_g, ln1_b, w_router, b_router, ws_gate, ws_up, ws_down,
                         ln2_g, ln2_b)
        outs = _inproj(xs if l == 0 else (x,), ln_g, ln_b, lw["w_pad"], lw["bf_pad"])
        if l == 0:
            x, outs = outs[0], outs[1:]
        qb, k, v, kb, vb, xr, xg, lf = outs

        c_p = _cumsum(lf[:Tp].reshape(B, S, LANES))
        ct_p = c_p[:, :, :N_HEADS].transpose(0, 2, 1).reshape(B, N_HEADS, S // ATT_TK, ATT_TK)
        lf_cache = jnp.pad(cache_logf[l], ((0, 0), (0, 0), (0, LANES - N_HEADS)))
        c_s = _cumsum(jnp.concatenate([lf_cache, lf[Tp:].reshape(Bd, Tn, LANES)], axis=1))
        ct_s = c_s[:, :, :N_HEADS].transpose(0, 2, 1)

        oa_p = _attn_prompt(qb, kb, vb, c_p, ct_p, B, S)
        oa_s = _attn_sample(qb, kb, vb, cache_k.reshape(depth, Bd, P * N_HEADS, HEAD_DIM),
                            cache_v.reshape(depth, Bd, P * N_HEADS, HEAD_DIM), l, c_s, ct_s,
                            Tp, Bd, P, Tn)

        ol_p, pc, ph = _lru(xr, xg, zero_conv, zero_h, lw, 0, B, S, LRU_TS)
        ol_s, sc, sh = _lru(xr, xg, state_conv[l], state_lru[l].reshape(Bd, 1, LRU_W), lw,
                            Tp, Bd, Tn, Tn)

        x1, x1s, idx, gates = _outproj(oa_p, oa_s, ol_p, ol_s, x, lw, alpha)
        x = _moe(x1, x1s, idx, gates, lw, we_gate, we_up, we_down, l, alpha)

        ks.append(k)
        vs.append(v)
        lfs.append(lf[:, :N_HEADS])
        p_conv.append(pc)
        p_lru.append(ph.reshape(B, LRU_W))
        s_conv.append(sc)
        s_lru.append(sh.reshape(Bd, LRU_W))

    def split(ts, tail):
        p = jnp.stack([t[:Tp].reshape((B, S) + tail) for t in ts])
        s = jnp.stack([t[Tp:].reshape((Bd, Tn) + tail) for t in ts])
        return p, s

    p_k, s_k = split(ks, (N_HEADS, HEAD_DIM))
    p_v, s_v = split(vs, (N_HEADS, HEAD_DIM))
    p_logf, s_logf = split(lfs, (N_HEADS,))
    y_prompt = x[:Tp].reshape(B, S, D_MODEL)
    y_sample = x[Tp:].reshape(Bd, Tn, D_MODEL)
    return (y_prompt, y_sample, p_k, p_v, p_logf, jnp.stack(p_conv), jnp.stack(p_lru),
            s_k, s_v, s_logf, jnp.stack(s_conv), jnp.stack(s_lru))
```

```python
import functools

import jax
import jax.numpy as jnp
from jax import lax
from jax.experimental import pallas as pl
from jax.experimental.pallas import tpu as pltpu

F32 = jnp.float32
BF16 = jnp.bfloat16
I32 = jnp.int32
U32 = jnp.uint32

D_MODEL = 2048
N_HEADS = 8
HEAD_DIM = 128
ATT_W = N_HEADS * HEAD_DIM
LRU_W = D_MODEL - ATT_W
LRU_BLOCKS = 8
LRU_BW = LRU_W // LRU_BLOCKS
CONV_W = 4
LRU_C = 8.0
N_EXPERTS = 64
TOP_K = 8
D_EXPERT = 512
D_SHARED = 512
ROUTE_SCALE = 2.5
ATT_SCALE = HEAD_DIM ** -0.5
LN_EPS = 1e-5
RMS_EPS = 1e-6

LANES = 128
SUBLANES = 8
SLAB_ROWS = D_MODEL // (2 * LANES)
VMEM_LIMIT = 56 * 1024 * 1024

TM = 256
MOE_BLK = 256
TM_COMBINE = 128
DMA_QUEUES = 2
ROW_RING = 3
ATT_TQ = 512
ATT_TK = ATT_TQ
ATT_HEAD_GROUP = 2
LRU_TS = 256
CUM_CHUNK = 128

C_Q, C_K, C_V, C_XR, C_XG, C_F, C_END = 0, 1024, 2048, 3072, 4096, 5120, 5248


def _params(n_axes=1):
    return pltpu.CompilerParams(dimension_semantics=("arbitrary",) * n_axes,
                                vmem_limit_bytes=VMEM_LIMIT)


def _resident(shape):
    nd = len(shape)
    return pl.BlockSpec(shape, lambda *_: (0,) * nd, pipeline_mode=pl.Buffered(1))


def _layer_norm(x, g, b):
    mu = jnp.mean(x, axis=-1, keepdims=True)
    xc = x - mu
    var = jnp.mean(xc * xc, axis=-1, keepdims=True)
    return xc * lax.rsqrt(var + LN_EPS) * g + b


def _rms_norm(x, g):
    return x * lax.rsqrt(jnp.mean(x * x, axis=-1, keepdims=True) + RMS_EPS) * g


def _silu(x):
    return x * jax.nn.sigmoid(x)


def _gelu_tanh(x):
    return 0.5 * x * (1.0 + jnp.tanh(0.7978845608028654 * (x + 0.044715 * (x * x * x))))


def _split3(x):
    hi = x.astype(BF16)
    r1 = x - hi.astype(F32)
    mid = r1.astype(BF16)
    lo = (r1 - mid.astype(F32)).astype(BF16)
    return hi, mid, lo


def _inproj_kernel(first_layer, n_prompt_blocks, *refs):
    if first_layer:
        xp_ref, xs_ref, g_ref, b_ref, w_ref, bf_ref, x0_ref = refs[:7]
        outs = refs[7:]
        x = jnp.where(pl.program_id(0) < n_prompt_blocks, xp_ref[...], xs_ref[...])
        x = _layer_norm(x, g_ref[...], b_ref[...])
        x0_ref[...] = x
    else:
        x_ref, g_ref, b_ref, w_ref, bf_ref = refs[:5]
        outs = refs[5:]
        x = x_ref[...]
    q_ref, k_ref, v_ref, kb_ref, vb_ref, xr_ref, xg_ref, lf_ref = outs
    h = x.astype(BF16)

    def mm(c0, c1):
        return jnp.dot(h, w_ref[:, c0:c1], preferred_element_type=F32)

    q_ref[...] = mm(C_Q, C_K).astype(BF16)
    k = mm(C_K, C_V)
    k_ref[...] = k
    kb_ref[...] = k.astype(BF16)
    v = mm(C_V, C_XR)
    v_ref[...] = v
    vb_ref[...] = v.astype(BF16)
    xr_ref[...] = mm(C_XR, C_XG)
    xg_ref[...] = mm(C_XG, C_F)
    f = mm(C_F, C_END) + bf_ref[...]
    lf_ref[...] = jnp.minimum(f, 0.0) - jnp.log1p(jnp.exp(-jnp.abs(f)))


def _inproj(xs, ln_g, ln_b, w_pad, bf_pad):
    first_layer = len(xs) == 2
    T = sum(x.shape[0] for x in xs)
    n_p = xs[0].shape[0] // TM
    row = lambda w: pl.BlockSpec((TM, w), lambda i: (i, 0))
    out_shape = [
        jax.ShapeDtypeStruct((T, ATT_W), BF16),
        jax.ShapeDtypeStruct((T, ATT_W), F32),
        jax.ShapeDtypeStruct((T, ATT_W), F32),
        jax.ShapeDtypeStruct((T, ATT_W), BF16),
        jax.ShapeDtypeStruct((T, ATT_W), BF16),
        jax.ShapeDtypeStruct((T, LRU_W), F32),
        jax.ShapeDtypeStruct((T, LRU_W), F32),
        jax.ShapeDtypeStruct((T, LANES), F32),
    ]
    out_specs = [row(ATT_W)] * 5 + [row(LRU_W)] * 2 + [row(LANES)]
    if first_layer:
        out_shape = [jax.ShapeDtypeStruct((T, D_MODEL), F32)] + out_shape
        out_specs = [row(D_MODEL)] + out_specs
        x_specs = [pl.BlockSpec((TM, D_MODEL), lambda i: (jnp.minimum(i, n_p - 1), 0)),
                   pl.BlockSpec((TM, D_MODEL), lambda i: (jnp.maximum(i - n_p, 0), 0))]
    else:
        x_specs = [row(D_MODEL)]
    return pl.pallas_call(
        functools.partial(_inproj_kernel, first_layer, n_p),
        grid=(T // TM,),
        in_specs=x_specs + [_resident((1, D_MODEL)), _resident((1, D_MODEL)),
                            _resident((D_MODEL, C_END)), _resident((1, LANES))],
        out_specs=out_specs,
        out_shape=out_shape,
        compiler_params=_params(),
        name="inproj",
    )(*xs, ln_g, ln_b, w_pad, bf_pad)


def _cumsum_kernel(lf_ref, c_ref, *, S):
    carry = jnp.zeros((1, LANES), F32)
    for c0 in range(0, S, CUM_CHUNK):
        n = min(CUM_CHUNK, S - c0)
        r = lax.broadcasted_iota(I32, (n, n), 0)
        c = lax.broadcasted_iota(I32, (n, n), 1)
        tril = (r >= c).astype(BF16)
        hi, mid, lo = _split3(lf_ref[c0:c0 + n, :])
        acc = jnp.dot(tril, hi, preferred_element_type=F32)
        acc = acc + jnp.dot(tril, mid, preferred_element_type=F32)
        acc = acc + jnp.dot(tril, lo, preferred_element_type=F32)
        acc = acc + carry
        c_ref[c0:c0 + n, :] = acc
        carry = acc[n - 1:n, :]


def _cumsum(lf):
    B, S, _ = lf.shape
    spec = pl.BlockSpec((None, S, LANES), lambda b: (b, 0, 0))
    return pl.pallas_call(
        functools.partial(_cumsum_kernel, S=S),
        grid=(B,),
        in_specs=[spec],
        out_specs=spec,
        out_shape=jax.ShapeDtypeStruct((B, S, LANES), F32),
        compiler_params=_params(),
        name="cumsum_logf",
    )(lf)


def _attn_prompt_kernel(q_ref, k_ref, v_ref, c_ref, ct_ref, o_ref, *, S):
    nq = S // ATT_TQ
    for h0 in range(0, N_HEADS, ATT_HEAD_GROUP):
        heads = range(h0, h0 + ATT_HEAD_GROUP)

        def q_body(qi, carry, heads=heads):
            q0 = pl.multiple_of(qi * ATT_TQ, ATT_TQ)
            hsl = [slice(h * HEAD_DIM, (h + 1) * HEAD_DIM) for h in heads]
            qs = [q_ref[pl.ds(q0, ATT_TQ), hs] for hs in hsl]
            cqs = [c_ref[pl.ds(q0, ATT_TQ), h:h + 1] for h in heads]

            def kv_step(kj, sts, diagonal):
                k0 = pl.multiple_of(kj * ATT_TK, ATT_TK)
                out = []
                for h, hs, q, cq, (m, l, acc) in zip(heads, hsl, qs, cqs, sts):
                    k = k_ref[pl.ds(k0, ATT_TK), hs]
                    v = v_ref[pl.ds(k0, ATT_TK), hs]
                    ck = ct_ref[h, pl.ds(kj, 1), :]
                    s = lax.dot_general(q, k, (((1,), (1,)), ((), ())),
                                        preferred_element_type=F32) * ATT_SCALE
                    s = s + cq - ck
                    if diagonal:
                        row = lax.broadcasted_iota(I32, (ATT_TQ, ATT_TK), 0)
                        col = lax.broadcasted_iota(I32, (ATT_TQ, ATT_TK), 1)
                        s = jnp.where(row >= col, s, -jnp.inf)
                    m_new = jnp.maximum(m, jnp.max(s, axis=1, keepdims=True))
                    alpha = jnp.exp(m - m_new)
                    p = jnp.exp(s - m_new)
                    l = alpha * l + jnp.sum(p, axis=1, keepdims=True)
                    acc = alpha * acc + jnp.dot(p.astype(BF16), v, preferred_element_type=F32)
                    out.append((m_new, l, acc))
                return tuple(out)

            init = tuple((jnp.full((ATT_TQ, 1), -jnp.inf, F32), jnp.zeros((ATT_TQ, 1), F32),
                          jnp.zeros((ATT_TQ, HEAD_DIM), F32)) for _ in heads)
            sts = lax.fori_loop(0, qi, lambda kj, sts: kv_step(kj, sts, False), init)
            sts = kv_step(qi, sts, True)
            for hs, (m, l, acc) in zip(hsl, sts):
                o_ref[pl.ds(q0, ATT_TQ), hs] = acc / l
            return carry

        lax.fori_loop(0, nq, q_body, 0)


def _attn_prompt(qb, kb, vb, c, ct, B, S):
    T = B * S
    seq = pl.BlockSpec((S, ATT_W), lambda b: (b, 0))
    return pl.pallas_call(
        functools.partial(_attn_prompt_kernel, S=S),
        grid=(B,),
        in_specs=[seq, seq, seq,
                  pl.BlockSpec((None, S, LANES), lambda b: (b, 0, 0)),
                  pl.BlockSpec((None, N_HEADS, S // ATT_TK, ATT_TK), lambda b: (b, 0, 0, 0))],
        out_specs=seq,
        out_shape=jax.ShapeDtypeStruct((T, ATT_W), F32),
        compiler_params=_params(),
        name="fox_prompt",
    )(qb, kb, vb, c, ct)


def _attn_sample_kernel(q_ref, kn_ref, vn_ref, ck_ref, cv_ref, c_ref, ct_ref, o_ref, *, P, Tn):
    r = lax.broadcasted_iota(I32, (Tn, Tn), 0)
    c = lax.broadcasted_iota(I32, (Tn, Tn), 1)
    nt = (((1,), (1,)), ((), ()))
    for h in range(N_HEADS):
        hs = slice(h * HEAD_DIM, (h + 1) * HEAD_DIM)
        q = q_ref[:, hs]
        kc = ck_ref[pl.ds(h, P, stride=N_HEADS), :].astype(BF16)
        vc = cv_ref[pl.ds(h, P, stride=N_HEADS), :].astype(BF16)
        cq = c_ref[P:P + Tn, h:h + 1]
        ckr = ct_ref[h:h + 1, :]
        s_c = lax.dot_general(q, kc, nt, preferred_element_type=F32) * ATT_SCALE
        s_c = s_c + cq - ckr[:, :P]
        s_n = lax.dot_general(q, kn_ref[:, hs], nt, preferred_element_type=F32) * ATT_SCALE
        s_n = jnp.where(r >= c, s_n + cq - ckr[:, P:P + Tn], -jnp.inf)
        m = jnp.maximum(jnp.max(s_c, axis=1, keepdims=True), jnp.max(s_n, axis=1, keepdims=True))
        p_c = jnp.exp(s_c - m)
        p_n = jnp.exp(s_n - m)
        l = jnp.sum(p_c, axis=1, keepdims=True) + jnp.sum(p_n, axis=1, keepdims=True)
        o = jnp.dot(p_c.astype(BF16), vc, preferred_element_type=F32)
        o = o + jnp.dot(p_n.astype(BF16), vn_ref[:, hs], preferred_element_type=F32)
        o_ref[:, hs] = o / l


def _attn_sample(qb, kb, vb, cache_k, cache_v, layer, c, ct, row0, Bd, P, Tn):
    blk0 = row0 // Tn
    new = pl.BlockSpec((Tn, ATT_W), lambda b: (blk0 + b, 0))
    cache = pl.BlockSpec((None, None, P * N_HEADS, HEAD_DIM), lambda b: (layer, b, 0, 0))
    return pl.pallas_call(
        functools.partial(_attn_sample_kernel, P=P, Tn=Tn),
        grid=(Bd,),
        in_specs=[new, new, new, cache, cache,
                  pl.BlockSpec((None, P + Tn, LANES), lambda b: (b, 0, 0)),
                  pl.BlockSpec((None, N_HEADS, P + Tn), lambda b: (b, 0, 0))],
        out_specs=pl.BlockSpec((Tn, ATT_W), lambda b: (b, 0)),
        out_shape=jax.ShapeDtypeStruct((Bd * Tn, ATT_W), F32),
        compiler_params=_params(),
        name="fox_sample",
    )(qb, kb, vb, cache_k, cache_v, c, ct)


def _lru_kernel(xr_ref, xg_ref, cs_ref, h0_ref, cw_ref, cb_ref, wr_ref, br_ref, wi_ref, bi_ref,
                lam_ref, o_ref, cso_ref, hl_ref, xp_ref, a_ref, u_ref, hs_ref, h_ref, *, ts):
    s = pl.program_id(1)
    pad = SUBLANES
    nprev = CONV_W - 1

    @pl.when(s == 0)
    def _():
        xp_ref[pad - nprev:pad, :] = cs_ref[...]
        h_ref[...] = h0_ref[...]

    x = xr_ref[...]
    xp_ref[pad:pad + ts, :] = x
    xc = cb_ref[...] + cw_ref[nprev:nprev + 1, :] * x
    for i in range(nprev):
        xc = xc + cw_ref[i:i + 1, :] * xp_ref[pad - nprev + i:pad - nprev + i + ts, :]
    tail = xp_ref[pad + ts - nprev:pad + ts, :]
    xp_ref[pad - nprev:pad, :] = tail
    cso_ref[...] = tail

    xcb = xc.astype(BF16)

    def block_diag(w_ref, b_ref):
        parts = [jnp.dot(xcb[:, n * LRU_BW:(n + 1) * LRU_BW], w_ref[n], preferred_element_type=F32)
                 for n in range(LRU_BLOCKS)]
        return jnp.concatenate(parts, axis=1) + b_ref[...]

    r = jax.nn.sigmoid(block_diag(wr_ref, br_ref))
    g = jax.nn.sigmoid(block_diag(wi_ref, bi_ref))
    nl = -lam_ref[...]
    softplus = jnp.maximum(nl, 0.0) + jnp.log1p(jnp.exp(-jnp.abs(nl)))
    log_a = (-LRU_C) * r * softplus
    a = jnp.exp(log_a)
    u = jnp.sqrt(1.0 - a * a) * g * xc

    sub = lax.broadcasted_iota(I32, (ts, LRU_W), 0) % SUBLANES
    d = 1
    while d < SUBLANES:
        keep = sub >= d
        u = jnp.where(keep, u + a * pltpu.roll(u, d, axis=0), u)
        a = jnp.where(keep, a * pltpu.roll(a, d, axis=0), a)
        d *= 2
    a_ref[...] = a
    u_ref[...] = u

    def group(gi, hprev):
        r0 = pl.multiple_of(gi * SUBLANES, SUBLANES)
        hh = u_ref[pl.ds(r0, SUBLANES), :] + a_ref[pl.ds(r0, SUBLANES), :] * hprev
        hs_ref[pl.ds(r0, SUBLANES), :] = hh
        return hh[SUBLANES - 1:SUBLANES, :]

    hlast = lax.fori_loop(0, ts // SUBLANES, group, h_ref[...])
    h_ref[...] = hlast
    hl_ref[...] = hlast
    o_ref[...] = hs_ref[...] * _gelu_tanh(xg_ref[...])


def _lru(xr, xg, conv_state, h0, lw, row0, nb, S, ts):
    ns = S // ts
    blk0 = row0 // ts
    rows = pl.BlockSpec((ts, LRU_W), lambda b, s: (blk0 + b * ns + s, 0))
    vec = lambda n: _resident((n, LRU_W))
    per_seq = lambda n: pl.BlockSpec((None, n, LRU_W), lambda b, s: (b, 0, 0))
    in_specs = [rows, rows, per_seq(CONV_W - 1), per_seq(1), vec(CONV_W), vec(1),
                _resident((LRU_BLOCKS, LRU_BW, LRU_BW)), vec(1),
                _resident((LRU_BLOCKS, LRU_BW, LRU_BW)), vec(1), vec(1)]
    args = [xr, xg, conv_state, h0, lw["conv_w"], lw["conv_b"], lw["w_r"], lw["b_r"],
            lw["w_i"], lw["b_i"], lw["lam"]]
    return pl.pallas_call(
        functools.partial(_lru_kernel, ts=ts),
        grid=(nb, ns),
        in_specs=in_specs,
        out_specs=[pl.BlockSpec((ts, LRU_W), lambda b, s: (b * ns + s, 0)),
                   per_seq(CONV_W - 1), per_seq(1)],
        out_shape=[jax.ShapeDtypeStruct((nb * S, LRU_W), F32),
                   jax.ShapeDtypeStruct((nb, CONV_W - 1, LRU_W), F32),
                   jax.ShapeDtypeStruct((nb, 1, LRU_W), F32)],
        scratch_shapes=[pltpu.VMEM((SUBLANES + ts, LRU_W), F32),
                        pltpu.VMEM((ts, LRU_W), F32), pltpu.VMEM((ts, LRU_W), F32),
                        pltpu.VMEM((ts, LRU_W), F32), pltpu.VMEM((1, LRU_W), F32)],
        compiler_params=_params(2),
        name="rg_lru",
    )(*args)


def _outproj_kernel(oap_ref, oas_ref, olp_ref, ols_ref, x_ref, w_ref, ga_ref, gl_ref, lg_ref,
                    lb_ref, wrh_ref, wrl_ref, rb_ref, x1_ref, x1s_ref, idx_ref, gate_ref,
                    *, alpha, n_prompt_blocks):
    is_prompt = pl.program_id(0) < n_prompt_blocks
    oa = jnp.where(is_prompt, oap_ref[...], oas_ref[...])
    ol = jnp.where(is_prompt, olp_ref[...], ols_ref[...])
    na = _rms_norm(oa, ga_ref[...]).astype(BF16)
    nl = _rms_norm(ol, gl_ref[...]).astype(BF16)
    m = jnp.dot(na, w_ref[0:ATT_W, :], preferred_element_type=F32)
    m = m + jnp.dot(nl, w_ref[ATT_W:D_MODEL, :], preferred_element_type=F32)
    x1 = _layer_norm(alpha * x_ref[...] + m, lg_ref[...], lb_ref[...])
    x1_ref[...] = x1
    _matrix_to_rows(x1s_ref, x1, x1.shape[0])

    xh = x1.astype(BF16)
    xl = (x1 - xh.astype(F32)).astype(BF16)
    logits = jnp.dot(xh, wrh_ref[...], preferred_element_type=F32)
    logits = logits + jnp.dot(xl, wrh_ref[...], preferred_element_type=F32)
    logits = logits + jnp.dot(xh, wrl_ref[...], preferred_element_type=F32)
    scores = jax.nn.sigmoid(logits)
    tm = scores.shape[0]
    lane = lax.broadcasted_iota(I32, (tm, LANES), 1)
    lane_f = lane.astype(F32)
    sel = jnp.where(lane < N_EXPERTS, scores + rb_ref[...], -jnp.inf)
    idx_out = jnp.zeros((tm, LANES), F32)
    gate_out = jnp.zeros((tm, LANES), F32)
    gsum = jnp.zeros((tm, 1), F32)
    for k in range(TOP_K):
        mx = jnp.max(sel, axis=1, keepdims=True)
        ik = jnp.min(jnp.where(sel == mx, lane_f, float(LANES)), axis=1, keepdims=True)
        hit = lane_f == ik
        gk = jnp.sum(jnp.where(hit, scores, 0.0), axis=1, keepdims=True)
        sel = jnp.where(hit, -jnp.inf, sel)
        idx_out = jnp.where(lane == k, ik, idx_out)
        gate_out = jnp.where(lane == k, gk, gate_out)
        gsum = gsum + gk
    idx_ref[...] = idx_out.astype(I32)
    gate_ref[...] = gate_out / gsum * ROUTE_SCALE


def _outproj(oa_p, oa_s, ol_p, ol_s, x, lw, alpha):
    T = x.shape[0]
    n_p = oa_p.shape[0] // TM
    row = lambda w: pl.BlockSpec((TM, w), lambda i: (i, 0))
    prompt = lambda w: pl.BlockSpec((TM, w), lambda i: (jnp.minimum(i, n_p - 1), 0))
    sample = lambda w: pl.BlockSpec((TM, w), lambda i: (jnp.maximum(i - n_p, 0), 0))
    vec = lambda n: _resident((1, n))
    return pl.pallas_call(
        functools.partial(_outproj_kernel, alpha=alpha, n_prompt_blocks=n_p),
        grid=(T // TM,),
        in_specs=[prompt(ATT_W), sample(ATT_W), prompt(LRU_W), sample(LRU_W), row(D_MODEL),
                  _resident((D_MODEL, D_MODEL)),
                  vec(ATT_W), vec(LRU_W), vec(D_MODEL), vec(D_MODEL),
                  _resident((D_MODEL, LANES)), _resident((D_MODEL, LANES)), vec(LANES)],
        out_specs=[row(D_MODEL), pl.BlockSpec((TM * SLAB_ROWS, LANES), lambda i: (i, 0)),
                   row(LANES), row(LANES)],
        out_shape=[jax.ShapeDtypeStruct((T, D_MODEL), F32),
                   jax.ShapeDtypeStruct((T * SLAB_ROWS, LANES), U32),
                   jax.ShapeDtypeStruct((T, LANES), I32),
                   jax.ShapeDtypeStruct((T, LANES), F32)],
        compiler_params=_params(),
        name="outproj_ln_router",
    )(oa_p, oa_s, ol_p, ol_s, x, lw["w_out"], lw["g_att"], lw["g_lru"], lw["ln1_g"], lw["ln1_b"],
      lw["wr_hi"], lw["wr_lo"], lw["b_router"])


def _rank_kernel(idx_ref, dest_ref, cnt_ref, carry_ref, start_ref):
    phase = pl.program_id(0)
    i = pl.program_id(1)

    @pl.when((phase == 0) & (i == 0))
    def _():
        carry_ref[...] = jnp.zeros_like(carry_ref)

    tm = idx_ref.shape[0]
    idx = idx_ref[...]
    lane = lax.broadcasted_iota(I32, (tm, LANES), 1)
    onehot = jnp.zeros((tm, LANES), F32)
    for k in range(TOP_K):
        onehot = onehot + (lane == idx[:, k:k + 1]).astype(F32)
    colsum = jnp.sum(onehot, axis=0, keepdims=True)

    @pl.when(phase == 0)
    def _():
        carry_ref[...] = carry_ref[...] + colsum
        dest_ref[...] = jnp.zeros_like(dest_ref)

    @pl.when((phase == 1) & (i == 0))
    def _():
        counts = carry_ref[...]
        cnt_ref[...] = counts.astype(I32)
        padded = jnp.floor((counts + (MOE_BLK - 1)) * (1.0 / MOE_BLK)) * MOE_BLK
        r = lax.broadcasted_iota(I32, (LANES, LANES), 0)
        c = lax.broadcasted_iota(I32, (LANES, LANES), 1)
        before = (r < c).astype(BF16)
        start = jnp.zeros((SUBLANES, LANES), F32)
        for term in _split3(jnp.broadcast_to(padded, (SUBLANES, LANES))):
            start = start + jnp.dot(term, before, preferred_element_type=F32)
        start_ref[...] = start[0:1, :]
        carry_ref[...] = jnp.zeros_like(carry_ref)

    @pl.when(phase == 1)
    def _():
        r = lax.broadcasted_iota(I32, (tm, tm), 0)
        c = lax.broadcasted_iota(I32, (tm, tm), 1)
        below = (r > c).astype(BF16)
        entry = jnp.dot(below, onehot.astype(BF16), preferred_element_type=F32)
        entry = entry + carry_ref[...] + start_ref[...]
        out = jnp.zeros((tm, LANES), F32)
        for k in range(TOP_K):
            ek = jnp.sum(jnp.where(lane == idx[:, k:k + 1], entry, 0.0), axis=1, keepdims=True)
            out = jnp.where(lane == k, ek, out)
        dest_ref[...] = out.astype(I32)
        carry_ref[...] = carry_ref[...] + colsum


def _rank(idx):
    T = idx.shape[0]
    row = pl.BlockSpec((TM, LANES), lambda p, i: (i, 0))
    return pl.pallas_call(
        _rank_kernel,
        grid=(2, T // TM),
        in_specs=[row],
        out_specs=[pl.BlockSpec((TM, LANES), lambda p, i: (i * p, 0)),
                   pl.BlockSpec((1, LANES), lambda p, i: (0, 0))],
        out_shape=[jax.ShapeDtypeStruct((T, LANES), I32), jax.ShapeDtypeStruct((1, LANES), I32)],
        scratch_shapes=[pltpu.VMEM((1, LANES), F32), pltpu.VMEM((1, LANES), F32)],
        compiler_params=_params(2),
        name="route_rank",
    )(idx)


def _dispatch_kernel(dest_ref, zst_ref, zcnt_ref, nu_ref, x_ref, xs_hbm, zblk, sem_o, sem_z,
                     *, n_blocks):
    i = pl.program_id(0)
    tm = x_ref.shape[0] // SLAB_ROWS
    blk_rows = MOE_BLK * SLAB_ROWS

    @pl.when(i == 0)
    def _():
        zblk[...] = jnp.zeros_like(zblk)
        zrow = zblk.at[pl.ds(0, SLAB_ROWS)]

        def pad_copy(slot):
            dst = xs_hbm.at[pl.ds(pl.multiple_of(slot * SLAB_ROWS, SLAB_ROWS), SLAB_ROWS)]
            return pltpu.make_async_copy(zrow, dst, sem_z)

        def block_copy(g):
            dst = xs_hbm.at[pl.ds(pl.multiple_of(g * blk_rows, blk_rows), blk_rows)]
            return pltpu.make_async_copy(zblk, dst, sem_z)

        def start_expert(e, carry):
            z0 = zst_ref[e]
            return lax.fori_loop(0, zcnt_ref[e], lambda j, c: (pad_copy(z0 + j).start(), c)[1], carry)

        def wait_expert(e, carry):
            return lax.fori_loop(0, zcnt_ref[e], lambda j, c: (pad_copy(0).wait(), c)[1], carry)

        lax.fori_loop(0, N_EXPERTS, start_expert, 0)
        lax.fori_loop(nu_ref[0], n_blocks, lambda g, c: (block_copy(g).start(), c)[1], 0)
        lax.fori_loop(0, N_EXPERTS, wait_expert, 0)
        lax.fori_loop(nu_ref[0], n_blocks, lambda g, c: (block_copy(g).wait(), c)[1], 0)

    base = i * (tm * TOP_K)

    def per_token(t, carry):
        src = x_ref.at[pl.ds(pl.multiple_of(t * SLAB_ROWS, SLAB_ROWS), SLAB_ROWS)]
        for k in range(TOP_K):
            slot = dest_ref[base + t * TOP_K + k]
            dst = xs_hbm.at[pl.ds(pl.multiple_of(slot * SLAB_ROWS, SLAB_ROWS), SLAB_ROWS)]
            pltpu.make_async_copy(src, dst, sem_o).start(priority=k % DMA_QUEUES)
        return carry

    lax.fori_loop(0, tm, per_token, 0)
    for _ in range(TOP_K):
        pltpu.make_async_copy(x_ref, xs_hbm.at[pl.ds(0, tm * SLAB_ROWS)], sem_o).wait()


def _dispatch(dest_flat, zst, zcnt, n_used, x1s, n_blocks):
    T = x1s.shape[0] // SLAB_ROWS
    grid_spec = pltpu.PrefetchScalarGridSpec(
        num_scalar_prefetch=4,
        grid=(T // TM,),
        in_specs=[pl.BlockSpec((TM * SLAB_ROWS, LANES), lambda i, *_: (i, 0))],
        out_specs=pl.BlockSpec(memory_space=pl.ANY),
        scratch_shapes=[pltpu.VMEM((MOE_BLK * SLAB_ROWS, LANES), U32),
                        pltpu.SemaphoreType.DMA(()), pltpu.SemaphoreType.DMA(())],
    )
    return pl.pallas_call(
        functools.partial(_dispatch_kernel, n_blocks=n_blocks),
        grid_spec=grid_spec,
        out_shape=jax.ShapeDtypeStruct((n_blocks * MOE_BLK * SLAB_ROWS, LANES), U32),
        compiler_params=_params(),
        name="moe_dispatch",
    )(dest_flat, zst, zcnt, n_used, x1s)


def _rows_to_matrix(ref, n):
    cols = []
    for r in range(SLAB_ROWS):
        w = ref[pl.ds(r, n, stride=SLAB_ROWS), :]
        cols.append(pltpu.bitcast(w << 16, F32))
        cols.append(pltpu.bitcast(w & jnp.uint32(0xFFFF0000), F32))
    return jnp.concatenate(cols, axis=1)


def _matrix_to_rows(ref, y, n):
    for r in range(SLAB_ROWS):
        lo = y[:, (2 * r) * LANES:(2 * r + 1) * LANES].astype(BF16).astype(F32)
        hi = y[:, (2 * r + 1) * LANES:(2 * r + 2) * LANES].astype(BF16).astype(F32)
        ref[pl.ds(r, n, stride=SLAB_ROWS), :] = (
            pltpu.bitcast(hi, U32) | (pltpu.bitcast(lo, U32) >> 16))


def _experts_kernel(first_ref, nu_ref, xs_hbm, wg_ref, wu_ref, wd_ref, ys_hbm,
                    wg_s, wu_s, wd_s, xbuf, ybuf, gsem, ssem, *, n_blocks):
    e = pl.program_id(0)
    nu = nu_ref[0]
    blk_rows = MOE_BLK * SLAB_ROWS
    def rows_copy(g):
        slot = g % ROW_RING
        src = xs_hbm.at[pl.ds(pl.multiple_of(g * blk_rows, blk_rows), blk_rows)]
        return pltpu.make_async_copy(src, xbuf.at[slot], gsem.at[slot])

    def result_copy(g, slot):
        dst = ys_hbm.at[pl.ds(pl.multiple_of(g * blk_rows, blk_rows), blk_rows)]
        return pltpu.make_async_copy(ybuf.at[slot], dst, ssem.at[slot])

    @pl.when(e == 0)
    def _():
        rows_copy(0).start()

        @pl.when(nu > 1)
        def _():
            rows_copy(1).start()

    g0 = first_ref[e]
    g1 = first_ref[e + 1]

    @pl.when(g1 > g0)
    def _():
        wg_s[...] = wg_ref[...].astype(BF16)
        wu_s[...] = wu_ref[...].astype(BF16)
        wd_s[...] = wd_ref[...].astype(BF16)

    def block(g, carry):
        p = g % 2
        xb_cur = xbuf.at[g % ROW_RING]
        yb_cur = ybuf.at[p]
        rows_copy(g).wait()

        @pl.when(g >= 2)
        def _():
            result_copy(g - 2, p).wait()

        @pl.when(g + 2 < nu)
        def _():
            rows_copy(g + 2).start()

        x = _rows_to_matrix(xb_cur, MOE_BLK).astype(BF16)
        gt = jnp.dot(x, wg_s[...], preferred_element_type=F32)
        up = jnp.dot(x, wu_s[...], preferred_element_type=F32)
        h = (_silu(gt) * up).astype(BF16)
        y = jnp.dot(h, wd_s[...], preferred_element_type=F32)
        _matrix_to_rows(yb_cur, y, MOE_BLK)
        result_copy(g, p).start()

        @pl.when(g == nu - 1)
        def _():
            @pl.when(g >= 1)
            def _():
                result_copy(g - 1, 1 - p).wait()

            result_copy(g, p).wait()
            yb_cur[...] = jnp.zeros_like(yb_cur)

            def fill_start(gb, c):
                result_copy(gb, p).start()
                return c

            def fill_wait(gb, c):
                result_copy(gb, p).wait()
                return c

            lax.fori_loop(nu, n_blocks, fill_start, 0)
            lax.fori_loop(nu, n_blocks, fill_wait, 0)

        return carry

    lax.fori_loop(g0, g1, block, 0)


def _experts(first_blk, n_used, xs, we_gate, we_up, we_down, layer):
    blk_rows = MOE_BLK * SLAB_ROWS
    n_blocks = xs.shape[0] // blk_rows
    any_spec = pl.BlockSpec(memory_space=pl.ANY)
    w_in_spec = pl.BlockSpec((None, None, D_MODEL, D_EXPERT), lambda e, *_: (layer, e, 0, 0))
    grid_spec = pltpu.PrefetchScalarGridSpec(
        num_scalar_prefetch=2,
        grid=(N_EXPERTS,),
        in_specs=[any_spec, w_in_spec, w_in_spec,
                  pl.BlockSpec((None, None, D_EXPERT, D_MODEL), lambda e, *_: (layer, e, 0, 0))],
        out_specs=any_spec,
        scratch_shapes=[pltpu.VMEM((D_MODEL, D_EXPERT), BF16), pltpu.VMEM((D_MODEL, D_EXPERT), BF16),
                        pltpu.VMEM((D_EXPERT, D_MODEL), BF16),
                        pltpu.VMEM((ROW_RING, blk_rows, LANES), U32),
                        pltpu.VMEM((2, blk_rows, LANES), U32),
                        pltpu.SemaphoreType.DMA((ROW_RING,)), pltpu.SemaphoreType.DMA((2,))],
    )
    return pl.pallas_call(
        functools.partial(_experts_kernel, n_blocks=n_blocks),
        grid_spec=grid_spec,
        out_shape=jax.ShapeDtypeStruct((n_blocks * blk_rows, LANES), U32),
        compiler_params=_params(),
        name="moe_experts",
    )(first_blk, n_used, xs, we_gate, we_up, we_down)


def _combine_kernel(dest_ref, ys_hbm, gate_ref, x_ref, wsg_ref, wsu_ref, wsd_ref, lg_ref, lb_ref,
                    o_ref, buf0, buf1, gsem, *, alpha):
    i = pl.program_id(0)
    tm = x_ref.shape[0]
    bufs = (buf0, buf1)

    def gather_rows(step, buf, sem):
        base = step * (tm * TOP_K)

        def per_token(t, carry):
            d0 = pl.multiple_of(t * SLAB_ROWS, SLAB_ROWS)
            for k in range(TOP_K):
                slot = dest_ref[base + t * TOP_K + k]
                src = ys_hbm.at[pl.ds(pl.multiple_of(slot * SLAB_ROWS, SLAB_ROWS), SLAB_ROWS)]
                pltpu.make_async_copy(src, buf.at[k, pl.ds(d0, SLAB_ROWS)], sem).start(
                    priority=k % DMA_QUEUES)
            return carry

        lax.fori_loop(0, tm, per_token, 0)

    @pl.when(i == 0)
    def _():
        gather_rows(0, buf0, gsem.at[0])

    def step(p):
        @pl.when(i + 1 < pl.num_programs(0))
        def _():
            gather_rows(i + 1, bufs[1 - p], gsem.at[1 - p])

        x1 = x_ref[...]
        xb = x1.astype(BF16)
        hs = _silu(jnp.dot(xb, wsg_ref[...], preferred_element_type=F32))
        hs = (hs * jnp.dot(xb, wsu_ref[...], preferred_element_type=F32)).astype(BF16)
        acc = jnp.dot(hs, wsd_ref[...], preferred_element_type=F32)
        for k in range(TOP_K):
            pltpu.make_async_copy(ys_hbm.at[pl.ds(0, tm * SLAB_ROWS)], bufs[p].at[k],
                                  gsem.at[p]).wait()
        gates = gate_ref[...]
        for k in range(TOP_K):
            acc = acc + gates[:, k:k + 1] * _rows_to_matrix(bufs[p].at[k], tm)
        o_ref[...] = _layer_norm(alpha * x1 + acc, lg_ref[...], lb_ref[...])

    @pl.when(i % 2 == 0)
    def _():
        step(0)

    @pl.when(i % 2 == 1)
    def _():
        step(1)


def _combine(dest_flat, ys, gates, x1, lw, alpha):
    T = x1.shape[0]
    tm = TM_COMBINE
    row = lambda w: pl.BlockSpec((tm, w), lambda i, *_: (i, 0))
    res = _resident
    grid_spec = pltpu.PrefetchScalarGridSpec(
        num_scalar_prefetch=1,
        grid=(T // tm,),
        in_specs=[pl.BlockSpec(memory_space=pl.ANY), row(LANES), row(D_MODEL),
                  res((D_MODEL, D_SHARED)), res((D_MODEL, D_SHARED)), res((D_SHARED, D_MODEL)),
                  res((1, D_MODEL)), res((1, D_MODEL))],
        out_specs=row(D_MODEL),
        scratch_shapes=[pltpu.VMEM((TOP_K, tm * SLAB_ROWS, LANES), U32),
                        pltpu.VMEM((TOP_K, tm * SLAB_ROWS, LANES), U32),
                        pltpu.SemaphoreType.DMA((2,))],
    )
    return pl.pallas_call(
        functools.partial(_combine_kernel, alpha=alpha),
        grid_spec=grid_spec,
        out_shape=jax.ShapeDtypeStruct((T, D_MODEL), F32),
        compiler_params=_params(),
        name="moe_combine",
    )(dest_flat, ys, gates, x1, lw["ws_gate"], lw["ws_up"], lw["ws_down"], lw["ln2_g"],
      lw["ln2_b"])


def _block_tables(counts):
    padded = (counts + MOE_BLK - 1) // MOE_BLK * MOE_BLK
    pad_end = jnp.cumsum(padded)
    pad_start = pad_end - padded
    first_blk = jnp.concatenate([pad_start, pad_end[-1:]]) // MOE_BLK
    n_used = pad_end[-1:] // MOE_BLK
    pad_first = pad_start + counts
    pad_count = padded - counts
    return (first_blk.astype(I32), n_used.astype(I32), pad_first.astype(I32),
            pad_count.astype(I32))


def _moe(x1, x1s, idx, gates, lw, we_gate, we_up, we_down, layer, alpha):
    T = x1.shape[0]
    n_blocks = T * TOP_K // MOE_BLK + N_EXPERTS
    dest, cnt = _rank(idx)
    first_blk, n_used, pad_first, pad_count = _block_tables(cnt[0, :N_EXPERTS])
    dest_flat = dest[:, :TOP_K].reshape(-1)
    xs = _dispatch(dest_flat, pad_first, pad_count, n_used, x1s, n_blocks)
    ys = _experts(first_blk, n_used, xs, we_gate, we_up, we_down, layer)
    return _combine(dest_flat, ys, gates, x1, lw, alpha)


def _prep_layer(l, w_in, b_f, w_out, g_att, g_lru, conv_w, conv_b, w_r, b_r, w_i, b_i, lru_lambda,
                ln1_g, ln1_b, w_router, b_router, ws_gate, ws_up, ws_down, ln2_g, ln2_b):
    qkv_end = 3 * ATT_W
    wl = w_in[l]
    w_pad = jnp.concatenate(
        [wl[:, :qkv_end], wl[:, qkv_end + N_HEADS:],
         jnp.pad(wl[:, qkv_end:qkv_end + N_HEADS], ((0, 0), (0, LANES - N_HEADS)))],
        axis=1).astype(BF16)
    wr = jnp.pad(w_router[l], ((0, 0), (0, LANES - N_EXPERTS)))
    wr_hi = wr.astype(BF16)
    return dict(
        w_pad=w_pad,
        bf_pad=jnp.pad(b_f[l], (0, LANES - N_HEADS)).reshape(1, LANES),
        w_out=w_out[l].astype(BF16),
        g_att=g_att[l].reshape(1, ATT_W), g_lru=g_lru[l].reshape(1, LRU_W),
        conv_w=conv_w[l], conv_b=conv_b[l].reshape(1, LRU_W),
        w_r=w_r[l].astype(BF16), b_r=b_r[l].reshape(1, LRU_W),
        w_i=w_i[l].astype(BF16), b_i=b_i[l].reshape(1, LRU_W),
        lam=lru_lambda[l].reshape(1, LRU_W),
        ln1_g=ln1_g[l].reshape(1, D_MODEL), ln1_b=ln1_b[l].reshape(1, D_MODEL),
        wr_hi=wr_hi, wr_lo=(wr - wr_hi.astype(F32)).astype(BF16),
        b_router=jnp.pad(b_router[l], (0, LANES - N_EXPERTS)).reshape(1, LANES),
        ws_gate=ws_gate[l].astype(BF16), ws_up=ws_up[l].astype(BF16),
        ws_down=ws_down[l].astype(BF16),
        ln2_g=ln2_g[l].reshape(1, D_MODEL), ln2_b=ln2_b[l].reshape(1, D_MODEL),
    )


def kernel(x_prompt, x_sample, cache_k, cache_v, cache_logf, state_conv, state_lru, ln_in_g, ln_in_b, w_in, b_f, w_out, g_att, g_lru, conv_w, conv_b, w_r, b_r, w_i, b_i, lru_lambda, ln1_g, ln1_b, w_router, b_router, we_gate, we_up, we_down, ws_gate, ws_up, ws_down, ln2_g, ln2_b):
    B, S, _ = x_prompt.shape
    Bd, Tn, _ = x_sample.shape
    depth = w_in.shape[0]
    P = cache_k.shape[2]
    Tp = B * S
    Ts = Bd * Tn
    alpha = (2 * depth) ** 0.25

    xs = (x_prompt.reshape(Tp, D_MODEL), x_sample.reshape(Ts, D_MODEL))
    ln_g = ln_in_g.reshape(1, D_MODEL)
    ln_b = ln_in_b.reshape(1, D_MODEL)
    zero_conv = jnp.zeros((B, CONV_W - 1, LRU_W), F32)
    zero_h = jnp.zeros((B, 1, LRU_W), F32)

    ks, vs, lfs, p_conv, p_lru, s_conv, s_lru = [], [], [], [], [], [], []
    for l in range(depth):
        lw = _prep_layer(l, w_in, b_f, w_out, g_att, g_lru, conv_w, conv_b, w_r, b_r, w_i, b_i,
                         lru_lambda, ln1_g, ln1_b, w_router, b_router, ws_gate, ws_up, ws_down,
                         ln2_g, ln2_b)
        outs = _inproj(xs if l == 0 else (x,), ln_g, ln_b, lw["w_pad"], lw["bf_pad"])
        if l == 0:
            x, outs = outs[0], outs[1:]
        qb, k, v, kb, vb, xr, xg, lf = outs

        c_p = _cumsum(lf[:Tp].reshape(B, S, LANES))
        ct_p = c_p[:, :, :N_HEADS].transpose(0, 2, 1).reshape(B, N_HEADS, S // ATT_TK, ATT_TK)
        lf_cache = jnp.pad(cache_logf[l], ((0, 0), (0, 0), (0, LANES - N_HEADS)))
        c_s = _cumsum(jnp.concatenate([lf_cache, lf[Tp:].reshape(Bd, Tn, LANES)], axis=1))
        ct_s = c_s[:, :, :N_HEADS].transpose(0, 2, 1)

        oa_p = _attn_prompt(qb, kb, vb, c_p, ct_p, B, S)
        oa_s = _attn_sample(qb, kb, vb, cache_k.reshape(depth, Bd, P * N_HEADS, HEAD_DIM),
                            cache_v.reshape(depth, Bd, P * N_HEADS, HEAD_DIM), l, c_s, ct_s,
                            Tp, Bd, P, Tn)

        ol_p, pc, ph = _lru(xr, xg, zero_conv, zero_h, lw, 0, B, S, LRU_TS)
        ol_s, sc, sh = _lru(xr, xg, state_conv[l], state_lru[l].reshape(Bd, 1, LRU_W), lw,
                            Tp, Bd, Tn, Tn)

        x1, x1s, idx, gates = _outproj(oa_p, oa_s, ol_p, ol_s, x, lw, alpha)
        x = _moe(x1, x1s, idx, gates, lw, we_gate, we_up, we_down, l, alpha)

        ks.append(k)
        vs.append(v)
        lfs.append(lf[:, :N_HEADS])
        p_conv.append(pc)
        p_lru.append(ph.reshape(B, LRU_W))
        s_conv.append(sc)
        s_lru.append(sh.reshape(Bd, LRU_W))

    def split(ts, tail):
        p = jnp.stack([t[:Tp].reshape((B, S) + tail) for t in ts])
        s = jnp.stack([t[Tp:].reshape((Bd, Tn) + tail) for t in ts])
        return p, s

    p_k, s_k = split(ks, (N_HEADS, HEAD_DIM))
    p_v, s_v = split(vs, (N_HEADS, HEAD_DIM))
    p_logf, s_logf = split(lfs, (N_HEADS,))
    y_prompt = x[:Tp].reshape(B, S, D_MODEL)
    y_sample = x[Tp:].reshape(Bd, Tn, D_MODEL)
    return (y_prompt, y_sample, p_k, p_v, p_logf, jnp.stack(p_conv), jnp.stack(p_lru),
            s_k, s_v, s_logf, jnp.stack(s_conv), jnp.stack(s_lru))
```
